```python
import math
import jax, jax.numpy as jnp
from jax import lax
import numpy as np

D_MODEL = 1024
BATCH = 32
SEQ = 2048
DEPTH = 1

HEAD_DIM = 64
DIFF_HEADS = 4
DIFF_QK = 2 * HEAD_DIM
DIFF_V = 2 * HEAD_DIM
DIFF_WIDTH = DIFF_HEADS * DIFF_V
SWA_Q_HEADS = 8
SWA_KV_HEADS = 2
SWA_GROUP = SWA_Q_HEADS // SWA_KV_HEADS
SWA_WIDTH = SWA_Q_HEADS * HEAD_DIM
WINDOW = 128
BLOCK = 128
MIX_WIDTH = DIFF_WIDTH + SWA_WIDTH
N_ATT_HEADS = SWA_Q_HEADS + DIFF_HEADS
COL_DQ = DIFF_HEADS * DIFF_QK
COL_DK = DIFF_HEADS * DIFF_QK
COL_DV = DIFF_HEADS * DIFF_V
COL_SQ = SWA_Q_HEADS * HEAD_DIM
COL_SK = SWA_KV_HEADS * HEAD_DIM
COL_SV = SWA_KV_HEADS * HEAD_DIM
IN_WIDTH = COL_DQ + COL_DK + COL_DV + COL_SQ + COL_SK + COL_SV
SPLITS = (COL_DQ, COL_DQ + COL_DK, COL_DQ + COL_DK + COL_DV, COL_DQ + COL_DK + COL_DV + COL_SQ, COL_DQ + COL_DK + COL_DV + COL_SQ + COL_SK)
PEER_HEADS = 8
N_KEYS = 128
N_EXPERTS = N_KEYS * N_KEYS
PEER_QUERY = 256
PEER_HALF = PEER_QUERY // 2
PEER_TOPK = 16
PEER_CHUNK = 128
LN_EPS = 1e-5
DEEPNORM_ALPHA = (2 * DEPTH) ** 0.25
DEEPNORM_BETA = (8 * DEPTH) ** -0.25
NEG_INF = -1e30

kernel_name = "hybrid_diffattn_swa_peer_deepnorm"


def layer_norm(x, g, b):
    xf = x.astype(jnp.float32)
    mu = jnp.mean(xf, axis=-1, keepdims=True)
    var = jnp.mean(jnp.square(xf - mu), axis=-1, keepdims=True)
    return ((xf - mu) * lax.rsqrt(var + LN_EPS)).astype(x.dtype) * g + b


def rms_norm(x, g):
    xf = x.astype(jnp.float32)
    return (xf * lax.rsqrt(jnp.mean(xf * xf, axis=-1, keepdims=True) + LN_EPS)).astype(x.dtype) * g


def alibi_slopes():
    i = jnp.arange(1, N_ATT_HEADS + 1, dtype=jnp.float32)
    return jnp.exp2(-8.0 * i / N_ATT_HEADS)


def diff_attention(q, k, v, lq1, lk1, lq2, lk2, subln_g, slopes, lambda_init):
    B_, S_ = q.shape[0], q.shape[1]
    lam = (jnp.exp(jnp.sum((lq1 * lk1).astype(jnp.float32)))
           - jnp.exp(jnp.sum((lq2 * lk2).astype(jnp.float32))) + lambda_init)
    scale = HEAD_DIM ** -0.5
    outs = []
    for blk in range(S_ // BLOCK):
        start, end = blk * BLOCK, (blk + 1) * BLOCK
        qb = q[:, start:end]
        kp = k[:, :end]
        vp = v[:, :end]
        s = jnp.einsum('bqhmd,bkhmd->bhmqk', qb, kp).astype(jnp.float32) * scale
        dist = (jnp.arange(start, end)[:, None] - jnp.arange(end)[None, :]).astype(jnp.float32)
        bias = -slopes[:, None, None, None] * dist[None, None]
        s = jnp.where(dist >= 0, s + bias, NEG_INF)
        p = jax.nn.softmax(s, axis=-1)
        a = p[:, :, 0] - lam * p[:, :, 1]
        outs.append(jnp.einsum('bhqk,bkhe->bqhe', a.astype(v.dtype), vp))
    o = jnp.concatenate(outs, axis=1)
    o = rms_norm(o, subln_g) * (1.0 - lambda_init)
    return o.reshape(B_, S_, DIFF_WIDTH)


def swa_attention(q, k, v, sinks, slopes):
    B_, S_ = q.shape[0], q.shape[1]
    nb = S_ // BLOCK
    qb = q.reshape(B_, nb, BLOCK, SWA_KV_HEADS, SWA_GROUP, HEAD_DIM)

    def banded(z):
        zb = z.reshape(B_, nb, BLOCK, SWA_KV_HEADS, HEAD_DIM)
        prev = jnp.pad(zb, ((0, 0), (1, 0), (0, 0), (0, 0), (0, 0)))[:, :-1]
        return jnp.concatenate([prev, zb], axis=2)

    kb, vb = banded(k), banded(v)
    s = jnp.einsum('bnqhgd,bnkhd->bnhgqk', qb, kb).astype(jnp.float32) * (HEAD_DIM ** -0.5)
    i = jnp.arange(BLOCK)[:, None]
    j = jnp.arange(2 * BLOCK)[None, :]
    dist = i - j + BLOCK
    key_pos = jnp.arange(nb)[:, None, None] * BLOCK - BLOCK + j[None]
    valid = (dist >= 0) & (dist < WINDOW) & (key_pos >= 0)
    sl = slopes.reshape(SWA_KV_HEADS, SWA_GROUP)[:, :, None, None]
    s = s - sl * dist.astype(jnp.float32)
    s = jnp.where(valid[:, None, None], s, NEG_INF)
    sink = sinks.astype(jnp.float32).reshape(SWA_KV_HEADS, SWA_GROUP)[:, :, None, None]
    m = jnp.maximum(jnp.max(s, axis=-1, keepdims=True), sink)
    p = jnp.exp(s - m)
    p = p / (jnp.sum(p, axis=-1, keepdims=True) + jnp.exp(sink - m))
    o = jnp.einsum('bnhgqk,bnkhd->bnqhgd', p.astype(v.dtype), vb)
    return o.reshape(B_, S_, SWA_WIDTH)


def peer(h, w_pq, sub_keys, u_tab, v_tab):
    B_, S_, D = h.shape
    hc = h.reshape(-1, PEER_CHUNK, D)

    def chunk(hx):
        q = (hx @ w_pq).reshape(PEER_CHUNK, PEER_HEADS, 2, PEER_HALF)
        sc = jnp.einsum('thpd,hpnd->thpn', q, sub_keys).astype(jnp.float32)
        v_half, i_half = lax.top_k(sc, PEER_TOPK)
        cand = v_half[:, :, 0, :, None] + v_half[:, :, 1, None, :]
        cand_idx = i_half[:, :, 0, :, None] * N_KEYS + i_half[:, :, 1, None, :]
        cand = cand.reshape(PEER_CHUNK, PEER_HEADS, PEER_TOPK * PEER_TOPK)
        cand_idx = cand_idx.reshape(PEER_CHUNK, PEER_HEADS, PEER_TOPK * PEER_TOPK)
        top_s, top_pos = lax.top_k(cand, PEER_TOPK)
        idx = jnp.take_along_axis(cand_idx, top_pos, axis=-1)
        g = jax.nn.softmax(top_s, axis=-1)
        a = jax.nn.gelu(jnp.einsum('thkd,td->thk', u_tab[idx], hx), approximate=False)
        return jnp.einsum('thk,thkd->td', (g * a).astype(hx.dtype), v_tab[idx])

    return lax.map(chunk, hc).reshape(B_, S_, D)


def setup_inputs(seed: int = 0) -> dict:
    key = jax.random.key(seed)
    ks = jax.random.split(key, 20)
    f32 = jnp.float32

    def nrm(k, shape, s):
        return jax.random.normal(k, shape, f32) * s

    col = jnp.arange(IN_WIDTH)
    is_v = ((col >= SPLITS[1]) & (col < SPLITS[2])) | (col >= SPLITS[4])
    col_scale = jnp.where(is_v, DEEPNORM_BETA, 1.0).astype(f32)
    L, D = DEPTH, D_MODEL
    return {
        "x": nrm(ks[0], (BATCH, SEQ, D), 1.0),
        "c": nrm(ks[1], (BATCH, D), 1.0),
        "w_ada": nrm(ks[2], (L, D, 6 * D), D ** -0.5),
        "b_ada": nrm(ks[3], (L, 6 * D), 0.01),
        "w_in": nrm(ks[4], (L, D, IN_WIDTH), D ** -0.5) * col_scale,
        "lambda_q1": nrm(ks[5], (L, HEAD_DIM), 0.1),
        "lambda_k1": nrm(ks[6], (L, HEAD_DIM), 0.1),
        "lambda_q2": nrm(ks[7], (L, HEAD_DIM), 0.1),
        "lambda_k2": nrm(ks[8], (L, HEAD_DIM), 0.1),
        "subln_g": 1.0 + nrm(ks[9], (L, DIFF_V), 0.02),
        "sinks": nrm(ks[10], (L, SWA_Q_HEADS), 0.5),
        "w_out": nrm(ks[11], (L, MIX_WIDTH, D), MIX_WIDTH ** -0.5 * DEEPNORM_BETA),
        "ln1_g": 1.0 + nrm(ks[12], (L, D), 0.02),
        "ln1_b": nrm(ks[13], (L, D), 0.02),
        "w_pq": nrm(ks[14], (L, D, PEER_HEADS * PEER_QUERY), D ** -0.5),
        "sub_keys": nrm(ks[15], (L, PEER_HEADS, 2, N_KEYS, PEER_HALF), PEER_HALF ** -0.5),
        "u_tab": nrm(ks[16], (L, N_EXPERTS, D), D ** -0.5 * DEEPNORM_BETA),
        "v_tab": nrm(ks[17], (L, N_EXPERTS, D), DEEPNORM_BETA),
        "ln2_g": 1.0 + nrm(ks[18], (L, D), 0.02),
        "ln2_b": nrm(ks[19], (L, D), 0.02),
    }


def reference(x, c, w_ada, b_ada, w_in, lambda_q1, lambda_k1, lambda_q2, lambda_k2, subln_g, sinks, w_out, ln1_g, ln1_b, w_pq, sub_keys, u_tab, v_tab, ln2_g, ln2_b):
    B_, S_, _ = x.shape
    slopes = alibi_slopes()
    swa_slopes = slopes[:SWA_Q_HEADS]
    diff_slopes = slopes[SWA_Q_HEADS:]
    for l in range(DEPTH):
        lambda_init = 0.8 - 0.6 * math.exp(-0.3 * l)
        mod = jax.nn.silu(c) @ w_ada[l] + b_ada[l]
        sh1, sc1, g1, sh2, sc2, g2 = jnp.split(mod[:, None, :], 6, axis=-1)
        h = x * (1.0 + sc1) + sh1
        proj = h @ w_in[l]
        dq, dk, dv, sq, sk, sv = jnp.split(proj, SPLITS, axis=-1)
        diff_out = diff_attention(
            dq.reshape(B_, S_, DIFF_HEADS, 2, HEAD_DIM),
            dk.reshape(B_, S_, DIFF_HEADS, 2, HEAD_DIM),
            dv.reshape(B_, S_, DIFF_HEADS, DIFF_V),
            lambda_q1[l], lambda_k1[l], lambda_q2[l], lambda_k2[l], subln_g[l],
            diff_slopes, lambda_init)
        swa_out = swa_attention(
            sq.reshape(B_, S_, SWA_KV_HEADS, SWA_GROUP, HEAD_DIM),
            sk.reshape(B_, S_, SWA_KV_HEADS, HEAD_DIM),
            sv.reshape(B_, S_, SWA_KV_HEADS, HEAD_DIM),
            sinks[l], swa_slopes)
        mixed = jnp.concatenate([diff_out, swa_out], axis=-1) @ w_out[l]
        x = layer_norm(DEEPNORM_ALPHA * x + g1 * mixed, ln1_g[l], ln1_b[l])
        h = x * (1.0 + sc2) + sh2
        ffn = peer(h, w_pq[l], sub_keys[l], u_tab[l], v_tab[l])
        x = layer_norm(DEEPNORM_ALPHA * x + g2 * ffn, ln2_g[l], ln2_b[l])
    return x
```

```python
import functools
import math

import jax
import jax.numpy as jnp
from jax import lax
from jax.experimental import pallas as pl
from jax.experimental.pallas import tpu as pltpu
from jax.experimental.pallas import tpu_sc as plsc

F32 = jnp.float32
BF16 = jnp.bfloat16
I32 = jnp.int32

HEAD_DIM = 64
DIFF_HEADS = 4
DIFF_V = 2 * HEAD_DIM
SWA_Q_HEADS = 8
SWA_KV_HEADS = 2
SWA_GROUP = SWA_Q_HEADS // SWA_KV_HEADS
WINDOW = 128
N_ATT_HEADS = SWA_Q_HEADS + DIFF_HEADS
PEER_HEADS = 8
N_KEYS = 128
PEER_HALF = 128
PEER_TOPK = 16
PEER_SLOTS = PEER_HEADS * PEER_TOPK
LN_EPS = 1e-5
NEG_INF = -1e30

LANES = 128
SUBLANES = 8
VMEM_LIMIT = 48 * 1024 * 1024

SC_CORES = 2
SC_SUBCORES = 16
SC_WORKERS = SC_CORES * SC_SUBCORES
GATHER_ROWS = 64


def _nt_dot(a, b):
    return lax.dot_general(a, b, (((1,), (1,)), ((), ())), preferred_element_type=F32)


def _mod_kernel(c_ref, w_ref, b_ref, o_ref):
    c = c_ref[...]
    s = c * (1.0 / (1.0 + jnp.exp(-c)))
    o_ref[...] = jnp.dot(s.astype(BF16), w_ref[...].astype(BF16), preferred_element_type=F32) + b_ref[...]


def _mod(c, w, b):
    bsz, d = c.shape
    n = w.shape[1]
    tn = 768
    return pl.pallas_call(
        _mod_kernel,
        grid=(n // tn,),
        in_specs=[
            pl.BlockSpec((bsz, d), lambda j: (0, 0)),
            pl.BlockSpec((d, tn), lambda j: (0, j)),
            pl.BlockSpec((1, tn), lambda j: (0, j)),
        ],
        out_specs=pl.BlockSpec((bsz, tn), lambda j: (0, j)),
        out_shape=jax.ShapeDtypeStruct((bsz, n), F32),
        compiler_params=pltpu.CompilerParams(dimension_semantics=("arbitrary",), vmem_limit_bytes=VMEM_LIMIT),
        name="adaln_mod",
    )(c, w, b.reshape(1, n))


def _inproj_kernel(x_ref, mod_ref, w_ref, o_ref):
    h = x_ref[...] * (1.0 + mod_ref[1:2, :]) + mod_ref[0:1, :]
    o_ref[...] = jnp.dot(h.astype(BF16), w_ref[...], preferred_element_type=F32).astype(BF16)


def _inproj(x, mod3, w_bf16):
    bsz, s, d = x.shape
    n = w_bf16.shape[1]
    tm = 512
    return pl.pallas_call(
        _inproj_kernel,
        grid=(bsz, s // tm),
        in_specs=[
            pl.BlockSpec((None, tm, d), lambda b, i: (b, i, 0)),
            pl.BlockSpec((None, 6, d), lambda b, i: (b, 0, 0)),
            pl.BlockSpec((d, n), lambda b, i: (0, 0)),
        ],
        out_specs=pl.BlockSpec((None, tm, n), lambda b, i: (b, i, 0)),
        out_shape=jax.ShapeDtypeStruct((bsz, s, n), BF16),
        compiler_params=pltpu.CompilerParams(
            dimension_semantics=("arbitrary", "arbitrary"), vmem_limit_bytes=VMEM_LIMIT),
        name="in_proj",
    )(x, mod3, w_bf16)


def _diff_kernel(slopes_ref, q_ref, k_ref, v_ref, lam_ref, g_ref, o_ref, *, tq, lambda_init):
    h = pl.program_id(1)
    i = pl.program_id(2)
    slope = slopes_ref[SWA_Q_HEADS + h]
    scale = HEAD_DIM ** -0.5
    q = q_ref[...]
    qs = (q[:, :HEAD_DIM], q[:, HEAD_DIM:])
    row = (i * tq + lax.broadcasted_iota(I32, (tq, 1), 0)).astype(F32)

    def body(j, carry):
        ks = k_ref[pl.ds(pl.multiple_of(j * tq, tq), tq), :]
        vs = v_ref[pl.ds(pl.multiple_of(j * tq, tq), tq), :]
        col = (j * tq + lax.broadcasted_iota(I32, (1, tq), 1)).astype(F32)
        dist = row - col
        valid = dist >= 0.0
        bias = -slope * dist
        new = []
        for m in range(2):
            mx, l, acc = carry[3 * m: 3 * m + 3]
            s = _nt_dot(qs[m], ks[:, m * HEAD_DIM:(m + 1) * HEAD_DIM]) * scale
            s = jnp.where(valid, s + bias, NEG_INF)
            mx_new = jnp.maximum(mx, jnp.max(s, axis=-1, keepdims=True))
            p = jnp.exp(s - mx_new)
            corr = jnp.exp(mx - mx_new)
            l = l * corr + jnp.sum(p, axis=-1, keepdims=True)
            acc = acc * corr + jnp.dot(p.astype(BF16), vs, preferred_element_type=F32)
            new += [mx_new, l, acc]
        return tuple(new)

    init = []
    for _ in range(2):
        init += [jnp.full((tq, 1), NEG_INF, F32), jnp.zeros((tq, 1), F32), jnp.zeros((tq, DIFF_V), F32)]
    m0, l0, a0, m1, l1, a1 = lax.fori_loop(0, i + 1, body, tuple(init))

    lam_v = lam_ref[...]
    lam = (jnp.exp(jnp.sum(lam_v[0:1, :] * lam_v[1:2, :], axis=-1, keepdims=True))
           - jnp.exp(jnp.sum(lam_v[2:3, :] * lam_v[3:4, :], axis=-1, keepdims=True)) + lambda_init)
    o = a0 / l0 - lam * (a1 / l1)
    o = o * lax.rsqrt(jnp.mean(o * o, axis=-1, keepdims=True) + LN_EPS)
    o_ref[...] = (o * g_ref[...] * (1.0 - lambda_init)).astype(BF16)


def _diff_attention(proj, slopes, lam_vecs, subln_g, lambda_init):
    bsz, s, _ = proj.shape
    tq = 256
    kcol = DIFF_HEADS
    vcol = 2 * DIFF_HEADS
    return pl.pallas_call(
        functools.partial(_diff_kernel, tq=tq, lambda_init=lambda_init),
        grid=(bsz, DIFF_HEADS, s // tq),
        in_specs=[
            pl.BlockSpec(memory_space=pltpu.SMEM),
            pl.BlockSpec((None, tq, DIFF_V), lambda b, h, i: (b, i, h)),
            pl.BlockSpec((None, s, DIFF_V), lambda b, h, i: (b, 0, kcol + h)),
            pl.BlockSpec((None, s, DIFF_V), lambda b, h, i: (b, 0, vcol + h)),
            pl.BlockSpec((4, HEAD_DIM), lambda b, h, i: (0, 0)),
            pl.BlockSpec((1, DIFF_V), lambda b, h, i: (0, 0)),
        ],
        out_specs=pl.BlockSpec((None, tq, DIFF_V), lambda b, h, i: (b, i, h)),
        out_shape=jax.ShapeDtypeStruct((bsz, s, DIFF_HEADS * DIFF_V), BF16),
        compiler_params=pltpu.CompilerParams(
            dimension_semantics=("arbitrary", "arbitrary", "arbitrary"), vmem_limit_bytes=VMEM_LIMIT),
        name="diff_attention",
    )(slopes, proj, proj, proj, lam_vecs, subln_g.reshape(1, DIFF_V))


def _swa_kernel(slopes_ref, sinks_ref, q_ref, k_ref, v_ref, o_ref, *, tq):
    i = pl.program_id(1)
    scale = HEAD_DIM ** -0.5
    blk = WINDOW
    ii = lax.broadcasted_iota(I32, (blk, 2 * blk), 0)
    jj = lax.broadcasted_iota(I32, (blk, 2 * blk), 1)
    for r in range(tq // blk):
        start = i * tq + r * blk
        kstart = jnp.maximum(start - blk, 0)
        kb = k_ref[pl.ds(pl.multiple_of(kstart, blk), 2 * blk), :]
        vb = v_ref[pl.ds(pl.multiple_of(kstart, blk), 2 * blk), :]
        dist = (start + ii) - (kstart + jj)
        valid = (dist >= 0) & (dist < WINDOW)
        distf = dist.astype(F32)
        outs = []
        for kvh in range(SWA_KV_HEADS):
            k = kb[:, kvh * HEAD_DIM:(kvh + 1) * HEAD_DIM]
            v = vb[:, kvh * HEAD_DIM:(kvh + 1) * HEAD_DIM]
            for g in range(SWA_GROUP):
                hq = kvh * SWA_GROUP + g
                qh = q_ref[r * blk:(r + 1) * blk, hq * HEAD_DIM:(hq + 1) * HEAD_DIM]
                s = _nt_dot(qh, k) * scale - slopes_ref[hq] * distf
                s = jnp.where(valid, s, NEG_INF)
                sink = sinks_ref[hq]
                m = jnp.maximum(jnp.max(s, axis=-1, keepdims=True), sink)
                p = jnp.exp(s - m)
                denom = jnp.sum(p, axis=-1, keepdims=True) + jnp.exp(sink - m)
                outs.append(jnp.dot(p.astype(BF16), v, preferred_element_type=F32) / denom)
        o_ref[r * blk:(r + 1) * blk, :] = jnp.concatenate(outs, axis=-1).astype(BF16)


def _swa_attention(proj, slopes, sinks):
    bsz, s, _ = proj.shape
    tq = 256
    width = SWA_Q_HEADS * HEAD_DIM
    qcol = (3 * DIFF_HEADS * DIFF_V) // width
    kcol = (3 * DIFF_HEADS * DIFF_V + width) // LANES
    return pl.pallas_call(
        functools.partial(_swa_kernel, tq=tq),
        grid=(bsz, s // tq),
        in_specs=[
            pl.BlockSpec(memory_space=pltpu.SMEM),
            pl.BlockSpec(memory_space=pltpu.SMEM),
            pl.BlockSpec((None, tq, width), lambda b, i: (b, i, qcol)),
            pl.BlockSpec((None, s, LANES), lambda b, i: (b, 0, kcol)),
            pl.BlockSpec((None, s, LANES), lambda b, i: (b, 0, kcol + 1)),
        ],
        out_specs=pl.BlockSpec((None, tq, width), lambda b, i: (b, i, 0)),
        out_shape=jax.ShapeDtypeStruct((bsz, s, width), BF16),
        compiler_params=pltpu.CompilerParams(
            dimension_semantics=("arbitrary", "arbitrary"), vmem_limit_bytes=VMEM_LIMIT),
        name="swa_attention",
    )(slopes, sinks, proj, proj, proj)


def _layer_norm(y, g, b):
    mu = jnp.mean(y, axis=-1, keepdims=True)
    yc = y - mu
    var = jnp.mean(yc * yc, axis=-1, keepdims=True)
    return yc * lax.rsqrt(var + LN_EPS) * g + b


def _mid_kernel(do_ref, so_ref, x_ref, mod_ref, wo_ref, ln_ref, wpq_ref, x1_ref, h2_ref, q_ref, *, alpha):
    nd = do_ref.shape[-1]
    mixed = (jnp.dot(do_ref[...], wo_ref[:nd, :], preferred_element_type=F32)
             + jnp.dot(so_ref[...], wo_ref[nd:, :], preferred_element_type=F32))
    y = alpha * x_ref[...] + mod_ref[2:3, :] * mixed
    x1 = _layer_norm(y, ln_ref[0:1, :], ln_ref[1:2, :])
    x1_ref[...] = x1
    h2 = (x1 * (1.0 + mod_ref[4:5, :]) + mod_ref[3:4, :]).astype(BF16)
    h2_ref[...] = h2.astype(F32)
    q_ref[...] = jnp.dot(h2, wpq_ref[...], preferred_element_type=F32).astype(BF16)


def _mid(diff_out, swa_out, x, mod3, wo_bf16, ln1, wpq_bf16, alpha):
    bsz, s, d = x.shape
    nq = wpq_bf16.shape[1]
    tm = 512
    row = lambda b, i: (b, i, 0)
    const = lambda b, i: (0, 0)
    return pl.pallas_call(
        functools.partial(_mid_kernel, alpha=alpha),
        grid=(bsz, s // tm),
        in_specs=[
            pl.BlockSpec((None, tm, diff_out.shape[-1]), row),
            pl.BlockSpec((None, tm, swa_out.shape[-1]), row),
            pl.BlockSpec((None, tm, d), row),
            pl.BlockSpec((None, 6, d), lambda b, i: (b, 0, 0)),
            pl.BlockSpec(wo_bf16.shape, const),
            pl.BlockSpec((2, d), const),
            pl.BlockSpec(wpq_bf16.shape, const),
        ],
        out_specs=[
            pl.BlockSpec((None, tm, d), row),
            pl.BlockSpec((None, tm, d), row),
            pl.BlockSpec((None, tm, nq), row),
        ],
        out_shape=[
            jax.ShapeDtypeStruct((bsz, s, d), F32),
            jax.ShapeDtypeStruct((bsz, s, d), F32),
            jax.ShapeDtypeStruct((bsz, s, nq), BF16),
        ],
        compiler_params=pltpu.CompilerParams(
            dimension_semantics=("arbitrary", "arbitrary"), vmem_limit_bytes=VMEM_LIMIT),
        name="outproj_ln1_peerq",
    )(diff_out, swa_out, x, mod3, wo_bf16, ln1, wpq_bf16)


def _topk_rows(vals, pos, payload, k):
    out_v, out_p = [], []
    for _ in range(k):
        m = jnp.max(vals, axis=0, keepdims=True)
        first = jnp.min(jnp.where(vals == m, pos, 1e9), axis=0, keepdims=True)
        sel = pos == first
        if payload is None:
            out_p.append(first)
        else:
            out_p.append(jnp.max(jnp.where(sel, payload, -1.0), axis=0, keepdims=True))
        out_v.append(m)
        vals = jnp.where(sel, -jnp.inf, vals)
    return jnp.concatenate(out_v, axis=0), jnp.concatenate(out_p, axis=0)


def _candidates(v0, i0, v1, i1):
    k, lanes = v0.shape
    vals, poss, eids = [], [], []
    for a in range(4):
        nb = k if a == 0 else k // 2
        b_iota = lax.broadcasted_iota(I32, (nb, lanes), 0).astype(F32)
        vals.append(v0[a:a + 1, :] + v1[:nb, :])
        poss.append(a * k + b_iota)
        eids.append(i0[a:a + 1, :] * N_KEYS + i1[:nb, :])
    for b in range(3):
        na = k if b == 0 else k // 2
        a_iota = lax.broadcasted_iota(I32, (na, lanes), 0).astype(F32)
        vals.append(jnp.where(a_iota >= 4.0, v0[:na, :] + v1[b:b + 1, :], -jnp.inf))
        poss.append(a_iota * k + b)
        eids.append(i0[:na, :] * N_KEYS + i1[b:b + 1, :])
    return jnp.concatenate(vals, axis=0), jnp.concatenate(poss, axis=0), jnp.concatenate(eids, axis=0)


def _route_kernel(q_ref, keys_ref, idx_ref, gate_ref, idx_t, gate_t):
    tt = q_ref.shape[0]
    key_pos = lax.broadcasted_iota(I32, (N_KEYS, tt), 0).astype(F32)

    def head(h, carry):
        halves = []
        for p in range(2):
            qh = q_ref[:, pl.ds(pl.multiple_of((2 * h + p) * PEER_HALF, PEER_HALF), PEER_HALF)]
            sc = _nt_dot(keys_ref[h, p], qh)
            halves.append(_topk_rows(sc, key_pos, None, PEER_TOPK))
        (v0, i0), (v1, i1) = halves
        cv, cp, ce = _candidates(v0, i0, v1, i1)
        top_s, top_e = _topk_rows(cv, cp, ce, PEER_TOPK)
        e = jnp.exp(top_s - top_s[0:1, :])
        gate = e / jnp.sum(e, axis=0, keepdims=True)
        rows = pl.ds(pl.multiple_of(h * PEER_TOPK, PEER_TOPK), PEER_TOPK)
        idx_t[rows, :] = top_e
        gate_t[rows, :] = gate
        return carry

    lax.fori_loop(0, PEER_HEADS, head, 0)
    idx_ref[...] = idx_t[...].T.astype(I32)
    gate_ref[...] = gate_t[...]


def _route(q2d, keys_bf16):
    t, nq = q2d.shape
    tt = LANES
    return pl.pallas_call(
        _route_kernel,
        grid=(t // tt,),
        in_specs=[
            pl.BlockSpec((tt, nq), lambda i: (i, 0)),
            pl.BlockSpec(keys_bf16.shape, lambda i: (0, 0, 0, 0)),
        ],
        out_specs=[
            pl.BlockSpec((tt, PEER_SLOTS), lambda i: (i, 0)),
            pl.BlockSpec((None, PEER_SLOTS, tt), lambda i: (i, 0, 0)),
        ],
        out_shape=[
            jax.ShapeDtypeStruct((t, PEER_SLOTS), I32),
            jax.ShapeDtypeStruct((t // tt, PEER_SLOTS, tt), F32),
        ],
        scratch_shapes=[pltpu.VMEM((PEER_SLOTS, tt), F32), pltpu.VMEM((PEER_SLOTS, tt), F32)],
        compiler_params=pltpu.CompilerParams(dimension_semantics=("arbitrary",), vmem_limit_bytes=VMEM_LIMIT),
        name="peer_route",
    )(q2d, keys_bf16)


def _gather_rows(table, idx):
    n = idx.shape[0]
    width = table.shape[1]
    per_worker = n // SC_WORKERS
    steps = per_worker // GATHER_ROWS
    assert steps * GATHER_ROWS * SC_WORKERS == n and steps % 2 == 0
    mesh = plsc.VectorSubcoreMesh(core_axis_name="c", subcore_axis_name="s")

    def body(table_hbm, idx_hbm, out_hbm, idx_v, rows_v, gsem, osem):
        wid = lax.axis_index("s") * SC_CORES + lax.axis_index("c")
        pltpu.sync_copy(idx_hbm.at[wid], idx_v)

        def gather(j, slot):
            return pltpu.make_async_copy(table_hbm.at[idx_v.at[j]], rows_v.at[slot], gsem.at[slot])

        def put(j, slot):
            dst = out_hbm.at[pl.ds(wid * per_worker + j * GATHER_ROWS, GATHER_ROWS)]
            return pltpu.make_async_copy(rows_v.at[slot], dst, osem.at[slot])

        gather(0, 0).start()

        @pl.loop(0, steps, step=2)
        def _(j0):
            for slot in range(2):
                j = j0 + slot
                other = 1 - slot
                gather(j, slot).wait()
                put(j, slot).start()

                @pl.when(j + 1 < steps)
                def _():
                    @pl.when(j >= 1)
                    def _():
                        put(j - 1, other).wait()

                    gather(j + 1, other).start()

        put(steps - 2, 0).wait()
        put(steps - 1, 1).wait()

    return pl.kernel(
        body,
        out_type=jax.ShapeDtypeStruct((n, width), table.dtype),
        mesh=mesh,
        scratch_types=[
            pltpu.VMEM((steps, GATHER_ROWS), I32),
            pltpu.VMEM((2, GATHER_ROWS, width), table.dtype),
            pltpu.SemaphoreType.DMA((2,)),
            pltpu.SemaphoreType.DMA((2,)),
        ],
        name="peer_gather",
    )(table, idx.reshape(SC_WORKERS, steps, GATHER_ROWS))


def _pack_table(tab):
    half = tab.shape[1] // 2
    bits = lax.bitcast_convert_type(tab.astype(BF16), jnp.uint16).astype(jnp.uint32)
    return lax.bitcast_convert_type(bits[:, :half] | (bits[:, half:] << 16), I32)


def _unpack(w):
    lo = lax.bitcast_convert_type(w << 16, F32)
    hi = lax.bitcast_convert_type(w & jnp.int32(-65536), F32)
    return lo, hi


def _gelu(a):
    return 0.5 * a * (1.0 + lax.erf(a * (2.0 ** -0.5)))


def _expert_kernel(ug_ref, vg_ref, h_ref, gate_ref, x1_ref, mod_ref, ln_ref, o_ref, ffn_ref, *, alpha, tt):
    i = pl.program_id(0)
    groups = LANES // tt
    base = (i % groups) * tt
    nchunk = ug_ref.shape[-1] // LANES
    ntile = PEER_SLOTS // SUBLANES
    lane = lax.broadcasted_iota(I32, (PEER_SLOTS, LANES), 1)

    def u_body(t, a_all):
        h_t = h_ref[t]
        h_lo = [jnp.broadcast_to(h_t[c:c + 1, :], (SUBLANES, LANES)) for c in range(nchunk)]
        h_hi = [jnp.broadcast_to(h_t[nchunk + c:nchunk + c + 1, :], (SUBLANES, LANES)) for c in range(nchunk)]
        cols = []
        for e in range(ntile):
            acc = jnp.zeros((SUBLANES, LANES), F32)
            for c in range(nchunk):
                lo, hi = _unpack(ug_ref[t, pl.ds(e * SUBLANES, SUBLANES), pl.ds(c * LANES, LANES)])
                acc = acc + lo * h_lo[c] + hi * h_hi[c]
            cols.append(jnp.sum(acc, axis=-1, keepdims=True))
        a_col = jnp.concatenate(cols, axis=0)
        return jnp.where(lane == base + t, a_col, a_all)

    a_all = lax.fori_loop(0, tt, u_body, jnp.zeros((PEER_SLOTS, LANES), F32))
    w_all = _gelu(a_all) * gate_ref[...]

    def v_body(t, carry):
        w_col = jnp.sum(jnp.where(lane == base + t, w_all, 0.0), axis=-1, keepdims=True)
        acc_lo = [jnp.zeros((SUBLANES, LANES), F32) for _ in range(nchunk)]
        acc_hi = [jnp.zeros((SUBLANES, LANES), F32) for _ in range(nchunk)]
        for e in range(ntile):
            wc = jnp.broadcast_to(w_col[e * SUBLANES:(e + 1) * SUBLANES, :], (SUBLANES, LANES))
            for c in range(nchunk):
                lo, hi = _unpack(vg_ref[t, pl.ds(e * SUBLANES, SUBLANES), pl.ds(c * LANES, LANES)])
                acc_lo[c] = acc_lo[c] + wc * lo
                acc_hi[c] = acc_hi[c] + wc * hi
        ffn_ref[t] = jnp.concatenate([jnp.sum(a, axis=0, keepdims=True) for a in acc_lo + acc_hi], axis=0)
        return carry

    lax.fori_loop(0, tt, v_body, 0)
    y = alpha * x1_ref[...] + mod_ref[5] * ffn_ref[...]
    mu = jnp.mean(y, axis=(1, 2), keepdims=True)
    yc = y - mu
    var = jnp.mean(yc * yc, axis=(1, 2), keepdims=True)
    o_ref[...] = yc * lax.rsqrt(var + LN_EPS) * ln_ref[0] + ln_ref[1]


def _experts(ug, vg, h2, gate_t, x1, mod3, ln2, alpha, tok0, seq):
    tc = ug.shape[0]
    ft = h2.shape[1:]
    tt = 32
    blk0 = tok0 // tt
    groups = LANES // tt
    tok = lambda i: (blk0 + i, 0, 0)
    return pl.pallas_call(
        functools.partial(_expert_kernel, alpha=alpha, tt=tt),
        grid=(tc // tt,),
        in_specs=[
            pl.BlockSpec((tt,) + ug.shape[1:], lambda i: (i, 0, 0)),
            pl.BlockSpec((tt,) + vg.shape[1:], lambda i: (i, 0, 0)),
            pl.BlockSpec((tt,) + ft, tok),
            pl.BlockSpec((None, PEER_SLOTS, LANES), lambda i: ((blk0 + i) // groups, 0, 0)),
            pl.BlockSpec((tt,) + ft, tok),
            pl.BlockSpec((None, 6) + ft, lambda i: (((blk0 + i) * tt) // seq, 0, 0, 0)),
            pl.BlockSpec((2,) + ft, lambda i: (0, 0, 0)),
        ],
        out_specs=pl.BlockSpec((tt,) + ft, lambda i: (i, 0, 0)),
        out_shape=jax.ShapeDtypeStruct((tc,) + ft, F32),
        scratch_shapes=[pltpu.VMEM((tt,) + ft, F32)],
        compiler_params=pltpu.CompilerParams(dimension_semantics=("arbitrary",), vmem_limit_bytes=VMEM_LIMIT),
        name="peer_experts_ln2",
    )(ug, vg, h2, gate_t, x1, mod3, ln2)


PEER_TOKEN_CHUNK = 2048


def _peer_and_norm(h2, x1, idx, gate_t, u_pack, v_pack, mod3, ln2, alpha, seq):
    t = h2.shape[0]
    chunk = min(PEER_TOKEN_CHUNK, t)
    outs = []
    for c0 in range(0, t, chunk):
        flat = idx[c0:c0 + chunk].reshape(-1)
        ug = _gather_rows(u_pack, flat).reshape(chunk, PEER_SLOTS, -1)
        vg = _gather_rows(v_pack, flat).reshape(chunk, PEER_SLOTS, -1)
        outs.append(_experts(ug, vg, h2, gate_t, x1, mod3, ln2, alpha, c0, seq))
    return jnp.concatenate(outs, axis=0)


def kernel(x, c, w_ada, b_ada, w_in, lambda_q1, lambda_k1, lambda_q2, lambda_k2, subln_g, sinks, w_out, ln1_g, ln1_b, w_pq, sub_keys, u_tab, v_tab, ln2_g, ln2_b):
    bsz, seq, d = x.shape
    depth = w_ada.shape[0]
    alpha = (2 * depth) ** 0.25
    slopes = jnp.exp2(-8.0 * jnp.arange(1, N_ATT_HEADS + 1, dtype=F32) / N_ATT_HEADS)
    for l in range(depth):
        lambda_init = 0.8 - 0.6 * math.exp(-0.3 * l)
        mod3 = _mod(c, w_ada[l], b_ada[l]).reshape(bsz, 6, d)
        proj = _inproj(x, mod3, w_in[l].astype(BF16))
        lam_vecs = jnp.stack([lambda_q1[l], lambda_k1[l], lambda_q2[l], lambda_k2[l]])
        diff_out = _diff_attention(proj, slopes, lam_vecs, subln_g[l], lambda_init)
        swa_out = _swa_attention(proj, slopes, sinks[l])
        x1, h2, q = _mid(diff_out, swa_out, x, mod3, w_out[l].astype(BF16),
                         jnp.stack([ln1_g[l], ln1_b[l]]), w_pq[l].astype(BF16), alpha)
        idx, gate_t = _route(q.reshape(bsz * seq, -1), sub_keys[l].astype(BF16))
        ft = (d // LANES, LANES)
        out = _peer_and_norm(h2.reshape((bsz * seq,) + ft), x1.reshape((bsz * seq,) + ft), idx, gate_t,
                             _pack_table(u_tab[l]), _pack_table(v_tab[l]), mod3.reshape((bsz, 6) + ft),
                             jnp.stack([ln2_g[l], ln2_b[l]]).reshape((2,) + ft), alpha, seq)
        x = out.reshape(bsz, seq, d)
    return x
```

```python
import functools
import math

import jax
import jax.numpy as jnp
from jax import lax
from jax.experimental import pallas as pl
from jax.experimental.pallas import tpu as pltpu
from jax.experimental.pallas import tpu_sc as plsc

F32 = jnp.float32
BF16 = jnp.bfloat16
I32 = jnp.int32

HEAD_DIM = 64
DIFF_HEADS = 4
DIFF_V = 2 * HEAD_DIM
SWA_Q_HEADS = 8
SWA_KV_HEADS = 2
SWA_GROUP = SWA_Q_HEADS // SWA_KV_HEADS
WINDOW = 128
N_ATT_HEADS = SWA_Q_HEADS + DIFF_HEADS
PEER_HEADS = 8
N_KEYS = 128
PEER_HALF = 128
PEER_TOPK = 16
PEER_SLOTS = PEER_HEADS * PEER_TOPK
LN_EPS = 1e-5
NEG_INF = -1e30

LANES = 128
SUBLANES = 8
VMEM_LIMIT = 48 * 1024 * 1024

SC_CORES = 2
SC_SUBCORES = 16
SC_WORKERS = SC_CORES * SC_SUBCORES
GATHER_ROWS = 64


def _nt_dot(a, b):
    return lax.dot_general(a, b, (((1,), (1,)), ((), ())), preferred_element_type=F32)


def _mod_kernel(c_ref, w_ref, b_ref, o_ref):
    c = c_ref[...]
    s = c * (1.0 / (1.0 + jnp.exp(-c)))
    o_ref[...] = jnp.dot(s.astype(BF16), w_ref[...].astype(BF16), preferred_element_type=F32) + b_ref[...]


def _mod(c, w, b):
    bsz, d = c.shape
    n = w.shape[1]
    tn = 768
    return pl.pallas_call(
        _mod_kernel,
        grid=(n // tn,),
        in_specs=[
            pl.BlockSpec((bsz, d), lambda j: (0, 0)),
            pl.BlockSpec((d, tn), lambda j: (0, j)),
            pl.BlockSpec((1, tn), lambda j: (0, j)),
        ],
        out_specs=pl.BlockSpec((bsz, tn), lambda j: (0, j)),
        out_shape=jax.ShapeDtypeStruct((bsz, n), F32),
        compiler_params=pltpu.CompilerParams(dimension_semantics=("arbitrary",), vmem_limit_bytes=VMEM_LIMIT),
        name="adaln_mod",
    )(c, w, b.reshape(1, n))


def _inproj_kernel(x_ref, mod_ref, w_ref, o_ref):
    h = x_ref[...] * (1.0 + mod_ref[1:2, :]) + mod_ref[0:1, :]
    o_ref[...] = jnp.dot(h.astype(BF16), w_ref[...], preferred_element_type=F32).astype(BF16)


def _inproj(x, mod3, w_bf16, b0, bsz):
    _, s, d = x.shape
    n = w_bf16.shape[1]
    tm = 512
    return pl.pallas_call(
        _inproj_kernel,
        grid=(bsz, s // tm),
        in_specs=[
            pl.BlockSpec((None, tm, d), lambda b, i: (b0 + b, i, 0)),
            pl.BlockSpec((None, 6, d), lambda b, i: (b0 + b, 0, 0)),
            pl.BlockSpec((d, n), lambda b, i: (0, 0)),
        ],
        out_specs=pl.BlockSpec((None, tm, n), lambda b, i: (b, i, 0)),
        out_shape=jax.ShapeDtypeStruct((bsz, s, n), BF16),
        compiler_params=pltpu.CompilerParams(
            dimension_semantics=("arbitrary", "arbitrary"), vmem_limit_bytes=VMEM_LIMIT),
        name="in_proj",
    )(x, mod3, w_bf16)


def _diff_kernel(slopes_ref, q_ref, k_ref, v_ref, lam_ref, g_ref, o_ref, *, tq, lambda_init):
    h = pl.program_id(1)
    i = pl.program_id(2)
    slope = slopes_ref[SWA_Q_HEADS + h]
    scale = HEAD_DIM ** -0.5
    q = q_ref[...]
    qs = (q[:, :HEAD_DIM], q[:, HEAD_DIM:])
    row = (i * tq + lax.broadcasted_iota(I32, (tq, 1), 0)).astype(F32)

    def body(j, carry):
        ks = k_ref[pl.ds(pl.multiple_of(j * tq, tq), tq), :]
        vs = v_ref[pl.ds(pl.multiple_of(j * tq, tq), tq), :]
        col = (j * tq + lax.broadcasted_iota(I32, (1, tq), 1)).astype(F32)
        dist = row - col
        valid = dist >= 0.0
        bias = -slope * dist
        new = []
        for m in range(2):
            mx, l, acc = carry[3 * m: 3 * m + 3]
            s = _nt_dot(qs[m], ks[:, m * HEAD_DIM:(m + 1) * HEAD_DIM]) * scale
            s = jnp.where(valid, s + bias, NEG_INF)
            mx_new = jnp.maximum(mx, jnp.max(s, axis=-1, keepdims=True))
            p = jnp.exp(s - mx_new)
            corr = jnp.exp(mx - mx_new)
            l = l * corr + jnp.sum(p, axis=-1, keepdims=True)
            acc = acc * corr + jnp.dot(p.astype(BF16), vs, preferred_element_type=F32)
            new += [mx_new, l, acc]
        return tuple(new)

    init = []
    for _ in range(2):
        init += [jnp.full((tq, 1), NEG_INF, F32), jnp.zeros((tq, 1), F32), jnp.zeros((tq, DIFF_V), F32)]
    m0, l0, a0, m1, l1, a1 = lax.fori_loop(0, i + 1, body, tuple(init))

    lam_v = lam_ref[...]
    lam = (jnp.exp(jnp.sum(lam_v[0:1, :] * lam_v[1:2, :], axis=-1, keepdims=True))
           - jnp.exp(jnp.sum(lam_v[2:3, :] * lam_v[3:4, :], axis=-1, keepdims=True)) + lambda_init)
    o = a0 / l0 - lam * (a1 / l1)
    o = o * lax.rsqrt(jnp.mean(o * o, axis=-1, keepdims=True) + LN_EPS)
    o_ref[...] = (o * g_ref[...] * (1.0 - lambda_init)).astype(BF16)


def _diff_attention(proj, slopes, lam_vecs, subln_g, lambda_init):
    bsz, s, _ = proj.shape
    tq = 256
    kcol = DIFF_HEADS
    vcol = 2 * DIFF_HEADS
    return pl.pallas_call(
        functools.partial(_diff_kernel, tq=tq, lambda_init=lambda_init),
        grid=(bsz, DIFF_HEADS, s // tq),
        in_specs=[
            pl.BlockSpec(memory_space=pltpu.SMEM),
            pl.BlockSpec((None, tq, DIFF_V), lambda b, h, i: (b, i, h)),
            pl.BlockSpec((None, s, DIFF_V), lambda b, h, i: (b, 0, kcol + h)),
            pl.BlockSpec((None, s, DIFF_V), lambda b, h, i: (b, 0, vcol + h)),
            pl.BlockSpec((4, HEAD_DIM), lambda b, h, i: (0, 0)),
            pl.BlockSpec((1, DIFF_V), lambda b, h, i: (0, 0)),
        ],
        out_specs=pl.BlockSpec((None, tq, DIFF_V), lambda b, h, i: (b, i, h)),
        out_shape=jax.ShapeDtypeStruct((bsz, s, DIFF_HEADS * DIFF_V), BF16),
        compiler_params=pltpu.CompilerParams(
            dimension_semantics=("arbitrary", "arbitrary", "arbitrary"), vmem_limit_bytes=VMEM_LIMIT),
        name="diff_attention",
    )(slopes, proj, proj, proj, lam_vecs, subln_g.reshape(1, DIFF_V))


def _swa_kernel(slopes_ref, sinks_ref, q_ref, k_ref, v_ref, o_ref, *, tq):
    i = pl.program_id(1)
    scale = HEAD_DIM ** -0.5
    blk = WINDOW
    ii = lax.broadcasted_iota(I32, (blk, 2 * blk), 0)
    jj = lax.broadcasted_iota(I32, (blk, 2 * blk), 1)
    for r in range(tq // blk):
        start = i * tq + r * blk
        kstart = jnp.maximum(start - blk, 0)
        kb = k_ref[pl.ds(pl.multiple_of(kstart, blk), 2 * blk), :]
        vb = v_ref[pl.ds(pl.multiple_of(kstart, blk), 2 * blk), :]
        dist = (start + ii) - (kstart + jj)
        valid = (dist >= 0) & (dist < WINDOW)
        distf = dist.astype(F32)
        outs = []
        for kvh in range(SWA_KV_HEADS):
            k = kb[:, kvh * HEAD_DIM:(kvh + 1) * HEAD_DIM]
            v = vb[:, kvh * HEAD_DIM:(kvh + 1) * HEAD_DIM]
            for g in range(SWA_GROUP):
                hq = kvh * SWA_GROUP + g
                qh = q_ref[r * blk:(r + 1) * blk, hq * HEAD_DIM:(hq + 1) * HEAD_DIM]
                s = _nt_dot(qh, k) * scale - slopes_ref[hq] * distf
                s = jnp.where(valid, s, NEG_INF)
                sink = sinks_ref[hq]
                m = jnp.maximum(jnp.max(s, axis=-1, keepdims=True), sink)
                p = jnp.exp(s - m)
                denom = jnp.sum(p, axis=-1, keepdims=True) + jnp.exp(sink - m)
                outs.append(jnp.dot(p.astype(BF16), v, preferred_element_type=F32) / denom)
        o_ref[r * blk:(r + 1) * blk, :] = jnp.concatenate(outs, axis=-1).astype(BF16)


def _swa_attention(proj, slopes, sinks):
    bsz, s, _ = proj.shape
    tq = 256
    width = SWA_Q_HEADS * HEAD_DIM
    qcol = (3 * DIFF_HEADS * DIFF_V) // width
    kcol = (3 * DIFF_HEADS * DIFF_V + width) // LANES
    return pl.pallas_call(
        functools.partial(_swa_kernel, tq=tq),
        grid=(bsz, s // tq),
        in_specs=[
            pl.BlockSpec(memory_space=pltpu.SMEM),
            pl.BlockSpec(memory_space=pltpu.SMEM),
            pl.BlockSpec((None, tq, width), lambda b, i: (b, i, qcol)),
            pl.BlockSpec((None, s, LANES), lambda b, i: (b, 0, kcol)),
            pl.BlockSpec((None, s, LANES), lambda b, i: (b, 0, kcol + 1)),
        ],
        out_specs=pl.BlockSpec((None, tq, width), lambda b, i: (b, i, 0)),
        out_shape=jax.ShapeDtypeStruct((bsz, s, width), BF16),
        compiler_params=pltpu.CompilerParams(
            dimension_semantics=("arbitrary", "arbitrary"), vmem_limit_bytes=VMEM_LIMIT),
        name="swa_attention",
    )(slopes, sinks, proj, proj, proj)


def _layer_norm(y, g, b):
    mu = jnp.mean(y, axis=-1, keepdims=True)
    yc = y - mu
    var = jnp.mean(yc * yc, axis=-1, keepdims=True)
    return yc * lax.rsqrt(var + LN_EPS) * g + b


def _mid_kernel(do_ref, so_ref, x_ref, mod_ref, wo_ref, ln_ref, wpq_ref, x1_ref, h2_ref, q_ref, *, alpha):
    nd = do_ref.shape[-1]
    mixed = (jnp.dot(do_ref[...], wo_ref[:nd, :], preferred_element_type=F32)
             + jnp.dot(so_ref[...], wo_ref[nd:, :], preferred_element_type=F32))
    y = alpha * x_ref[...] + mod_ref[2:3, :] * mixed
    x1 = _layer_norm(y, ln_ref[0:1, :], ln_ref[1:2, :])
    x1_ref[...] = x1
    h2 = (x1 * (1.0 + mod_ref[4:5, :]) + mod_ref[3:4, :]).astype(BF16)
    h2_ref[...] = h2.astype(F32)
    q_ref[...] = jnp.dot(h2, wpq_ref[...], preferred_element_type=F32).astype(BF16)


def _mid(diff_out, swa_out, x, mod3, wo_bf16, ln1, wpq_bf16, alpha, b0):
    bsz = diff_out.shape[0]
    _, s, d = x.shape
    nq = wpq_bf16.shape[1]
    tm = 512
    row = lambda b, i: (b, i, 0)
    const = lambda b, i: (0, 0)
    return pl.pallas_call(
        functools.partial(_mid_kernel, alpha=alpha),
        grid=(bsz, s // tm),
        in_specs=[
            pl.BlockSpec((None, tm, diff_out.shape[-1]), row),
            pl.BlockSpec((None, tm, swa_out.shape[-1]), row),
            pl.BlockSpec((None, tm, d), lambda b, i: (b0 + b, i, 0)),
            pl.BlockSpec((None, 6, d), lambda b, i: (b0 + b, 0, 0)),
            pl.BlockSpec(wo_bf16.shape, const),
            pl.BlockSpec((2, d), const),
            pl.BlockSpec(wpq_bf16.shape, const),
        ],
        out_specs=[
            pl.BlockSpec((None, tm, d), row),
            pl.BlockSpec((None, tm, d), row),
            pl.BlockSpec((None, tm, nq), row),
        ],
        out_shape=[
            jax.ShapeDtypeStruct((bsz, s, d), F32),
            jax.ShapeDtypeStruct((bsz, s, d), F32),
            jax.ShapeDtypeStruct((bsz, s, nq), BF16),
        ],
        compiler_params=pltpu.CompilerParams(
            dimension_semantics=("arbitrary", "arbitrary"), vmem_limit_bytes=VMEM_LIMIT),
        name="outproj_ln1_peerq",
    )(diff_out, swa_out, x, mod3, wo_bf16, ln1, wpq_bf16)


def _topk_rows(vals, pos, payload, k):
    out_v, out_p = [], []
    for _ in range(k):
        m = jnp.max(vals, axis=0, keepdims=True)
        first = jnp.min(jnp.where(vals == m, pos, 1e9), axis=0, keepdims=True)
        sel = pos == first
        if payload is None:
            out_p.append(first)
        else:
            out_p.append(jnp.max(jnp.where(sel, payload, -1.0), axis=0, keepdims=True))
        out_v.append(m)
        vals = jnp.where(sel, -jnp.inf, vals)
    return jnp.concatenate(out_v, axis=0), jnp.concatenate(out_p, axis=0)


def _candidates(v0, i0, v1, i1):
    k, lanes = v0.shape
    vals, poss, eids = [], [], []
    for a in range(4):
        nb = k if a == 0 else k // 2
        b_iota = lax.broadcasted_iota(I32, (nb, lanes), 0).astype(F32)
        vals.append(v0[a:a + 1, :] + v1[:nb, :])
        poss.append(a * k + b_iota)
        eids.append(i0[a:a + 1, :] * N_KEYS + i1[:nb, :])
    for b in range(3):
        na = k if b == 0 else k // 2
        a_iota = lax.broadcasted_iota(I32, (na, lanes), 0).astype(F32)
        vals.append(jnp.where(a_iota >= 4.0, v0[:na, :] + v1[b:b + 1, :], -jnp.inf))
        poss.append(a_iota * k + b)
        eids.append(i0[:na, :] * N_KEYS + i1[b:b + 1, :])
    return jnp.concatenate(vals, axis=0), jnp.concatenate(poss, axis=0), jnp.concatenate(eids, axis=0)


def _route_kernel(q_ref, keys_ref, idx_ref, gate_ref, idx_t, gate_t):
    tt = q_ref.shape[0]
    key_pos = lax.broadcasted_iota(I32, (N_KEYS, tt), 0).astype(F32)

    def head(h, carry):
        halves = []
        for p in range(2):
            qh = q_ref[:, pl.ds(pl.multiple_of((2 * h + p) * PEER_HALF, PEER_HALF), PEER_HALF)]
            sc = _nt_dot(keys_ref[h, p], qh)
            halves.append(_topk_rows(sc, key_pos, None, PEER_TOPK))
        (v0, i0), (v1, i1) = halves
        cv, cp, ce = _candidates(v0, i0, v1, i1)
        top_s, top_e = _topk_rows(cv, cp, ce, PEER_TOPK)
        e = jnp.exp(top_s - top_s[0:1, :])
        gate = e / jnp.sum(e, axis=0, keepdims=True)
        rows = pl.ds(pl.multiple_of(h * PEER_TOPK, PEER_TOPK), PEER_TOPK)
        idx_t[rows, :] = top_e
        gate_t[rows, :] = gate
        return carry

    lax.fori_loop(0, PEER_HEADS, head, 0)
    idx_ref[...] = idx_t[...].T.astype(I32)
    gate_ref[...] = gate_t[...]


def _route(q2d, keys_bf16):
    t, nq = q2d.shape
    tt = LANES
    return pl.pallas_call(
        _route_kernel,
        grid=(t // tt,),
        in_specs=[
            pl.BlockSpec((tt, nq), lambda i: (i, 0)),
            pl.BlockSpec(keys_bf16.shape, lambda i: (0, 0, 0, 0)),
        ],
        out_specs=[
            pl.BlockSpec((tt, PEER_SLOTS), lambda i: (i, 0)),
            pl.BlockSpec((None, PEER_SLOTS, tt), lambda i: (i, 0, 0)),
        ],
        out_shape=[
            jax.ShapeDtypeStruct((t, PEER_SLOTS), I32),
            jax.ShapeDtypeStruct((t // tt, PEER_SLOTS, tt), F32),
        ],
        scratch_shapes=[pltpu.VMEM((PEER_SLOTS, tt), F32), pltpu.VMEM((PEER_SLOTS, tt), F32)],
        compiler_params=pltpu.CompilerParams(dimension_semantics=("arbitrary",), vmem_limit_bytes=VMEM_LIMIT),
        name="peer_route",
    )(q2d, keys_bf16)


def _gather_rows(table, idx):
    n = idx.shape[0]
    width = table.shape[1]
    per_worker = n // SC_WORKERS
    steps = per_worker // GATHER_ROWS
    assert steps * GATHER_ROWS * SC_WORKERS == n and steps % 2 == 0
    mesh = plsc.VectorSubcoreMesh(core_axis_name="c", subcore_axis_name="s")

    def body(table_hbm, idx_hbm, out_hbm, idx_v, rows_v, gsem, osem):
        wid = lax.axis_index("s") * SC_CORES + lax.axis_index("c")
        pltpu.sync_copy(idx_hbm.at[wid], idx_v)

        def gather(j, slot):
            return pltpu.make_async_copy(table_hbm.at[idx_v.at[j]], rows_v.at[slot], gsem.at[slot])

        def put(j, slot):
            dst = out_hbm.at[pl.ds(wid * per_worker + j * GATHER_ROWS, GATHER_ROWS)]
            return pltpu.make_async_copy(rows_v.at[slot], dst, osem.at[slot])

        gather(0, 0).start()

        @pl.loop(0, steps, step=2)
        def _(j0):
            for slot in range(2):
                j = j0 + slot
                other = 1 - slot
                gather(j, slot).wait()
                put(j, slot).start()

                @pl.when(j + 1 < steps)
                def _():
                    @pl.when(j >= 1)
                    def _():
                        put(j - 1, other).wait()

                    gather(j + 1, other).start()

        put(steps - 2, 0).wait()
        put(steps - 1, 1).wait()

    return pl.kernel(
        body,
        out_type=jax.ShapeDtypeStruct((n, width), table.dtype),
        mesh=mesh,
        scratch_types=[
            pltpu.VMEM((steps, GATHER_ROWS), I32),
            pltpu.VMEM((2, GATHER_ROWS, width), table.dtype),
            pltpu.SemaphoreType.DMA((2,)),
            pltpu.SemaphoreType.DMA((2,)),
        ],
        name="peer_gather",
    )(table, idx.reshape(SC_WORKERS, steps, GATHER_ROWS))


def _pack_table(tab):
    half = tab.shape[1] // 2
    bits = lax.bitcast_convert_type(tab.astype(BF16), jnp.uint16).astype(jnp.uint32)
    return lax.bitcast_convert_type(bits[:, :half] | (bits[:, half:] << 16), I32)


def _unpack(w):
    lo = lax.bitcast_convert_type(w << 16, F32)
    hi = lax.bitcast_convert_type(w & jnp.int32(-65536), F32)
    return lo, hi


EXPERT_TOKENS = 32
EXPERT_UNROLL = 4


def _gelu(a):
    return 0.5 * a * (1.0 + lax.erf(a * (2.0 ** -0.5)))


def _expert_kernel(ug_ref, vg_ref, h_ref, gate_ref, x1_ref, mod_ref, ln_ref, o_ref, ffn_ref, *, alpha, tt):
    i = pl.program_id(0)
    groups = LANES // tt
    base = (i % groups) * tt
    nchunk = ug_ref.shape[-1] // LANES
    ntile = PEER_SLOTS // SUBLANES
    lane = lax.broadcasted_iota(I32, (PEER_SLOTS, LANES), 1)

    def u_body(t, a_all):
        h_t = h_ref[t]
        h_lo = [jnp.broadcast_to(h_t[c:c + 1, :], (SUBLANES, LANES)) for c in range(nchunk)]
        h_hi = [jnp.broadcast_to(h_t[nchunk + c:nchunk + c + 1, :], (SUBLANES, LANES)) for c in range(nchunk)]
        cols = []
        for e in range(ntile):
            acc = jnp.zeros((SUBLANES, LANES), F32)
            for c in range(nchunk):
                lo, hi = _unpack(ug_ref[t, pl.ds(e * SUBLANES, SUBLANES), pl.ds(c * LANES, LANES)])
                acc = acc + lo * h_lo[c] + hi * h_hi[c]
            cols.append(jnp.sum(acc, axis=-1, keepdims=True))
        a_col = jnp.concatenate(cols, axis=0)
        return jnp.where(lane == base + t, a_col, a_all)

    a_all = lax.fori_loop(0, tt, u_body, jnp.zeros((PEER_SLOTS, LANES), F32), unroll=EXPERT_UNROLL)
    w_all = _gelu(a_all) * gate_ref[...]

    def v_body(t, carry):
        w_col = jnp.sum(jnp.where(lane == base + t, w_all, 0.0), axis=-1, keepdims=True)
        acc_lo = [jnp.zeros((SUBLANES, LANES), F32) for _ in range(nchunk)]
        acc_hi = [jnp.zeros((SUBLANES, LANES), F32) for _ in range(nchunk)]
        for e in range(ntile):
            wc = jnp.broadcast_to(w_col[e * SUBLANES:(e + 1) * SUBLANES, :], (SUBLANES, LANES))
            for c in range(nchunk):
                lo, hi = _unpack(vg_ref[t, pl.ds(e * SUBLANES, SUBLANES), pl.ds(c * LANES, LANES)])
                acc_lo[c] = acc_lo[c] + wc * lo
                acc_hi[c] = acc_hi[c] + wc * hi
        ffn_ref[t] = jnp.concatenate([jnp.sum(a, axis=0, keepdims=True) for a in acc_lo + acc_hi], axis=0)
        return carry

    lax.fori_loop(0, tt, v_body, 0, unroll=EXPERT_UNROLL)
    y = alpha * x1_ref[...] + mod_ref[5] * ffn_ref[...]
    mu = jnp.mean(y, axis=(1, 2), keepdims=True)
    yc = y - mu
    var = jnp.mean(yc * yc, axis=(1, 2), keepdims=True)
    o_ref[...] = yc * lax.rsqrt(var + LN_EPS) * ln_ref[0] + ln_ref[1]


def _experts(ug, vg, h2, gate_t, x1, mod3, ln2, alpha, b0, seq):
    tc = ug.shape[0]
    ft = h2.shape[1:]
    tt = EXPERT_TOKENS
    groups = LANES // tt
    tok = lambda i: (i, 0, 0)
    return pl.pallas_call(
        functools.partial(_expert_kernel, alpha=alpha, tt=tt),
        grid=(tc // tt,),
        in_specs=[
            pl.BlockSpec((tt,) + ug.shape[1:], tok),
            pl.BlockSpec((tt,) + vg.shape[1:], tok),
            pl.BlockSpec((tt,) + ft, tok),
            pl.BlockSpec((None, PEER_SLOTS, LANES), lambda i: (i // groups, 0, 0)),
            pl.BlockSpec((tt,) + ft, tok),
            pl.BlockSpec((None, 6) + ft, lambda i: (b0 + (i * tt) // seq, 0, 0, 0)),
            pl.BlockSpec((2,) + ft, lambda i: (0, 0, 0)),
        ],
        out_specs=pl.BlockSpec((tt,) + ft, tok),
        out_shape=jax.ShapeDtypeStruct((tc,) + ft, F32),
        scratch_shapes=[pltpu.VMEM((tt,) + ft, F32)],
        compiler_params=pltpu.CompilerParams(dimension_semantics=("arbitrary",), vmem_limit_bytes=VMEM_LIMIT),
        name="peer_experts_ln2",
    )(ug, vg, h2, gate_t, x1, mod3, ln2)


CHUNK_BATCHES = 1
CHUNK_LAG = 3


def _layer_chunk(x, b0, nb, mod3, slopes, lam_vecs, lambda_init, alpha, w):
    _, seq, d = x.shape
    ft = (d // LANES, LANES)
    proj = _inproj(x, mod3, w["w_in"], b0, nb)
    diff_out = _diff_attention(proj, slopes, lam_vecs, w["subln_g"], lambda_init)
    swa_out = _swa_attention(proj, slopes, w["sinks"])
    x1, h2, q = _mid(diff_out, swa_out, x, mod3, w["w_out"], w["ln1"], w["w_pq"], alpha, b0)
    idx, gate_t = _route(q.reshape(nb * seq, -1), w["sub_keys"])
    flat = idx.reshape(-1)
    ug = _gather_rows(w["u_pack"], flat).reshape(nb * seq, PEER_SLOTS, -1)
    vg = _gather_rows(w["v_pack"], flat).reshape(nb * seq, PEER_SLOTS, -1)
    return _experts(ug, vg, h2.reshape((nb * seq,) + ft), gate_t, x1.reshape((nb * seq,) + ft),
                    mod3.reshape(mod3.shape[:2] + ft), w["ln2"].reshape((2,) + ft), alpha, b0, seq)


def kernel(x, c, w_ada, b_ada, w_in, lambda_q1, lambda_k1, lambda_q2, lambda_k2, subln_g, sinks, w_out, ln1_g, ln1_b, w_pq, sub_keys, u_tab, v_tab, ln2_g, ln2_b):
    bsz, seq, d = x.shape
    depth = w_ada.shape[0]
    alpha = (2 * depth) ** 0.25
    slopes = jnp.exp2(-8.0 * jnp.arange(1, N_ATT_HEADS + 1, dtype=F32) / N_ATT_HEADS)
    nb = CHUNK_BATCHES
    for l in range(depth):
        lambda_init = 0.8 - 0.6 * math.exp(-0.3 * l)
        mod3 = _mod(c, w_ada[l], b_ada[l]).reshape(bsz, 6, d)
        lam_vecs = jnp.stack([lambda_q1[l], lambda_k1[l], lambda_q2[l], lambda_k2[l]])
        w = dict(w_in=w_in[l].astype(BF16), subln_g=subln_g[l], sinks=sinks[l], w_out=w_out[l].astype(BF16),
                 ln1=jnp.stack([ln1_g[l], ln1_b[l]]), w_pq=w_pq[l].astype(BF16),
                 sub_keys=sub_keys[l].astype(BF16), u_pack=_pack_table(u_tab[l]), v_pack=_pack_table(v_tab[l]),
                 ln2=jnp.stack([ln2_g[l], ln2_b[l]]))
        outs = []
        for ci, b0 in enumerate(range(0, bsz, nb)):
            mod_c = mod3
            if ci >= CHUNK_LAG:
                mod_c, outs[ci - CHUNK_LAG] = lax.optimization_barrier((mod3, outs[ci - CHUNK_LAG]))
            outs.append(_layer_chunk(x, b0, nb, mod_c, slopes, lam_vecs, lambda_init, alpha, w))
        x = jnp.concatenate(outs, axis=0).reshape(bsz, seq, d)
    return x
```

```python
import functools
import math

import jax
import jax.numpy as jnp
from jax import lax
from jax.experimental import pallas as pl
from jax.experimental.pallas import tpu as pltpu
from jax.experimental.pallas import tpu_sc as plsc

F32 = jnp.float32
BF16 = jnp.bfloat16
I32 = jnp.int32

HEAD_DIM = 64
DIFF_HEADS = 4
DIFF_V = 2 * HEAD_DIM
SWA_Q_HEADS = 8
SWA_KV_HEADS = 2
SWA_GROUP = SWA_Q_HEADS // SWA_KV_HEADS
WINDOW = 128
N_ATT_HEADS = SWA_Q_HEADS + DIFF_HEADS
PEER_HEADS = 8
N_KEYS = 128
PEER_HALF = 128
PEER_TOPK = 16
PEER_SLOTS = PEER_HEADS * PEER_TOPK
LN_EPS = 1e-5
NEG_INF = -1e30

LANES = 128
SUBLANES = 8
VMEM_LIMIT = 48 * 1024 * 1024

SC_CORES = 2
SC_SUBCORES = 16
SC_WORKERS = SC_CORES * SC_SUBCORES
GATHER_ROWS = 64


def _nt_dot(a, b):
    return lax.dot_general(a, b, (((1,), (1,)), ((), ())), preferred_element_type=F32)


def _mod_kernel(c_ref, w_ref, b_ref, o_ref):
    c = c_ref[...]
    s = c * (1.0 / (1.0 + jnp.exp(-c)))
    o_ref[...] = jnp.dot(s.astype(BF16), w_ref[...].astype(BF16), preferred_element_type=F32) + b_ref[...]


def _mod(c, w, b):
    bsz, d = c.shape
    n = w.shape[1]
    tn = 768
    return pl.pallas_call(
        _mod_kernel,
        grid=(n // tn,),
        in_specs=[
            pl.BlockSpec((bsz, d), lambda j: (0, 0)),
            pl.BlockSpec((d, tn), lambda j: (0, j)),
            pl.BlockSpec((1, tn), lambda j: (0, j)),
        ],
        out_specs=pl.BlockSpec((bsz, tn), lambda j: (0, j)),
        out_shape=jax.ShapeDtypeStruct((bsz, n), F32),
        compiler_params=pltpu.CompilerParams(dimension_semantics=("arbitrary",), vmem_limit_bytes=VMEM_LIMIT),
        name="adaln_mod",
    )(c, w, b.reshape(1, n))


def _inproj_kernel(x_ref, mod_ref, w_ref, o_ref):
    h = x_ref[...] * (1.0 + mod_ref[1:2, :]) + mod_ref[0:1, :]
    o_ref[...] = jnp.dot(h.astype(BF16), w_ref[...], preferred_element_type=F32).astype(BF16)


def _inproj(x, mod3, w_bf16, b0, bsz):
    _, s, d = x.shape
    n = w_bf16.shape[1]
    tm = 512
    return pl.pallas_call(
        _inproj_kernel,
        grid=(bsz, s // tm),
        in_specs=[
            pl.BlockSpec((None, tm, d), lambda b, i: (b0 + b, i, 0)),
            pl.BlockSpec((None, 6, d), lambda b, i: (b0 + b, 0, 0)),
            pl.BlockSpec((d, n), lambda b, i: (0, 0)),
        ],
        out_specs=pl.BlockSpec((None, tm, n), lambda b, i: (b, i, 0)),
        out_shape=jax.ShapeDtypeStruct((bsz, s, n), BF16),
        compiler_params=pltpu.CompilerParams(
            dimension_semantics=("arbitrary", "arbitrary"), vmem_limit_bytes=VMEM_LIMIT),
        name="in_proj",
    )(x, mod3, w_bf16)


def _diff_kernel(slopes_ref, q_ref, k_ref, v_ref, lam_ref, g_ref, o_ref, *, tq, lambda_init):
    h = pl.program_id(1)
    i = pl.program_id(2)
    slope = slopes_ref[SWA_Q_HEADS + h]
    scale = HEAD_DIM ** -0.5
    q = q_ref[...]
    qs = (q[:, :HEAD_DIM], q[:, HEAD_DIM:])
    row = (i * tq + lax.broadcasted_iota(I32, (tq, 1), 0)).astype(F32)

    def body(j, carry):
        ks = k_ref[pl.ds(pl.multiple_of(j * tq, tq), tq), :]
        vs = v_ref[pl.ds(pl.multiple_of(j * tq, tq), tq), :]
        col = (j * tq + lax.broadcasted_iota(I32, (1, tq), 1)).astype(F32)
        dist = row - col
        valid = dist >= 0.0
        bias = -slope * dist
        new = []
        for m in range(2):
            mx, l, acc = carry[3 * m: 3 * m + 3]
            s = _nt_dot(qs[m], ks[:, m * HEAD_DIM:(m + 1) * HEAD_DIM]) * scale
            s = jnp.where(valid, s + bias, NEG_INF)
            mx_new = jnp.maximum(mx, jnp.max(s, axis=-1, keepdims=True))
            p = jnp.exp(s - mx_new)
            corr = jnp.exp(mx - mx_new)
            l = l * corr + jnp.sum(p, axis=-1, keepdims=True)
            acc = acc * corr + jnp.dot(p.astype(BF16), vs, preferred_element_type=F32)
            new += [mx_new, l, acc]
        return tuple(new)

    init = []
    for _ in range(2):
        init += [jnp.full((tq, 1), NEG_INF, F32), jnp.zeros((tq, 1), F32), jnp.zeros((tq, DIFF_V), F32)]
    m0, l0, a0, m1, l1, a1 = lax.fori_loop(0, i + 1, body, tuple(init))

    lam_v = lam_ref[...]
    lam = (jnp.exp(jnp.sum(lam_v[0:1, :] * lam_v[1:2, :], axis=-1, keepdims=True))
           - jnp.exp(jnp.sum(lam_v[2:3, :] * lam_v[3:4, :], axis=-1, keepdims=True)) + lambda_init)
    o = a0 / l0 - lam * (a1 / l1)
    o = o * lax.rsqrt(jnp.mean(o * o, axis=-1, keepdims=True) + LN_EPS)
    o_ref[...] = (o * g_ref[...] * (1.0 - lambda_init)).astype(BF16)


def _diff_attention(proj, slopes, lam_vecs, subln_g, lambda_init):
    bsz, s, _ = proj.shape
    tq = 256
    kcol = DIFF_HEADS
    vcol = 2 * DIFF_HEADS
    return pl.pallas_call(
        functools.partial(_diff_kernel, tq=tq, lambda_init=lambda_init),
        grid=(bsz, DIFF_HEADS, s // tq),
        in_specs=[
            pl.BlockSpec(memory_space=pltpu.SMEM),
            pl.BlockSpec((None, tq, DIFF_V), lambda b, h, i: (b, i, h)),
            pl.BlockSpec((None, s, DIFF_V), lambda b, h, i: (b, 0, kcol + h)),
            pl.BlockSpec((None, s, DIFF_V), lambda b, h, i: (b, 0, vcol + h)),
            pl.BlockSpec((4, HEAD_DIM), lambda b, h, i: (0, 0)),
            pl.BlockSpec((1, DIFF_V), lambda b, h, i: (0, 0)),
        ],
        out_specs=pl.BlockSpec((None, tq, DIFF_V), lambda b, h, i: (b, i, h)),
        out_shape=jax.ShapeDtypeStruct((bsz, s, DIFF_HEADS * DIFF_V), BF16),
        compiler_params=pltpu.CompilerParams(
            dimension_semantics=("arbitrary", "arbitrary", "arbitrary"), vmem_limit_bytes=VMEM_LIMIT),
        name="diff_attention",
    )(slopes, proj, proj, proj, lam_vecs, subln_g.reshape(1, DIFF_V))


def _swa_kernel(slopes_ref, sinks_ref, q_ref, k_ref, v_ref, o_ref, *, tq):
    i = pl.program_id(1)
    scale = HEAD_DIM ** -0.5
    blk = WINDOW
    ii = lax.broadcasted_iota(I32, (blk, 2 * blk), 0)
    jj = lax.broadcasted_iota(I32, (blk, 2 * blk), 1)
    for r in range(tq // blk):
        start = i * tq + r * blk
        kstart = jnp.maximum(start - blk, 0)
        kb = k_ref[pl.ds(pl.multiple_of(kstart, blk), 2 * blk), :]
        vb = v_ref[pl.ds(pl.multiple_of(kstart, blk), 2 * blk), :]
        dist = (start + ii) - (kstart + jj)
        valid = (dist >= 0) & (dist < WINDOW)
        distf = dist.astype(F32)
        outs = []
        for kvh in range(SWA_KV_HEADS):
            k = kb[:, kvh * HEAD_DIM:(kvh + 1) * HEAD_DIM]
            v = vb[:, kvh * HEAD_DIM:(kvh + 1) * HEAD_DIM]
            for g in range(SWA_GROUP):
                hq = kvh * SWA_GROUP + g
                qh = q_ref[r * blk:(r + 1) * blk, hq * HEAD_DIM:(hq + 1) * HEAD_DIM]
                s = _nt_dot(qh, k) * scale - slopes_ref[hq] * distf
                s = jnp.where(valid, s, NEG_INF)
                sink = sinks_ref[hq]
                m = jnp.maximum(jnp.max(s, axis=-1, keepdims=True), sink)
                p = jnp.exp(s - m)
                denom = jnp.sum(p, axis=-1, keepdims=True) + jnp.exp(sink - m)
                outs.append(jnp.dot(p.astype(BF16), v, preferred_element_type=F32) / denom)
        o_ref[r * blk:(r + 1) * blk, :] = jnp.concatenate(outs, axis=-1).astype(BF16)


def _swa_attention(proj, slopes, sinks):
    bsz, s, _ = proj.shape
    tq = 256
    width = SWA_Q_HEADS * HEAD_DIM
    qcol = (3 * DIFF_HEADS * DIFF_V) // width
    kcol = (3 * DIFF_HEADS * DIFF_V + width) // LANES
    return pl.pallas_call(
        functools.partial(_swa_kernel, tq=tq),
        grid=(bsz, s // tq),
        in_specs=[
            pl.BlockSpec(memory_space=pltpu.SMEM),
            pl.BlockSpec(memory_space=pltpu.SMEM),
            pl.BlockSpec((None, tq, width), lambda b, i: (b, i, qcol)),
            pl.BlockSpec((None, s, LANES), lambda b, i: (b, 0, kcol)),
            pl.BlockSpec((None, s, LANES), lambda b, i: (b, 0, kcol + 1)),
        ],
        out_specs=pl.BlockSpec((None, tq, width), lambda b, i: (b, i, 0)),
        out_shape=jax.ShapeDtypeStruct((bsz, s, width), BF16),
        compiler_params=pltpu.CompilerParams(
            dimension_semantics=("arbitrary", "arbitrary"), vmem_limit_bytes=VMEM_LIMIT),
        name="swa_attention",
    )(slopes, sinks, proj, proj, proj)


def _layer_norm(y, g, b):
    mu = jnp.mean(y, axis=-1, keepdims=True)
    yc = y - mu
    var = jnp.mean(yc * yc, axis=-1, keepdims=True)
    return yc * lax.rsqrt(var + LN_EPS) * g + b


def _mid_kernel(do_ref, so_ref, x_ref, mod_ref, wo_ref, ln_ref, wpq_ref, x1_ref, h2_ref, q_ref, *, alpha):
    nd = do_ref.shape[-1]
    mixed = (jnp.dot(do_ref[...], wo_ref[:nd, :], preferred_element_type=F32)
             + jnp.dot(so_ref[...], wo_ref[nd:, :], preferred_element_type=F32))
    y = alpha * x_ref[...] + mod_ref[2:3, :] * mixed
    x1 = _layer_norm(y, ln_ref[0:1, :], ln_ref[1:2, :])
    x1_ref[...] = x1
    h2 = (x1 * (1.0 + mod_ref[4:5, :]) + mod_ref[3:4, :]).astype(BF16)
    h2_ref[...] = h2.astype(F32)
    q_ref[...] = jnp.dot(h2, wpq_ref[...], preferred_element_type=F32).astype(BF16)


def _mid(diff_out, swa_out, x, mod3, wo_bf16, ln1, wpq_bf16, alpha, b0):
    bsz = diff_out.shape[0]
    _, s, d = x.shape
    nq = wpq_bf16.shape[1]
    tm = 512
    row = lambda b, i: (b, i, 0)
    const = lambda b, i: (0, 0)
    return pl.pallas_call(
        functools.partial(_mid_kernel, alpha=alpha),
        grid=(bsz, s // tm),
        in_specs=[
            pl.BlockSpec((None, tm, diff_out.shape[-1]), row),
            pl.BlockSpec((None, tm, swa_out.shape[-1]), row),
            pl.BlockSpec((None, tm, d), lambda b, i: (b0 + b, i, 0)),
            pl.BlockSpec((None, 6, d), lambda b, i: (b0 + b, 0, 0)),
            pl.BlockSpec(wo_bf16.shape, const),
            pl.BlockSpec((2, d), const),
            pl.BlockSpec(wpq_bf16.shape, const),
        ],
        out_specs=[
            pl.BlockSpec((None, tm, d), row),
            pl.BlockSpec((None, tm, d), row),
            pl.BlockSpec((None, tm, nq), row),
        ],
        out_shape=[
            jax.ShapeDtypeStruct((bsz, s, d), F32),
            jax.ShapeDtypeStruct((bsz, s, d), F32),
            jax.ShapeDtypeStruct((bsz, s, nq), BF16),
        ],
        compiler_params=pltpu.CompilerParams(
            dimension_semantics=("arbitrary", "arbitrary"), vmem_limit_bytes=VMEM_LIMIT),
        name="outproj_ln1_peerq",
    )(diff_out, swa_out, x, mod3, wo_bf16, ln1, wpq_bf16)


def _topk_rows(vals, pos, payload, k):
    out_v, out_p = [], []
    for _ in range(k):
        m = jnp.max(vals, axis=0, keepdims=True)
        first = jnp.min(jnp.where(vals == m, pos, 1e9), axis=0, keepdims=True)
        sel = pos == first
        if payload is None:
            out_p.append(first)
        else:
            out_p.append(jnp.max(jnp.where(sel, payload, -1.0), axis=0, keepdims=True))
        out_v.append(m)
        vals = jnp.where(sel, -jnp.inf, vals)
    return jnp.concatenate(out_v, axis=0), jnp.concatenate(out_p, axis=0)


def _candidates(v0, i0, v1, i1):
    k, lanes = v0.shape
    vals, poss, eids = [], [], []
    for a in range(4):
        nb = k if a == 0 else k // 2
        b_iota = lax.broadcasted_iota(I32, (nb, lanes), 0).astype(F32)
        vals.append(v0[a:a + 1, :] + v1[:nb, :])
        poss.append(a * k + b_iota)
        eids.append(i0[a:a + 1, :] * N_KEYS + i1[:nb, :])
    for b in range(3):
        na = k if b == 0 else k // 2
        a_iota = lax.broadcasted_iota(I32, (na, lanes), 0).astype(F32)
        vals.append(jnp.where(a_iota >= 4.0, v0[:na, :] + v1[b:b + 1, :], -jnp.inf))
        poss.append(a_iota * k + b)
        eids.append(i0[:na, :] * N_KEYS + i1[b:b + 1, :])
    return jnp.concatenate(vals, axis=0), jnp.concatenate(poss, axis=0), jnp.concatenate(eids, axis=0)


def _route_kernel(q_ref, keys_ref, idx_ref, gate_ref, idx_t, gate_t):
    tt = q_ref.shape[0]
    key_pos = lax.broadcasted_iota(I32, (N_KEYS, tt), 0).astype(F32)

    def head(h, carry):
        halves = []
        for p in range(2):
            qh = q_ref[:, pl.ds(pl.multiple_of((2 * h + p) * PEER_HALF, PEER_HALF), PEER_HALF)]
            sc = _nt_dot(keys_ref[h, p], qh)
            halves.append(_topk_rows(sc, key_pos, None, PEER_TOPK))
        (v0, i0), (v1, i1) = halves
        cv, cp, ce = _candidates(v0, i0, v1, i1)
        top_s, top_e = _topk_rows(cv, cp, ce, PEER_TOPK)
        e = jnp.exp(top_s - top_s[0:1, :])
        gate = e / jnp.sum(e, axis=0, keepdims=True)
        rows = pl.ds(pl.multiple_of(h * PEER_TOPK, PEER_TOPK), PEER_TOPK)
        idx_t[rows, :] = top_e
        gate_t[rows, :] = gate
        return carry

    lax.fori_loop(0, PEER_HEADS, head, 0)
    idx_ref[...] = idx_t[...].T.astype(I32)
    gate_ref[...] = gate_t[...]


def _route(q2d, keys_bf16):
    t, nq = q2d.shape
    tt = LANES
    return pl.pallas_call(
        _route_kernel,
        grid=(t // tt,),
        in_specs=[
            pl.BlockSpec((tt, nq), lambda i: (i, 0)),
            pl.BlockSpec(keys_bf16.shape, lambda i: (0, 0, 0, 0)),
        ],
        out_specs=[
            pl.BlockSpec((tt, PEER_SLOTS), lambda i: (i, 0)),
            pl.BlockSpec((None, PEER_SLOTS, tt), lambda i: (i, 0, 0)),
        ],
        out_shape=[
            jax.ShapeDtypeStruct((t, PEER_SLOTS), I32),
            jax.ShapeDtypeStruct((t // tt, PEER_SLOTS, tt), F32),
        ],
        scratch_shapes=[pltpu.VMEM((PEER_SLOTS, tt), F32), pltpu.VMEM((PEER_SLOTS, tt), F32)],
        compiler_params=pltpu.CompilerParams(dimension_semantics=("arbitrary",), vmem_limit_bytes=VMEM_LIMIT),
        name="peer_route",
    )(q2d, keys_bf16)


def _gather_rows(table, idx):
    n = idx.shape[0]
    width = table.shape[1]
    per_worker = n // SC_WORKERS
    steps = per_worker // GATHER_ROWS
    assert steps * GATHER_ROWS * SC_WORKERS == n and steps % 2 == 0
    mesh = plsc.VectorSubcoreMesh(core_axis_name="c", subcore_axis_name="s")

    def body(table_hbm, idx_hbm, out_hbm, idx_v, rows_v, gsem, osem):
        wid = lax.axis_index("s") * SC_CORES + lax.axis_index("c")
        pltpu.sync_copy(idx_hbm.at[wid], idx_v)

        def gather(j, slot):
            return pltpu.make_async_copy(table_hbm.at[idx_v.at[j]], rows_v.at[slot], gsem.at[slot])

        def put(j, slot):
            dst = out_hbm.at[pl.ds(wid * per_worker + j * GATHER_ROWS, GATHER_ROWS)]
            return pltpu.make_async_copy(rows_v.at[slot], dst, osem.at[slot])

        gather(0, 0).start()

        @pl.loop(0, steps, step=2)
        def _(j0):
            for slot in range(2):
                j = j0 + slot
                other = 1 - slot
                gather(j, slot).wait()
                put(j, slot).start()

                @pl.when(j + 1 < steps)
                def _():
                    @pl.when(j >= 1)
                    def _():
                        put(j - 1, other).wait()

                    gather(j + 1, other).start()

        put(steps - 2, 0).wait()
        put(steps - 1, 1).wait()

    return pl.kernel(
        body,
        out_type=jax.ShapeDtypeStruct((n, width), table.dtype),
        mesh=mesh,
        scratch_types=[
            pltpu.VMEM((steps, GATHER_ROWS), I32),
            pltpu.VMEM((2, GATHER_ROWS, width), table.dtype),
            pltpu.SemaphoreType.DMA((2,)),
            pltpu.SemaphoreType.DMA((2,)),
        ],
        name="peer_gather",
    )(table, idx.reshape(SC_WORKERS, steps, GATHER_ROWS))


def _pack_table(tab):
    half = tab.shape[1] // 2
    bits = lax.bitcast_convert_type(tab.astype(BF16), jnp.uint16).astype(jnp.uint32)
    return lax.bitcast_convert_type(bits[:, :half] | (bits[:, half:] << 16), I32)


SC_LANES = 16
UDOT_ROWS = 64
UDOT_GROUP = 16


def _udot_rows(rows_ref, h_ref, out_ref, out_base):
    nrows, width = rows_ref.shape
    nchunk = width // SC_LANES
    lane = lax.iota(I32, SC_LANES)

    @pl.loop(0, nrows // UDOT_GROUP)
    def _(g):
        r0 = g * UDOT_GROUP

        def chunk(c, accs):
            off = pl.multiple_of(c * SC_LANES, SC_LANES)
            h_lo = h_ref[pl.ds(off, SC_LANES)]
            h_hi = h_ref[pl.ds(width + off, SC_LANES)]
            out = []
            for r in range(UDOT_GROUP):
                w = rows_ref[r0 + r, pl.ds(off, SC_LANES)]
                lo = lax.bitcast_convert_type(w << 16, F32)
                hi = lax.bitcast_convert_type(w & jnp.int32(-65536), F32)
                out.append(accs[r] + lo * h_lo + hi * h_hi)
            return tuple(out)

        zero = jnp.zeros((SC_LANES,), F32)
        accs = lax.fori_loop(0, nchunk, chunk, (zero,) * UDOT_GROUP)
        vec = zero
        for r in range(UDOT_GROUP):
            vec = jnp.where(lane == r, jnp.sum(accs[r]), vec)
        out_ref[pl.ds(pl.multiple_of(out_base + r0, UDOT_GROUP), UDOT_GROUP)] = vec


def _sc_udot(table, idx, h):
    t, k = idx.shape
    width = table.shape[1]
    tpw = t // SC_WORKERS
    halves = k // UDOT_ROWS
    assert tpw * SC_WORKERS == t and halves == 2
    nbuf = tpw * halves
    mesh = plsc.VectorSubcoreMesh(core_axis_name="c", subcore_axis_name="s")

    def body(table_hbm, idx_hbm, h_hbm, out_hbm, idx_v, rows_v, h_v, a_v, gsem, hsem):
        wid = lax.axis_index("s") * SC_CORES + lax.axis_index("c")
        tok0 = wid * tpw
        pltpu.sync_copy(idx_hbm.at[wid], idx_v)

        def h_slot(slot):
            return h_v.at[pl.ds(slot * 2 * width, 2 * width)]

        def gather(j, slot):
            rows = idx_v.at[pl.ds(pl.multiple_of(j * UDOT_ROWS, UDOT_ROWS), UDOT_ROWS)]
            return pltpu.make_async_copy(table_hbm.at[rows], rows_v.at[slot], gsem.at[slot])

        def h_copy(tok, slot):
            return pltpu.make_async_copy(h_hbm.at[tok0 + tok], h_slot(slot), hsem.at[slot])

        gather(0, 0).start()
        h_copy(0, 0).start()

        @pl.loop(0, tpw, step=2)
        def _(t0):
            for hs in range(2):
                tok = t0 + hs
                h_copy(tok, hs).wait()

                @pl.when(tok + 1 < tpw)
                def _():
                    h_copy(tok + 1, 1 - hs).start()

                for slot in range(halves):
                    j = tok * halves + slot
                    gather(j, slot).wait()

                    @pl.when(j + 1 < nbuf)
                    def _():
                        gather(j + 1, 1 - slot).start()

                    _udot_rows(rows_v.at[slot], h_slot(hs), a_v, tok * k + slot * UDOT_ROWS)

        pltpu.sync_copy(a_v, out_hbm.at[pl.ds(pl.multiple_of(tok0 * k, 8), tpw * k)])

    out = pl.kernel(
        body,
        out_type=jax.ShapeDtypeStruct((t * k,), F32),
        mesh=mesh,
        scratch_types=[
            pltpu.VMEM((nbuf * UDOT_ROWS,), I32),
            pltpu.VMEM((2, UDOT_ROWS, width), table.dtype),
            pltpu.VMEM((2 * 2 * width,), F32),
            pltpu.VMEM((tpw * k,), F32),
            pltpu.SemaphoreType.DMA((2,)),
            pltpu.SemaphoreType.DMA((2,)),
        ],
        compiler_params=pltpu.CompilerParams(needs_layout_passes=False),
        name="peer_udot",
    )(table, idx.reshape(SC_WORKERS, nbuf * UDOT_ROWS), h)
    return out.reshape(t, k)


def _unpack(w):
    lo = lax.bitcast_convert_type(w << 16, F32)
    hi = lax.bitcast_convert_type(w & jnp.int32(-65536), F32)
    return lo, hi


EXPERT_TOKENS = 32
EXPERT_UNROLL = 4


def _gelu(a):
    return 0.5 * a * (1.0 + lax.erf(a * (2.0 ** -0.5)))


def _expert_kernel(a_ref, vg_ref, gate_ref, x1_ref, mod_ref, ln_ref, o_ref, ffn_ref, *, alpha, tt):
    i = pl.program_id(0)
    groups = LANES // tt
    base = (i % groups) * tt
    nchunk = vg_ref.shape[-1] // LANES
    ntile = PEER_SLOTS // SUBLANES
    lane = lax.broadcasted_iota(I32, (PEER_SLOTS, LANES), 1)
    w_all = _gelu(a_ref[...].T) * gate_ref[...]

    def v_body(t, carry):
        w_col = jnp.sum(jnp.where(lane == base + t, w_all, 0.0), axis=-1, keepdims=True)
        acc_lo = [jnp.zeros((SUBLANES, LANES), F32) for _ in range(nchunk)]
        acc_hi = [jnp.zeros((SUBLANES, LANES), F32) for _ in range(nchunk)]
        for e in range(ntile):
            wc = jnp.broadcast_to(w_col[e * SUBLANES:(e + 1) * SUBLANES, :], (SUBLANES, LANES))
            for c in range(nchunk):
                lo, hi = _unpack(vg_ref[t, pl.ds(e * SUBLANES, SUBLANES), pl.ds(c * LANES, LANES)])
                acc_lo[c] = acc_lo[c] + wc * lo
                acc_hi[c] = acc_hi[c] + wc * hi
        ffn_ref[t] = jnp.concatenate([jnp.sum(a, axis=0, keepdims=True) for a in acc_lo + acc_hi], axis=0)
        return carry

    lax.fori_loop(0, tt, v_body, 0, unroll=EXPERT_UNROLL)
    y = alpha * x1_ref[...] + mod_ref[5] * ffn_ref[...]
    mu = jnp.mean(y, axis=(1, 2), keepdims=True)
    yc = y - mu
    var = jnp.mean(yc * yc, axis=(1, 2), keepdims=True)
    o_ref[...] = yc * lax.rsqrt(var + LN_EPS) * ln_ref[0] + ln_ref[1]


def _experts(a, vg, gate_t, x1, mod3, ln2, alpha, b0, seq):
    tc = vg.shape[0]
    ft = x1.shape[1:]
    tt = EXPERT_TOKENS
    groups = LANES // tt
    tok = lambda i: (i, 0, 0)
    return pl.pallas_call(
        functools.partial(_expert_kernel, alpha=alpha, tt=tt),
        grid=(tc // tt,),
        in_specs=[
            pl.BlockSpec((LANES, PEER_SLOTS), lambda i: (i // groups, 0)),
            pl.BlockSpec((tt,) + vg.shape[1:], tok),
            pl.BlockSpec((None, PEER_SLOTS, LANES), lambda i: (i // groups, 0, 0)),
            pl.BlockSpec((tt,) + ft, tok),
            pl.BlockSpec((None, 6) + ft, lambda i: (b0 + (i * tt) // seq, 0, 0, 0)),
            pl.BlockSpec((2,) + ft, lambda i: (0, 0, 0)),
        ],
        out_specs=pl.BlockSpec((tt,) + ft, tok),
        out_shape=jax.ShapeDtypeStruct((tc,) + ft, F32),
        scratch_shapes=[pltpu.VMEM((tt,) + ft, F32)],
        compiler_params=pltpu.CompilerParams(dimension_semantics=("arbitrary",), vmem_limit_bytes=VMEM_LIMIT),
        name="peer_experts_ln2",
    )(a, vg, gate_t, x1, mod3, ln2)


CHUNK_BATCHES = 1
CHUNK_LAG = 3


def _layer_chunk(x, b0, nb, mod3, slopes, lam_vecs, lambda_init, alpha, w):
    _, seq, d = x.shape
    ft = (d // LANES, LANES)
    proj = _inproj(x, mod3, w["w_in"], b0, nb)
    diff_out = _diff_attention(proj, slopes, lam_vecs, w["subln_g"], lambda_init)
    swa_out = _swa_attention(proj, slopes, w["sinks"])
    x1, h2, q = _mid(diff_out, swa_out, x, mod3, w["w_out"], w["ln1"], w["w_pq"], alpha, b0)
    idx, gate_t = _route(q.reshape(nb * seq, -1), w["sub_keys"])
    a = _sc_udot(w["u_pack"], idx, h2.reshape(nb * seq, d))
    vg = _gather_rows(w["v_pack"], idx.reshape(-1)).reshape(nb * seq, PEER_SLOTS, -1)
    return _experts(a, vg, gate_t, x1.reshape((nb * seq,) + ft),
                    mod3.reshape(mod3.shape[:2] + ft), w["ln2"].reshape((2,) + ft), alpha, b0, seq)


def kernel(x, c, w_ada, b_ada, w_in, lambda_q1, lambda_k1, lambda_q2, lambda_k2, subln_g, sinks, w_out, ln1_g, ln1_b, w_pq, sub_keys, u_tab, v_tab, ln2_g, ln2_b):
    bsz, seq, d = x.shape
    depth = w_ada.shape[0]
    alpha = (2 * depth) ** 0.25
    slopes = jnp.exp2(-8.0 * jnp.arange(1, N_ATT_HEADS + 1, dtype=F32) / N_ATT_HEADS)
    nb = CHUNK_BATCHES
    for l in range(depth):
        lambda_init = 0.8 - 0.6 * math.exp(-0.3 * l)
        mod3 = _mod(c, w_ada[l], b_ada[l]).reshape(bsz, 6, d)
        lam_vecs = jnp.stack([lambda_q1[l], lambda_k1[l], lambda_q2[l], lambda_k2[l]])
        w = dict(w_in=w_in[l].astype(BF16), subln_g=subln_g[l], sinks=sinks[l], w_out=w_out[l].astype(BF16),
                 ln1=jnp.stack([ln1_g[l], ln1_b[l]]), w_pq=w_pq[l].astype(BF16),
                 sub_keys=sub_keys[l].astype(BF16), u_pack=_pack_table(u_tab[l]), v_pack=_pack_table(v_tab[l]),
                 ln2=jnp.stack([ln2_g[l], ln2_b[l]]))
        outs = []
        for ci, b0 in enumerate(range(0, bsz, nb)):
            mod_c = mod3
            if ci >= CHUNK_LAG:
                mod_c, outs[ci - CHUNK_LAG] = lax.optimization_barrier((mod3, outs[ci - CHUNK_LAG]))
            outs.append(_layer_chunk(x, b0, nb, mod_c, slopes, lam_vecs, lambda_init, alpha, w))
        x = jnp.concatenate(outs, axis=0).reshape(bsz, seq, d)
    return x
```

```python
import functools
import math

import jax
import jax.numpy as jnp
from jax import lax
from jax.experimental import pallas as pl
from jax.experimental.pallas import tpu as pltpu
from jax.experimental.pallas import tpu_sc as plsc

F32 = jnp.float32
BF16 = jnp.bfloat16
I32 = jnp.int32

HEAD_DIM = 64
DIFF_HEADS = 4
DIFF_V = 2 * HEAD_DIM
SWA_Q_HEADS = 8
SWA_KV_HEADS = 2
SWA_GROUP = SWA_Q_HEADS // SWA_KV_HEADS
WINDOW = 128
N_ATT_HEADS = SWA_Q_HEADS + DIFF_HEADS
PEER_HEADS = 8
N_KEYS = 128
PEER_HALF = 128
PEER_TOPK = 16
PEER_SLOTS = PEER_HEADS * PEER_TOPK
LN_EPS = 1e-5
NEG_INF = -1e30

LANES = 128
VMEM_LIMIT = 48 * 1024 * 1024

SC_CORES = 2
SC_SUBCORES = 16
SC_WORKERS = SC_CORES * SC_SUBCORES


def _nt_dot(a, b):
    return lax.dot_general(a, b, (((1,), (1,)), ((), ())), preferred_element_type=F32)


def _mod_kernel(c_ref, w_ref, b_ref, o_ref):
    c = c_ref[...]
    s = c * (1.0 / (1.0 + jnp.exp(-c)))
    o_ref[...] = jnp.dot(s.astype(BF16), w_ref[...].astype(BF16), preferred_element_type=F32) + b_ref[...]


def _mod(c, w, b):
    bsz, d = c.shape
    n = w.shape[1]
    tn = 768
    return pl.pallas_call(
        _mod_kernel,
        grid=(n // tn,),
        in_specs=[
            pl.BlockSpec((bsz, d), lambda j: (0, 0)),
            pl.BlockSpec((d, tn), lambda j: (0, j)),
            pl.BlockSpec((1, tn), lambda j: (0, j)),
        ],
        out_specs=pl.BlockSpec((bsz, tn), lambda j: (0, j)),
        out_shape=jax.ShapeDtypeStruct((bsz, n), F32),
        compiler_params=pltpu.CompilerParams(dimension_semantics=("arbitrary",), vmem_limit_bytes=VMEM_LIMIT),
        name="adaln_mod",
    )(c, w, b.reshape(1, n))


def _inproj_kernel(x_ref, mod_ref, w_ref, o_ref):
    h = x_ref[...] * (1.0 + mod_ref[1:2, :]) + mod_ref[0:1, :]
    o_ref[...] = jnp.dot(h.astype(BF16), w_ref[...], preferred_element_type=F32).astype(BF16)


def _inproj(x, mod3, w_bf16, b0, bsz):
    _, s, d = x.shape
    n = w_bf16.shape[1]
    tm = 512
    return pl.pallas_call(
        _inproj_kernel,
        grid=(bsz, s // tm),
        in_specs=[
            pl.BlockSpec((None, tm, d), lambda b, i: (b0 + b, i, 0)),
            pl.BlockSpec((None, 6, d), lambda b, i: (b0 + b, 0, 0)),
            pl.BlockSpec((d, n), lambda b, i: (0, 0)),
        ],
        out_specs=pl.BlockSpec((None, tm, n), lambda b, i: (b, i, 0)),
        out_shape=jax.ShapeDtypeStruct((bsz, s, n), BF16),
        compiler_params=pltpu.CompilerParams(
            dimension_semantics=("arbitrary", "arbitrary"), vmem_limit_bytes=VMEM_LIMIT),
        name="in_proj",
    )(x, mod3, w_bf16)


def _diff_kernel(slopes_ref, q_ref, k_ref, v_ref, lam_ref, g_ref, o_ref, *, tq, lambda_init):
    h = pl.program_id(1)
    i = pl.program_id(2)
    slope = slopes_ref[SWA_Q_HEADS + h]
    scale = HEAD_DIM ** -0.5
    q = q_ref[...]
    qs = (q[:, :HEAD_DIM], q[:, HEAD_DIM:])
    row = (i * tq + lax.broadcasted_iota(I32, (tq, 1), 0)).astype(F32)

    def body(j, carry):
        ks = k_ref[pl.ds(pl.multiple_of(j * tq, tq), tq), :]
        vs = v_ref[pl.ds(pl.multiple_of(j * tq, tq), tq), :]
        col = (j * tq + lax.broadcasted_iota(I32, (1, tq), 1)).astype(F32)
        dist = row - col
        valid = dist >= 0.0
        bias = -slope * dist
        new = []
        for m in range(2):
            mx, l, acc = carry[3 * m: 3 * m + 3]
            s = _nt_dot(qs[m], ks[:, m * HEAD_DIM:(m + 1) * HEAD_DIM]) * scale
            s = jnp.where(valid, s + bias, NEG_INF)
            mx_new = jnp.maximum(mx, jnp.max(s, axis=-1, keepdims=True))
            p = jnp.exp(s - mx_new)
            corr = jnp.exp(mx - mx_new)
            l = l * corr + jnp.sum(p, axis=-1, keepdims=True)
            acc = acc * corr + jnp.dot(p.astype(BF16), vs, preferred_element_type=F32)
            new += [mx_new, l, acc]
        return tuple(new)

    init = []
    for _ in range(2):
        init += [jnp.full((tq, 1), NEG_INF, F32), jnp.zeros((tq, 1), F32), jnp.zeros((tq, DIFF_V), F32)]
    m0, l0, a0, m1, l1, a1 = lax.fori_loop(0, i + 1, body, tuple(init))

    lam_v = lam_ref[...]
    lam = (jnp.exp(jnp.sum(lam_v[0:1, :] * lam_v[1:2, :], axis=-1, keepdims=True))
           - jnp.exp(jnp.sum(lam_v[2:3, :] * lam_v[3:4, :], axis=-1, keepdims=True)) + lambda_init)
    o = a0 / l0 - lam * (a1 / l1)
    o = o * lax.rsqrt(jnp.mean(o * o, axis=-1, keepdims=True) + LN_EPS)
    o_ref[...] = (o * g_ref[...] * (1.0 - lambda_init)).astype(BF16)


def _diff_attention(proj, slopes, lam_vecs, subln_g, lambda_init):
    bsz, s, _ = proj.shape
    tq = 256
    kcol = DIFF_HEADS
    vcol = 2 * DIFF_HEADS
    return pl.pallas_call(
        functools.partial(_diff_kernel, tq=tq, lambda_init=lambda_init),
        grid=(bsz, DIFF_HEADS, s // tq),
        in_specs=[
            pl.BlockSpec(memory_space=pltpu.SMEM),
            pl.BlockSpec((None, tq, DIFF_V), lambda b, h, i: (b, i, h)),
            pl.BlockSpec((None, s, DIFF_V), lambda b, h, i: (b, 0, kcol + h)),
            pl.BlockSpec((None, s, DIFF_V), lambda b, h, i: (b, 0, vcol + h)),
            pl.BlockSpec((4, HEAD_DIM), lambda b, h, i: (0, 0)),
            pl.BlockSpec((1, DIFF_V), lambda b, h, i: (0, 0)),
        ],
        out_specs=pl.BlockSpec((None, tq, DIFF_V), lambda b, h, i: (b, i, h)),
        out_shape=jax.ShapeDtypeStruct((bsz, s, DIFF_HEADS * DIFF_V), BF16),
        compiler_params=pltpu.CompilerParams(
            dimension_semantics=("arbitrary", "arbitrary", "arbitrary"), vmem_limit_bytes=VMEM_LIMIT),
        name="diff_attention",
    )(slopes, proj, proj, proj, lam_vecs, subln_g.reshape(1, DIFF_V))


def _swa_kernel(slopes_ref, sinks_ref, q_ref, k_ref, v_ref, o_ref, *, tq):
    i = pl.program_id(1)
    scale = HEAD_DIM ** -0.5
    blk = WINDOW
    ii = lax.broadcasted_iota(I32, (blk, 2 * blk), 0)
    jj = lax.broadcasted_iota(I32, (blk, 2 * blk), 1)
    for r in range(tq // blk):
        start = i * tq + r * blk
        kstart = jnp.maximum(start - blk, 0)
        kb = k_ref[pl.ds(pl.multiple_of(kstart, blk), 2 * blk), :]
        vb = v_ref[pl.ds(pl.multiple_of(kstart, blk), 2 * blk), :]
        dist = (start + ii) - (kstart + jj)
        valid = (dist >= 0) & (dist < WINDOW)
        distf = dist.astype(F32)
        outs = []
        for kvh in range(SWA_KV_HEADS):
            k = kb[:, kvh * HEAD_DIM:(kvh + 1) * HEAD_DIM]
            v = vb[:, kvh * HEAD_DIM:(kvh + 1) * HEAD_DIM]
            for g in range(SWA_GROUP):
                hq = kvh * SWA_GROUP + g
                qh = q_ref[r * blk:(r + 1) * blk, hq * HEAD_DIM:(hq + 1) * HEAD_DIM]
                s = _nt_dot(qh, k) * scale - slopes_ref[hq] * distf
                s = jnp.where(valid, s, NEG_INF)
                sink = sinks_ref[hq]
                m = jnp.maximum(jnp.max(s, axis=-1, keepdims=True), sink)
                p = jnp.exp(s - m)
                denom = jnp.sum(p, axis=-1, keepdims=True) + jnp.exp(sink - m)
                outs.append(jnp.dot(p.astype(BF16), v, preferred_element_type=F32) / denom)
        o_ref[r * blk:(r + 1) * blk, :] = jnp.concatenate(outs, axis=-1).astype(BF16)


def _swa_attention(proj, slopes, sinks):
    bsz, s, _ = proj.shape
    tq = 256
    width = SWA_Q_HEADS * HEAD_DIM
    qcol = (3 * DIFF_HEADS * DIFF_V) // width
    kcol = (3 * DIFF_HEADS * DIFF_V + width) // LANES
    return pl.pallas_call(
        functools.partial(_swa_kernel, tq=tq),
        grid=(bsz, s // tq),
        in_specs=[
            pl.BlockSpec(memory_space=pltpu.SMEM),
            pl.BlockSpec(memory_space=pltpu.SMEM),
            pl.BlockSpec((None, tq, width), lambda b, i: (b, i, qcol)),
            pl.BlockSpec((None, s, LANES), lambda b, i: (b, 0, kcol)),
            pl.BlockSpec((None, s, LANES), lambda b, i: (b, 0, kcol + 1)),
        ],
        out_specs=pl.BlockSpec((None, tq, width), lambda b, i: (b, i, 0)),
        out_shape=jax.ShapeDtypeStruct((bsz, s, width), BF16),
        compiler_params=pltpu.CompilerParams(
            dimension_semantics=("arbitrary", "arbitrary"), vmem_limit_bytes=VMEM_LIMIT),
        name="swa_attention",
    )(slopes, sinks, proj, proj, proj)


def _layer_norm(y, g, b):
    mu = jnp.mean(y, axis=-1, keepdims=True)
    yc = y - mu
    var = jnp.mean(yc * yc, axis=-1, keepdims=True)
    return yc * lax.rsqrt(var + LN_EPS) * g + b


def _mid_kernel(do_ref, so_ref, x_ref, mod_ref, wo_ref, ln_ref, wpq_ref, x1_ref, h2_ref, q_ref, *, alpha):
    nd = do_ref.shape[-1]
    mixed = (jnp.dot(do_ref[...], wo_ref[:nd, :], preferred_element_type=F32)
             + jnp.dot(so_ref[...], wo_ref[nd:, :], preferred_element_type=F32))
    y = alpha * x_ref[...] + mod_ref[2:3, :] * mixed
    x1 = _layer_norm(y, ln_ref[0:1, :], ln_ref[1:2, :])
    x1_ref[...] = x1
    h2 = (x1 * (1.0 + mod_ref[4:5, :]) + mod_ref[3:4, :]).astype(BF16)
    h2_ref[...] = h2.astype(F32)
    q_ref[...] = jnp.dot(h2, wpq_ref[...], preferred_element_type=F32).astype(BF16)


def _mid(diff_out, swa_out, x, mod3, wo_bf16, ln1, wpq_bf16, alpha, b0):
    bsz = diff_out.shape[0]
    _, s, d = x.shape
    nq = wpq_bf16.shape[1]
    tm = 512
    row = lambda b, i: (b, i, 0)
    const = lambda b, i: (0, 0)
    return pl.pallas_call(
        functools.partial(_mid_kernel, alpha=alpha),
        grid=(bsz, s // tm),
        in_specs=[
            pl.BlockSpec((None, tm, diff_out.shape[-1]), row),
            pl.BlockSpec((None, tm, swa_out.shape[-1]), row),
            pl.BlockSpec((None, tm, d), lambda b, i: (b0 + b, i, 0)),
            pl.BlockSpec((None, 6, d), lambda b, i: (b0 + b, 0, 0)),
            pl.BlockSpec(wo_bf16.shape, const),
            pl.BlockSpec((2, d), const),
            pl.BlockSpec(wpq_bf16.shape, const),
        ],
        out_specs=[
            pl.BlockSpec((None, tm, d), row),
            pl.BlockSpec((None, tm, d), row),
            pl.BlockSpec((None, tm, nq), row),
        ],
        out_shape=[
            jax.ShapeDtypeStruct((bsz, s, d), F32),
            jax.ShapeDtypeStruct((bsz, s, d), F32),
            jax.ShapeDtypeStruct((bsz, s, nq), BF16),
        ],
        compiler_params=pltpu.CompilerParams(
            dimension_semantics=("arbitrary", "arbitrary"), vmem_limit_bytes=VMEM_LIMIT),
        name="outproj_ln1_peerq",
    )(diff_out, swa_out, x, mod3, wo_bf16, ln1, wpq_bf16)


def _topk_rows(vals, pos, payload, k):
    out_v, out_p = [], []
    for _ in range(k):
        m = jnp.max(vals, axis=0, keepdims=True)
        first = jnp.min(jnp.where(vals == m, pos, 1e9), axis=0, keepdims=True)
        sel = pos == first
        if payload is None:
            out_p.append(first)
        else:
            out_p.append(jnp.max(jnp.where(sel, payload, -1.0), axis=0, keepdims=True))
        out_v.append(m)
        vals = jnp.where(sel, -jnp.inf, vals)
    return jnp.concatenate(out_v, axis=0), jnp.concatenate(out_p, axis=0)


def _candidates(v0, i0, v1, i1):
    k, lanes = v0.shape
    vals, poss, eids = [], [], []
    for a in range(4):
        nb = k if a == 0 else k // 2
        b_iota = lax.broadcasted_iota(I32, (nb, lanes), 0).astype(F32)
        vals.append(v0[a:a + 1, :] + v1[:nb, :])
        poss.append(a * k + b_iota)
        eids.append(i0[a:a + 1, :] * N_KEYS + i1[:nb, :])
    for b in range(3):
        na = k if b == 0 else k // 2
        a_iota = lax.broadcasted_iota(I32, (na, lanes), 0).astype(F32)
        vals.append(jnp.where(a_iota >= 4.0, v0[:na, :] + v1[b:b + 1, :], -jnp.inf))
        poss.append(a_iota * k + b)
        eids.append(i0[:na, :] * N_KEYS + i1[b:b + 1, :])
    return jnp.concatenate(vals, axis=0), jnp.concatenate(poss, axis=0), jnp.concatenate(eids, axis=0)


def _route_kernel(q_ref, keys_ref, idx_ref, gate_ref, idx_t, gate_t):
    tt = q_ref.shape[0]
    key_pos = lax.broadcasted_iota(I32, (N_KEYS, tt), 0).astype(F32)

    def head(h, carry):
        halves = []
        for p in range(2):
            qh = q_ref[:, pl.ds(pl.multiple_of((2 * h + p) * PEER_HALF, PEER_HALF), PEER_HALF)]
            sc = _nt_dot(keys_ref[h, p], qh)
            halves.append(_topk_rows(sc, key_pos, None, PEER_TOPK))
        (v0, i0), (v1, i1) = halves
        cv, cp, ce = _candidates(v0, i0, v1, i1)
        top_s, top_e = _topk_rows(cv, cp, ce, PEER_TOPK)
        e = jnp.exp(top_s - top_s[0:1, :])
        gate = e / jnp.sum(e, axis=0, keepdims=True)
        rows = pl.ds(pl.multiple_of(h * PEER_TOPK, PEER_TOPK), PEER_TOPK)
        idx_t[rows, :] = top_e
        gate_t[rows, :] = gate
        return carry

    lax.fori_loop(0, PEER_HEADS, head, 0)
    idx_ref[...] = idx_t[...].T.astype(I32)
    gate_ref[...] = gate_t[...].T


def _route(q2d, keys_bf16):
    t, nq = q2d.shape
    tt = LANES
    return pl.pallas_call(
        _route_kernel,
        grid=(t // tt,),
        in_specs=[
            pl.BlockSpec((tt, nq), lambda i: (i, 0)),
            pl.BlockSpec(keys_bf16.shape, lambda i: (0, 0, 0, 0)),
        ],
        out_specs=[
            pl.BlockSpec((tt, PEER_SLOTS), lambda i: (i, 0)),
            pl.BlockSpec((tt, PEER_SLOTS), lambda i: (i, 0)),
        ],
        out_shape=[
            jax.ShapeDtypeStruct((t, PEER_SLOTS), I32),
            jax.ShapeDtypeStruct((t, PEER_SLOTS), F32),
        ],
        scratch_shapes=[pltpu.VMEM((PEER_SLOTS, tt), F32), pltpu.VMEM((PEER_SLOTS, tt), F32)],
        compiler_params=pltpu.CompilerParams(dimension_semantics=("arbitrary",), vmem_limit_bytes=VMEM_LIMIT),
        name="peer_route",
    )(q2d, keys_bf16)


def _pack_table(tab):
    half = tab.shape[1] // 2
    bits = lax.bitcast_convert_type(tab.astype(BF16), jnp.uint16).astype(jnp.uint32)
    return lax.bitcast_convert_type(bits[:, :half] | (bits[:, half:] << 16), I32)


SC_LANES = 16
UDOT_ROWS = 64
UDOT_GROUP = 16


def _udot_rows(rows_ref, h_ref, out_ref, out_base):
    nrows, width = rows_ref.shape
    nchunk = width // SC_LANES
    lane = lax.iota(I32, SC_LANES)

    @pl.loop(0, nrows // UDOT_GROUP)
    def _(g):
        r0 = g * UDOT_GROUP

        def chunk(c, accs):
            off = pl.multiple_of(c * SC_LANES, SC_LANES)
            h_lo = h_ref[pl.ds(off, SC_LANES)]
            h_hi = h_ref[pl.ds(width + off, SC_LANES)]
            out = []
            for r in range(UDOT_GROUP):
                w = rows_ref[r0 + r, pl.ds(off, SC_LANES)]
                lo = lax.bitcast_convert_type(w << 16, F32)
                hi = lax.bitcast_convert_type(w & jnp.int32(-65536), F32)
                out.append(accs[r] + lo * h_lo + hi * h_hi)
            return tuple(out)

        zero = jnp.zeros((SC_LANES,), F32)
        accs = lax.fori_loop(0, nchunk, chunk, (zero,) * UDOT_GROUP)
        vec = zero
        for r in range(UDOT_GROUP):
            vec = jnp.where(lane == r, jnp.sum(accs[r]), vec)
        out_ref[pl.ds(pl.multiple_of(out_base + r0, UDOT_GROUP), UDOT_GROUP)] = vec


def _sc_udot(table, idx, h):
    t, k = idx.shape
    width = table.shape[1]
    tpw = t // SC_WORKERS
    halves = k // UDOT_ROWS
    assert tpw * SC_WORKERS == t and halves == 2
    nbuf = tpw * halves
    mesh = plsc.VectorSubcoreMesh(core_axis_name="c", subcore_axis_name="s")

    def body(table_hbm, idx_hbm, h_hbm, out_hbm, idx_v, rows_v, h_v, a_v, gsem, hsem):
        wid = lax.axis_index("s") * SC_CORES + lax.axis_index("c")
        tok0 = wid * tpw
        pltpu.sync_copy(idx_hbm.at[wid], idx_v)

        def h_slot(slot):
            return h_v.at[pl.ds(slot * 2 * width, 2 * width)]

        def gather(j, slot):
            rows = idx_v.at[pl.ds(pl.multiple_of(j * UDOT_ROWS, UDOT_ROWS), UDOT_ROWS)]
            return pltpu.make_async_copy(table_hbm.at[rows], rows_v.at[slot], gsem.at[slot])

        def h_copy(tok, slot):
            return pltpu.make_async_copy(h_hbm.at[tok0 + tok], h_slot(slot), hsem.at[slot])

        gather(0, 0).start()
        h_copy(0, 0).start()

        @pl.loop(0, tpw, step=2)
        def _(t0):
            for hs in range(2):
                tok = t0 + hs
                h_copy(tok, hs).wait()

                @pl.when(tok + 1 < tpw)
                def _():
                    h_copy(tok + 1, 1 - hs).start()

                for slot in range(halves):
                    j = tok * halves + slot
                    gather(j, slot).wait()

                    @pl.when(j + 1 < nbuf)
                    def _():
                        gather(j + 1, 1 - slot).start()

                    _udot_rows(rows_v.at[slot], h_slot(hs), a_v, tok * k + slot * UDOT_ROWS)

        pltpu.sync_copy(a_v, out_hbm.at[pl.ds(pl.multiple_of(tok0 * k, 8), tpw * k)])

    out = pl.kernel(
        body,
        out_type=jax.ShapeDtypeStruct((t * k,), F32),
        mesh=mesh,
        scratch_types=[
            pltpu.VMEM((nbuf * UDOT_ROWS,), I32),
            pltpu.VMEM((2, UDOT_ROWS, width), table.dtype),
            pltpu.VMEM((2 * 2 * width,), F32),
            pltpu.VMEM((tpw * k,), F32),
            pltpu.SemaphoreType.DMA((2,)),
            pltpu.SemaphoreType.DMA((2,)),
        ],
        compiler_params=pltpu.CompilerParams(needs_layout_passes=False),
        name="peer_udot",
    )(table, idx.reshape(SC_WORKERS, nbuf * UDOT_ROWS), h)
    return out.reshape(t, k)


VSUM_CHUNKS = 8


def _vsum_rows(rows_ref, wgt_ref, wgt_base, out_ref, out_base, first):
    nrows, width = rows_ref.shape
    span = VSUM_CHUNKS * SC_LANES

    @pl.loop(0, width // span)
    def _(blk):
        col0 = pl.multiple_of(blk * span, span)

        def row(r, accs):
            wv = plsc.load_gather(wgt_ref, [jnp.full((SC_LANES,), wgt_base + r, I32)])
            out = []
            for c in range(VSUM_CHUNKS):
                w = rows_ref[r, pl.ds(col0 + c * SC_LANES, SC_LANES)]
                lo = lax.bitcast_convert_type(w << 16, F32)
                hi = lax.bitcast_convert_type(w & jnp.int32(-65536), F32)
                out.append(accs[2 * c] + wv * lo)
                out.append(accs[2 * c + 1] + wv * hi)
            return tuple(out)

        zero = jnp.zeros((SC_LANES,), F32)
        accs = lax.fori_loop(0, nrows, row, (zero,) * (2 * VSUM_CHUNKS))
        for c in range(VSUM_CHUNKS):
            for half in range(2):
                dst = pl.ds(pl.multiple_of(out_base + half * width + col0 + c * SC_LANES, SC_LANES), SC_LANES)
                if first:
                    out_ref[dst] = accs[2 * c + half]
                else:
                    out_ref[dst] = out_ref[dst] + accs[2 * c + half]


def _sc_vsum(table, idx, wgt):
    t, k = idx.shape
    width = table.shape[1]
    d = 2 * width
    tpw = t // SC_WORKERS
    halves = k // UDOT_ROWS
    assert tpw * SC_WORKERS == t and halves == 2 and tpw % 2 == 0
    nbuf = tpw * halves
    mesh = plsc.VectorSubcoreMesh(core_axis_name="c", subcore_axis_name="s")

    def body(table_hbm, idx_hbm, wgt_hbm, out_hbm, idx_v, rows_v, wgt_v, out_v, gsem, osem):
        wid = lax.axis_index("s") * SC_CORES + lax.axis_index("c")
        tok0 = wid * tpw
        pltpu.sync_copy(idx_hbm.at[wid], idx_v)
        pltpu.sync_copy(wgt_hbm.at[wid], wgt_v)

        def gather(j, slot):
            rows = idx_v.at[pl.ds(pl.multiple_of(j * UDOT_ROWS, UDOT_ROWS), UDOT_ROWS)]
            return pltpu.make_async_copy(table_hbm.at[rows], rows_v.at[slot], gsem.at[slot])

        def put(tok, slot):
            return pltpu.make_async_copy(out_v.at[pl.ds(slot * d, d)], out_hbm.at[tok0 + tok], osem.at[slot])

        gather(0, 0).start()

        @pl.loop(0, tpw, step=2)
        def _(t0):
            for os_ in range(2):
                tok = t0 + os_

                @pl.when(tok >= 2)
                def _():
                    put(tok - 2, os_).wait()

                for slot in range(halves):
                    j = tok * halves + slot
                    gather(j, slot).wait()

                    @pl.when(j + 1 < nbuf)
                    def _():
                        gather(j + 1, 1 - slot).start()

                    _vsum_rows(rows_v.at[slot], wgt_v, tok * k + slot * UDOT_ROWS, out_v, os_ * d, slot == 0)
                put(tok, os_).start()

        put(tpw - 2, 0).wait()
        put(tpw - 1, 1).wait()

    return pl.kernel(
        body,
        out_type=jax.ShapeDtypeStruct((t, d), F32),
        mesh=mesh,
        scratch_types=[
            pltpu.VMEM((nbuf * UDOT_ROWS,), I32),
            pltpu.VMEM((2, UDOT_ROWS, width), table.dtype),
            pltpu.VMEM((tpw * k,), F32),
            pltpu.VMEM((2 * d,), F32),
            pltpu.SemaphoreType.DMA((2,)),
            pltpu.SemaphoreType.DMA((2,)),
        ],
        compiler_params=pltpu.CompilerParams(needs_layout_passes=False),
        name="peer_vsum",
    )(table, idx.reshape(SC_WORKERS, nbuf * UDOT_ROWS), wgt.reshape(SC_WORKERS, tpw * k))


def _wgt_kernel(a_ref, gate_ref, o_ref):
    a = a_ref[...]
    o_ref[...] = gate_ref[...] * (0.5 * a * (1.0 + lax.erf(a * (2.0 ** -0.5))))


def _expert_weights(a, gate):
    t, k = a.shape
    tm = math.gcd(t, 1024)
    spec = pl.BlockSpec((tm, k), lambda i: (i, 0))
    return pl.pallas_call(
        _wgt_kernel,
        grid=(t // tm,),
        in_specs=[spec, spec],
        out_specs=spec,
        out_shape=jax.ShapeDtypeStruct((t, k), F32),
        compiler_params=pltpu.CompilerParams(dimension_semantics=("arbitrary",), vmem_limit_bytes=VMEM_LIMIT),
        name="peer_weights",
    )(a, gate)


def _final_kernel(x1_ref, ffn_ref, mod_ref, ln_ref, o_ref, *, alpha):
    y = alpha * x1_ref[...] + mod_ref[5:6, :] * ffn_ref[...]
    o_ref[...] = _layer_norm(y, ln_ref[0:1, :], ln_ref[1:2, :])


def _final(x1, ffn, mod3, ln2, alpha, b0):
    nb, s, d = x1.shape
    tm = 512
    row = lambda b, i: (b, i, 0)
    return pl.pallas_call(
        functools.partial(_final_kernel, alpha=alpha),
        grid=(nb, s // tm),
        in_specs=[
            pl.BlockSpec((None, tm, d), row),
            pl.BlockSpec((None, tm, d), row),
            pl.BlockSpec((None, 6, d), lambda b, i: (b0 + b, 0, 0)),
            pl.BlockSpec((2, d), lambda b, i: (0, 0)),
        ],
        out_specs=pl.BlockSpec((None, tm, d), row),
        out_shape=jax.ShapeDtypeStruct((nb, s, d), F32),
        compiler_params=pltpu.CompilerParams(
            dimension_semantics=("arbitrary", "arbitrary"), vmem_limit_bytes=VMEM_LIMIT),
        name="deepnorm_ln2",
    )(x1, ffn, mod3, ln2)


CHUNK_BATCHES = 1
CHUNK_LAG = 3


def _layer_chunk(x, b0, nb, mod3, slopes, lam_vecs, lambda_init, alpha, w):
    _, seq, d = x.shape
    proj = _inproj(x, mod3, w["w_in"], b0, nb)
    diff_out = _diff_attention(proj, slopes, lam_vecs, w["subln_g"], lambda_init)
    swa_out = _swa_attention(proj, slopes, w["sinks"])
    x1, h2, q = _mid(diff_out, swa_out, x, mod3, w["w_out"], w["ln1"], w["w_pq"], alpha, b0)
    idx, gate = _route(q.reshape(nb * seq, -1), w["sub_keys"])
    a = _sc_udot(w["u_pack"], idx, h2.reshape(nb * seq, d))
    ffn = _sc_vsum(w["v_pack"], idx, _expert_weights(a, gate))
    return _final(x1, ffn.reshape(nb, seq, d), mod3, w["ln2"], alpha, b0)


def kernel(x, c, w_ada, b_ada, w_in, lambda_q1, lambda_k1, lambda_q2, lambda_k2, subln_g, sinks, w_out, ln1_g, ln1_b, w_pq, sub_keys, u_tab, v_tab, ln2_g, ln2_b):
    bsz, seq, d = x.shape
    depth = w_ada.shape[0]
    alpha = (2 * depth) ** 0.25
    slopes = jnp.exp2(-8.0 * jnp.arange(1, N_ATT_HEADS + 1, dtype=F32) / N_ATT_HEADS)
    nb = CHUNK_BATCHES
    for l in range(depth):
        lambda_init = 0.8 - 0.6 * math.exp(-0.3 * l)
        mod3 = _mod(c, w_ada[l], b_ada[l]).reshape(bsz, 6, d)
        lam_vecs = jnp.stack([lambda_q1[l], lambda_k1[l], lambda_q2[l], lambda_k2[l]])
        w = dict(w_in=w_in[l].astype(BF16), subln_g=subln_g[l], sinks=sinks[l], w_out=w_out[l].astype(BF16),
                 ln1=jnp.stack([ln1_g[l], ln1_b[l]]), w_pq=w_pq[l].astype(BF16),
                 sub_keys=sub_keys[l].astype(BF16), u_pack=_pack_table(u_tab[l]), v_pack=_pack_table(v_tab[l]),
                 ln2=jnp.stack([ln2_g[l], ln2_b[l]]))
        outs = []
        for ci, b0 in enumerate(range(0, bsz, nb)):
            mod_c = mod3
            if ci >= CHUNK_LAG:
                mod_c, outs[ci - CHUNK_LAG] = lax.optimization_barrier((mod3, outs[ci - CHUNK_LAG]))
            outs.append(_layer_chunk(x, b0, nb, mod_c, slopes, lam_vecs, lambda_init, alpha, w))
        x = jnp.concatenate(outs, axis=0).reshape(bsz, seq, d)
    return x
```

```python
import functools
import math

import jax
import jax.numpy as jnp
from jax import lax
from jax.experimental import pallas as pl
from jax.experimental.pallas import tpu as pltpu
from jax.experimental.pallas import tpu_sc as plsc

F32 = jnp.float32
BF16 = jnp.bfloat16
I32 = jnp.int32

HEAD_DIM = 64
DIFF_HEADS = 4
DIFF_V = 2 * HEAD_DIM
SWA_Q_HEADS = 8
SWA_KV_HEADS = 2
SWA_GROUP = SWA_Q_HEADS // SWA_KV_HEADS
WINDOW = 128
N_ATT_HEADS = SWA_Q_HEADS + DIFF_HEADS
PEER_HEADS = 8
N_KEYS = 128
PEER_HALF = 128
PEER_TOPK = 16
PEER_SLOTS = PEER_HEADS * PEER_TOPK
LN_EPS = 1e-5
NEG_INF = -1e30

LANES = 128
VMEM_LIMIT = 48 * 1024 * 1024

SC_CORES = 2
SC_SUBCORES = 16
SC_WORKERS = SC_CORES * SC_SUBCORES


def _nt_dot(a, b):
    return lax.dot_general(a, b, (((1,), (1,)), ((), ())), preferred_element_type=F32)


def _mod_kernel(c_ref, w_ref, b_ref, o_ref):
    c = c_ref[...]
    s = c * (1.0 / (1.0 + jnp.exp(-c)))
    o_ref[...] = jnp.dot(s.astype(BF16), w_ref[...].astype(BF16), preferred_element_type=F32) + b_ref[...]


def _mod(c, w, b):
    bsz, d = c.shape
    n = w.shape[1]
    tn = 768
    return pl.pallas_call(
        _mod_kernel,
        grid=(n // tn,),
        in_specs=[
            pl.BlockSpec((bsz, d), lambda j: (0, 0)),
            pl.BlockSpec((d, tn), lambda j: (0, j)),
            pl.BlockSpec((1, tn), lambda j: (0, j)),
        ],
        out_specs=pl.BlockSpec((bsz, tn), lambda j: (0, j)),
        out_shape=jax.ShapeDtypeStruct((bsz, n), F32),
        compiler_params=pltpu.CompilerParams(dimension_semantics=("arbitrary",), vmem_limit_bytes=VMEM_LIMIT),
        name="adaln_mod",
    )(c, w, b.reshape(1, n))


def _inproj_kernel(x_ref, mod_ref, w_ref, o_ref):
    h = x_ref[...] * (1.0 + mod_ref[1:2, :]) + mod_ref[0:1, :]
    o_ref[...] = jnp.dot(h.astype(BF16), w_ref[...], preferred_element_type=F32).astype(BF16)


def _inproj(x, mod3, w_bf16, b0, bsz):
    _, s, d = x.shape
    n = w_bf16.shape[1]
    tm = 512
    return pl.pallas_call(
        _inproj_kernel,
        grid=(bsz, s // tm),
        in_specs=[
            pl.BlockSpec((None, tm, d), lambda b, i: (b0 + b, i, 0)),
            pl.BlockSpec((None, 6, d), lambda b, i: (b0 + b, 0, 0)),
            pl.BlockSpec((d, n), lambda b, i: (0, 0)),
        ],
        out_specs=pl.BlockSpec((None, tm, n), lambda b, i: (b, i, 0)),
        out_shape=jax.ShapeDtypeStruct((bsz, s, n), BF16),
        compiler_params=pltpu.CompilerParams(
            dimension_semantics=("arbitrary", "arbitrary"), vmem_limit_bytes=VMEM_LIMIT),
        name="in_proj",
    )(x, mod3, w_bf16)


def _diff_kernel(slopes_ref, q_ref, k_ref, v_ref, lam_ref, g_ref, o_ref, *, tq, lambda_init):
    h = pl.program_id(1)
    i = pl.program_id(2)
    slope = slopes_ref[SWA_Q_HEADS + h]
    scale = HEAD_DIM ** -0.5
    q = q_ref[...]
    qs = (q[:, :HEAD_DIM], q[:, HEAD_DIM:])
    row = (i * tq + lax.broadcasted_iota(I32, (tq, 1), 0)).astype(F32)

    def body(j, carry):
        ks = k_ref[pl.ds(pl.multiple_of(j * tq, tq), tq), :]
        vs = v_ref[pl.ds(pl.multiple_of(j * tq, tq), tq), :]
        col = (j * tq + lax.broadcasted_iota(I32, (1, tq), 1)).astype(F32)
        dist = row - col
        valid = dist >= 0.0
        bias = -slope * dist
        new = []
        for m in range(2):
            mx, l, acc = carry[3 * m: 3 * m + 3]
            s = _nt_dot(qs[m], ks[:, m * HEAD_DIM:(m + 1) * HEAD_DIM]) * scale
            s = jnp.where(valid, s + bias, NEG_INF)
            mx_new = jnp.maximum(mx, jnp.max(s, axis=-1, keepdims=True))
            p = jnp.exp(s - mx_new)
            corr = jnp.exp(mx - mx_new)
            l = l * corr + jnp.sum(p, axis=-1, keepdims=True)
            acc = acc * corr + jnp.dot(p.astype(BF16), vs, preferred_element_type=F32)
            new += [mx_new, l, acc]
        return tuple(new)

    init = []
    for _ in range(2):
        init += [jnp.full((tq, 1), NEG_INF, F32), jnp.zeros((tq, 1), F32), jnp.zeros((tq, DIFF_V), F32)]
    m0, l0, a0, m1, l1, a1 = lax.fori_loop(0, i + 1, body, tuple(init))

    lam_v = lam_ref[...]
    lam = (jnp.exp(jnp.sum(lam_v[0:1, :] * lam_v[1:2, :], axis=-1, keepdims=True))
           - jnp.exp(jnp.sum(lam_v[2:3, :] * lam_v[3:4, :], axis=-1, keepdims=True)) + lambda_init)
    o = a0 / l0 - lam * (a1 / l1)
    o = o * lax.rsqrt(jnp.mean(o * o, axis=-1, keepdims=True) + LN_EPS)
    o_ref[...] = (o * g_ref[...] * (1.0 - lambda_init)).astype(BF16)


def _diff_attention(proj, slopes, lam_vecs, subln_g, lambda_init):
    bsz, s, _ = proj.shape
    tq = 256
    kcol = DIFF_HEADS
    vcol = 2 * DIFF_HEADS
    return pl.pallas_call(
        functools.partial(_diff_kernel, tq=tq, lambda_init=lambda_init),
        grid=(bsz, DIFF_HEADS, s // tq),
        in_specs=[
            pl.BlockSpec(memory_space=pltpu.SMEM),
            pl.BlockSpec((None, tq, DIFF_V), lambda b, h, i: (b, i, h)),
            pl.BlockSpec((None, s, DIFF_V), lambda b, h, i: (b, 0, kcol + h)),
            pl.BlockSpec((None, s, DIFF_V), lambda b, h, i: (b, 0, vcol + h)),
            pl.BlockSpec((4, HEAD_DIM), lambda b, h, i: (0, 0)),
            pl.BlockSpec((1, DIFF_V), lambda b, h, i: (0, 0)),
        ],
        out_specs=pl.BlockSpec((None, tq, DIFF_V), lambda b, h, i: (b, i, h)),
        out_shape=jax.ShapeDtypeStruct((bsz, s, DIFF_HEADS * DIFF_V), BF16),
        compiler_params=pltpu.CompilerParams(
            dimension_semantics=("arbitrary", "arbitrary", "arbitrary"), vmem_limit_bytes=VMEM_LIMIT),
        name="diff_attention",
    )(slopes, proj, proj, proj, lam_vecs, subln_g.reshape(1, DIFF_V))


def _swa_kernel(slopes_ref, sinks_ref, q_ref, k_ref, v_ref, o_ref, *, tq):
    i = pl.program_id(1)
    scale = HEAD_DIM ** -0.5
    blk = WINDOW
    ii = lax.broadcasted_iota(I32, (blk, 2 * blk), 0)
    jj = lax.broadcasted_iota(I32, (blk, 2 * blk), 1)
    for r in range(tq // blk):
        start = i * tq + r * blk
        kstart = jnp.maximum(start - blk, 0)
        kb = k_ref[pl.ds(pl.multiple_of(kstart, blk), 2 * blk), :]
        vb = v_ref[pl.ds(pl.multiple_of(kstart, blk), 2 * blk), :]
        dist = (start + ii) - (kstart + jj)
        valid = (dist >= 0) & (dist < WINDOW)
        distf = dist.astype(F32)
        outs = []
        for kvh in range(SWA_KV_HEADS):
            k = kb[:, kvh * HEAD_DIM:(kvh + 1) * HEAD_DIM]
            v = vb[:, kvh * HEAD_DIM:(kvh + 1) * HEAD_DIM]
            for g in range(SWA_GROUP):
                hq = kvh * SWA_GROUP + g
                qh = q_ref[r * blk:(r + 1) * blk, hq * HEAD_DIM:(hq + 1) * HEAD_DIM]
                s = _nt_dot(qh, k) * scale - slopes_ref[hq] * distf
                s = jnp.where(valid, s, NEG_INF)
                sink = sinks_ref[hq]
                m = jnp.maximum(jnp.max(s, axis=-1, keepdims=True), sink)
                p = jnp.exp(s - m)
                denom = jnp.sum(p, axis=-1, keepdims=True) + jnp.exp(sink - m)
                outs.append(jnp.dot(p.astype(BF16), v, preferred_element_type=F32) / denom)
        o_ref[r * blk:(r + 1) * blk, :] = jnp.concatenate(outs, axis=-1).astype(BF16)


def _swa_attention(proj, slopes, sinks):
    bsz, s, _ = proj.shape
    tq = 256
    width = SWA_Q_HEADS * HEAD_DIM
    qcol = (3 * DIFF_HEADS * DIFF_V) // width
    kcol = (3 * DIFF_HEADS * DIFF_V + width) // LANES
    return pl.pallas_call(
        functools.partial(_swa_kernel, tq=tq),
        grid=(bsz, s // tq),
        in_specs=[
            pl.BlockSpec(memory_space=pltpu.SMEM),
            pl.BlockSpec(memory_space=pltpu.SMEM),
            pl.BlockSpec((None, tq, width), lambda b, i: (b, i, qcol)),
            pl.BlockSpec((None, s, LANES), lambda b, i: (b, 0, kcol)),
            pl.BlockSpec((None, s, LANES), lambda b, i: (b, 0, kcol + 1)),
        ],
        out_specs=pl.BlockSpec((None, tq, width), lambda b, i: (b, i, 0)),
        out_shape=jax.ShapeDtypeStruct((bsz, s, width), BF16),
        compiler_params=pltpu.CompilerParams(
            dimension_semantics=("arbitrary", "arbitrary"), vmem_limit_bytes=VMEM_LIMIT),
        name="swa_attention",
    )(slopes, sinks, proj, proj, proj)


def _layer_norm(y, g, b):
    mu = jnp.mean(y, axis=-1, keepdims=True)
    yc = y - mu
    var = jnp.mean(yc * yc, axis=-1, keepdims=True)
    return yc * lax.rsqrt(var + LN_EPS) * g + b


def _pack_pairs(lo, hi):
    lo_bits = lax.bitcast_convert_type(lo.astype(F32), I32)
    hi_bits = lax.bitcast_convert_type(hi.astype(F32), I32)
    return lax.shift_right_logical(lo_bits, 16) | (hi_bits & jnp.int32(-65536))


def _mid_kernel(do_ref, so_ref, x_ref, mod_ref, wo_ref, ln_ref, wpq_ref, x1_ref, h2_ref, q_ref, *, alpha):
    nd = do_ref.shape[-1]
    mixed = (jnp.dot(do_ref[...], wo_ref[:nd, :], preferred_element_type=F32)
             + jnp.dot(so_ref[...], wo_ref[nd:, :], preferred_element_type=F32))
    y = alpha * x_ref[...] + mod_ref[2:3, :] * mixed
    x1 = _layer_norm(y, ln_ref[0:1, :], ln_ref[1:2, :])
    x1_ref[...] = x1
    h2 = (x1 * (1.0 + mod_ref[4:5, :]) + mod_ref[3:4, :]).astype(BF16)
    half = h2.shape[-1] // 2
    h2_ref[...] = _pack_pairs(h2[:, :half], h2[:, half:])
    q_ref[...] = jnp.dot(h2, wpq_ref[...], preferred_element_type=F32).astype(BF16)


def _mid(diff_out, swa_out, x, mod3, wo_bf16, ln1, wpq_bf16, alpha, b0):
    bsz = diff_out.shape[0]
    _, s, d = x.shape
    nq = wpq_bf16.shape[1]
    tm = 512
    row = lambda b, i: (b, i, 0)
    const = lambda b, i: (0, 0)
    return pl.pallas_call(
        functools.partial(_mid_kernel, alpha=alpha),
        grid=(bsz, s // tm),
        in_specs=[
            pl.BlockSpec((None, tm, diff_out.shape[-1]), row),
            pl.BlockSpec((None, tm, swa_out.shape[-1]), row),
            pl.BlockSpec((None, tm, d), lambda b, i: (b0 + b, i, 0)),
            pl.BlockSpec((None, 6, d), lambda b, i: (b0 + b, 0, 0)),
            pl.BlockSpec(wo_bf16.shape, const),
            pl.BlockSpec((2, d), const),
            pl.BlockSpec(wpq_bf16.shape, const),
        ],
        out_specs=[
            pl.BlockSpec((None, tm, d), row),
            pl.BlockSpec((None, tm, d // 2), row),
            pl.BlockSpec((None, tm, nq), row),
        ],
        out_shape=[
            jax.ShapeDtypeStruct((bsz, s, d), F32),
            jax.ShapeDtypeStruct((bsz, s, d // 2), I32),
            jax.ShapeDtypeStruct((bsz, s, nq), BF16),
        ],
        compiler_params=pltpu.CompilerParams(
            dimension_semantics=("arbitrary", "arbitrary"), vmem_limit_bytes=VMEM_LIMIT),
        name="outproj_ln1_peerq",
    )(diff_out, swa_out, x, mod3, wo_bf16, ln1, wpq_bf16)


def _topk_rows(vals, pos, payload, k):
    out_v, out_p = [], []
    for _ in range(k):
        m = jnp.max(vals, axis=0, keepdims=True)
        first = jnp.min(jnp.where(vals == m, pos, 1e9), axis=0, keepdims=True)
        sel = pos == first
        if payload is None:
            out_p.append(first)
        else:
            out_p.append(jnp.max(jnp.where(sel, payload, -1.0), axis=0, keepdims=True))
        out_v.append(m)
        vals = jnp.where(sel, -jnp.inf, vals)
    return jnp.concatenate(out_v, axis=0), jnp.concatenate(out_p, axis=0)


def _candidates(v0, i0, v1, i1):
    k, lanes = v0.shape
    vals, poss, eids = [], [], []
    for a in range(4):
        nb = k if a == 0 else k // 2
        b_iota = lax.broadcasted_iota(I32, (nb, lanes), 0).astype(F32)
        vals.append(v0[a:a + 1, :] + v1[:nb, :])
        poss.append(a * k + b_iota)
        eids.append(i0[a:a + 1, :] * N_KEYS + i1[:nb, :])
    for b in range(3):
        na = k if b == 0 else k // 2
        a_iota = lax.broadcasted_iota(I32, (na, lanes), 0).astype(F32)
        vals.append(jnp.where(a_iota >= 4.0, v0[:na, :] + v1[b:b + 1, :], -jnp.inf))
        poss.append(a_iota * k + b)
        eids.append(i0[:na, :] * N_KEYS + i1[b:b + 1, :])
    return jnp.concatenate(vals, axis=0), jnp.concatenate(poss, axis=0), jnp.concatenate(eids, axis=0)


def _route_kernel(q_ref, keys_ref, idx_ref, gate_ref, idx_t, gate_t):
    tt = q_ref.shape[0]
    key_pos = lax.broadcasted_iota(I32, (N_KEYS, tt), 0).astype(F32)

    def head(h, carry):
        halves = []
        for p in range(2):
            qh = q_ref[:, pl.ds(pl.multiple_of((2 * h + p) * PEER_HALF, PEER_HALF), PEER_HALF)]
            sc = _nt_dot(keys_ref[h, p], qh)
            halves.append(_topk_rows(sc, key_pos, None, PEER_TOPK))
        (v0, i0), (v1, i1) = halves
        cv, cp, ce = _candidates(v0, i0, v1, i1)
        top_s, top_e = _topk_rows(cv, cp, ce, PEER_TOPK)
        e = jnp.exp(top_s - top_s[0:1, :])
        gate = e / jnp.sum(e, axis=0, keepdims=True)
        rows = pl.ds(pl.multiple_of(h * PEER_TOPK, PEER_TOPK), PEER_TOPK)
        idx_t[rows, :] = top_e
        gate_t[rows, :] = gate
        return carry

    lax.fori_loop(0, PEER_HEADS, head, 0)
    idx_ref[...] = idx_t[...].T.astype(I32)
    gate_ref[...] = gate_t[...].T


def _route(q2d, keys_bf16):
    t, nq = q2d.shape
    tt = LANES
    return pl.pallas_call(
        _route_kernel,
        grid=(t // tt,),
        in_specs=[
            pl.BlockSpec((tt, nq), lambda i: (i, 0)),
            pl.BlockSpec(keys_bf16.shape, lambda i: (0, 0, 0, 0)),
        ],
        out_specs=[
            pl.BlockSpec((tt, PEER_SLOTS), lambda i: (i, 0)),
            pl.BlockSpec((tt, PEER_SLOTS), lambda i: (i, 0)),
        ],
        out_shape=[
            jax.ShapeDtypeStruct((t, PEER_SLOTS), I32),
            jax.ShapeDtypeStruct((t, PEER_SLOTS), F32),
        ],
        scratch_shapes=[pltpu.VMEM((PEER_SLOTS, tt), F32), pltpu.VMEM((PEER_SLOTS, tt), F32)],
        compiler_params=pltpu.CompilerParams(dimension_semantics=("arbitrary",), vmem_limit_bytes=VMEM_LIMIT),
        name="peer_route",
    )(q2d, keys_bf16)


def _pack_table(tab):
    half = tab.shape[1] // 2
    bits = lax.bitcast_convert_type(tab.astype(BF16), jnp.uint16).astype(jnp.uint32)
    return lax.bitcast_convert_type(bits[:, :half] | (bits[:, half:] << 16), I32)


SC_LANES = 16
UDOT_ROWS = 64
UDOT_GROUP = 16


BF16_GROUP = 4


def _widen_pair_sum(acc, packed_bf16):
    bits = plsc.bitcast(packed_bf16, I32)
    lo = lax.bitcast_convert_type(bits << 16, F32)
    hi = lax.bitcast_convert_type(bits & jnp.int32(-65536), F32)
    return acc + lo + hi


def _udot_rows(rows_ref, h_ref, out_ref, out_base):
    nrows, width = rows_ref.shape
    span = BF16_GROUP * SC_LANES
    lane = lax.iota(I32, SC_LANES)

    @pl.loop(0, nrows // UDOT_GROUP)
    def _(g):
        r0 = g * UDOT_GROUP

        def chunk(c, accs):
            off = pl.multiple_of(c * span, span)
            hs = [plsc.bitcast(h_ref[pl.ds(off + i * SC_LANES, SC_LANES)], BF16) for i in range(BF16_GROUP)]
            out = []
            for r in range(UDOT_GROUP):
                part = None
                for i in range(BF16_GROUP):
                    w = plsc.bitcast(rows_ref[r0 + r, pl.ds(off + i * SC_LANES, SC_LANES)], BF16)
                    part = w * hs[i] if part is None else part + w * hs[i]
                out.append(_widen_pair_sum(accs[r], part))
            return tuple(out)

        zero = jnp.zeros((SC_LANES,), F32)
        accs = lax.fori_loop(0, width // span, chunk, (zero,) * UDOT_GROUP)
        vec = zero
        for r in range(UDOT_GROUP):
            vec = jnp.where(lane == r, jnp.sum(accs[r]), vec)
        out_ref[pl.ds(pl.multiple_of(out_base + r0, UDOT_GROUP), UDOT_GROUP)] = vec


def _sc_udot(table, idx, h):
    t, k = idx.shape
    width = table.shape[1]
    tpw = t // SC_WORKERS
    halves = k // UDOT_ROWS
    assert tpw * SC_WORKERS == t and halves == 2
    nbuf = tpw * halves
    mesh = plsc.VectorSubcoreMesh(core_axis_name="c", subcore_axis_name="s")

    def body(table_hbm, idx_hbm, h_hbm, out_hbm, idx_v, rows_v, h_v, a_v, gsem, hsem):
        wid = lax.axis_index("s") * SC_CORES + lax.axis_index("c")
        tok0 = wid * tpw
        pltpu.sync_copy(idx_hbm.at[wid], idx_v)

        def h_slot(slot):
            return h_v.at[pl.ds(slot * width, width)]

        def gather(j, slot):
            rows = idx_v.at[pl.ds(pl.multiple_of(j * UDOT_ROWS, UDOT_ROWS), UDOT_ROWS)]
            return pltpu.make_async_copy(table_hbm.at[rows], rows_v.at[slot], gsem.at[slot])

        def h_copy(tok, slot):
            return pltpu.make_async_copy(h_hbm.at[tok0 + tok], h_slot(slot), hsem.at[slot])

        gather(0, 0).start()
        h_copy(0, 0).start()

        @pl.loop(0, tpw, step=2)
        def _(t0):
            for hs in range(2):
                tok = t0 + hs
                h_copy(tok, hs).wait()

                @pl.when(tok + 1 < tpw)
                def _():
                    h_copy(tok + 1, 1 - hs).start()

                for slot in range(halves):
                    j = tok * halves + slot
                    gather(j, slot).wait()

                    @pl.when(j + 1 < nbuf)
                    def _():
                        gather(j + 1, 1 - slot).start()

                    _udot_rows(rows_v.at[slot], h_slot(hs), a_v, tok * k + slot * UDOT_ROWS)

        pltpu.sync_copy(a_v, out_hbm.at[pl.ds(pl.multiple_of(tok0 * k, 8), tpw * k)])

    out = pl.kernel(
        body,
        out_type=jax.ShapeDtypeStruct((t * k,), F32),
        mesh=mesh,
        scratch_types=[
            pltpu.VMEM((nbuf * UDOT_ROWS,), I32),
            pltpu.VMEM((2, UDOT_ROWS, width), table.dtype),
            pltpu.VMEM((2 * width,), I32),
            pltpu.VMEM((tpw * k,), F32),
            pltpu.SemaphoreType.DMA((2,)),
            pltpu.SemaphoreType.DMA((2,)),
        ],
        compiler_params=pltpu.CompilerParams(needs_layout_passes=False),
        name="peer_udot",
    )(table, idx.reshape(SC_WORKERS, nbuf * UDOT_ROWS), h)
    return out.reshape(t, k)


VSUM_CHUNKS = 8


def _vsum_rows(rows_ref, wgt_ref, wgt_base, out_ref, out_base, first):
    nrows, width = rows_ref.shape
    span = VSUM_CHUNKS * SC_LANES

    @pl.loop(0, width // span)
    def _(blk):
        col0 = pl.multiple_of(blk * span, span)

        def row_group(rg, accs):
            r0 = rg * BF16_GROUP
            parts = [None] * VSUM_CHUNKS
            for i in range(BF16_GROUP):
                splat = jnp.full((SC_LANES,), wgt_base + r0 + i, I32)
                wv = plsc.bitcast(plsc.load_gather(wgt_ref, [splat]), BF16)
                for c in range(VSUM_CHUNKS):
                    w = plsc.bitcast(rows_ref[r0 + i, pl.ds(col0 + c * SC_LANES, SC_LANES)], BF16)
                    parts[c] = w * wv if parts[c] is None else parts[c] + w * wv
            out = []
            for c in range(VSUM_CHUNKS):
                bits = plsc.bitcast(parts[c], I32)
                out.append(accs[2 * c] + lax.bitcast_convert_type(bits << 16, F32))
                out.append(accs[2 * c + 1] + lax.bitcast_convert_type(bits & jnp.int32(-65536), F32))
            return tuple(out)

        zero = jnp.zeros((SC_LANES,), F32)
        accs = lax.fori_loop(0, nrows // BF16_GROUP, row_group, (zero,) * (2 * VSUM_CHUNKS))
        for c in range(VSUM_CHUNKS):
            for half in range(2):
                dst = pl.ds(pl.multiple_of(out_base + half * width + col0 + c * SC_LANES, SC_LANES), SC_LANES)
                if first:
                    out_ref[dst] = accs[2 * c + half]
                else:
                    out_ref[dst] = out_ref[dst] + accs[2 * c + half]


def _sc_vsum(table, idx, wgt):
    t, k = idx.shape
    width = table.shape[1]
    d = 2 * width
    tpw = t // SC_WORKERS
    halves = k // UDOT_ROWS
    assert tpw * SC_WORKERS == t and halves == 2 and tpw % 2 == 0
    nbuf = tpw * halves
    mesh = plsc.VectorSubcoreMesh(core_axis_name="c", subcore_axis_name="s")

    def body(table_hbm, idx_hbm, wgt_hbm, out_hbm, idx_v, rows_v, wgt_v, out_v, gsem, osem):
        wid = lax.axis_index("s") * SC_CORES + lax.axis_index("c")
        tok0 = wid * tpw
        pltpu.sync_copy(idx_hbm.at[wid], idx_v)
        pltpu.sync_copy(wgt_hbm.at[wid], wgt_v)

        def gather(j, slot):
            rows = idx_v.at[pl.ds(pl.multiple_of(j * UDOT_ROWS, UDOT_ROWS), UDOT_ROWS)]
            return pltpu.make_async_copy(table_hbm.at[rows], rows_v.at[slot], gsem.at[slot])

        def put(tok, slot):
            return pltpu.make_async_copy(out_v.at[pl.ds(slot * d, d)], out_hbm.at[tok0 + tok], osem.at[slot])

        gather(0, 0).start()

        @pl.loop(0, tpw, step=2)
        def _(t0):
            for os_ in range(2):
                tok = t0 + os_

                @pl.when(tok >= 2)
                def _():
                    put(tok - 2, os_).wait()

                for slot in range(halves):
                    j = tok * halves + slot
                    gather(j, slot).wait()

                    @pl.when(j + 1 < nbuf)
                    def _():
                        gather(j + 1, 1 - slot).start()

                    _vsum_rows(rows_v.at[slot], wgt_v, tok * k + slot * UDOT_ROWS, out_v, os_ * d, slot == 0)
                put(tok, os_).start()

        put(tpw - 2, 0).wait()
        put(tpw - 1, 1).wait()

    return pl.kernel(
        body,
        out_type=jax.ShapeDtypeStruct((t, d), F32),
        mesh=mesh,
        scratch_types=[
            pltpu.VMEM((nbuf * UDOT_ROWS,), I32),
            pltpu.VMEM((2, UDOT_ROWS, width), table.dtype),
            pltpu.VMEM((tpw * k,), I32),
            pltpu.VMEM((2 * d,), F32),
            pltpu.SemaphoreType.DMA((2,)),
            pltpu.SemaphoreType.DMA((2,)),
        ],
        compiler_params=pltpu.CompilerParams(needs_layout_passes=False),
        name="peer_vsum",
    )(table, idx.reshape(SC_WORKERS, nbuf * UDOT_ROWS), wgt.reshape(SC_WORKERS, tpw * k))


def _wgt_kernel(a_ref, gate_ref, o_ref):
    a = a_ref[...]
    w = (gate_ref[...] * (0.5 * a * (1.0 + lax.erf(a * (2.0 ** -0.5))))).astype(BF16)
    o_ref[...] = _pack_pairs(w, w)


def _expert_weights(a, gate):
    t, k = a.shape
    tm = math.gcd(t, 1024)
    spec = pl.BlockSpec((tm, k), lambda i: (i, 0))
    return pl.pallas_call(
        _wgt_kernel,
        grid=(t // tm,),
        in_specs=[spec, spec],
        out_specs=spec,
        out_shape=jax.ShapeDtypeStruct((t, k), I32),
        compiler_params=pltpu.CompilerParams(dimension_semantics=("arbitrary",), vmem_limit_bytes=VMEM_LIMIT),
        name="peer_weights",
    )(a, gate)


def _final_kernel(x1_ref, ffn_ref, mod_ref, ln_ref, o_ref, *, alpha):
    y = alpha * x1_ref[...] + mod_ref[5:6, :] * ffn_ref[...]
    o_ref[...] = _layer_norm(y, ln_ref[0:1, :], ln_ref[1:2, :])


def _final(x1, ffn, mod3, ln2, alpha, b0):
    nb, s, d = x1.shape
    tm = 512
    row = lambda b, i: (b, i, 0)
    return pl.pallas_call(
        functools.partial(_final_kernel, alpha=alpha),
        grid=(nb, s // tm),
        in_specs=[
            pl.BlockSpec((None, tm, d), row),
            pl.BlockSpec((None, tm, d), row),
            pl.BlockSpec((None, 6, d), lambda b, i: (b0 + b, 0, 0)),
            pl.BlockSpec((2, d), lambda b, i: (0, 0)),
        ],
        out_specs=pl.BlockSpec((None, tm, d), row),
        out_shape=jax.ShapeDtypeStruct((nb, s, d), F32),
        compiler_params=pltpu.CompilerParams(
            dimension_semantics=("arbitrary", "arbitrary"), vmem_limit_bytes=VMEM_LIMIT),
        name="deepnorm_ln2",
    )(x1, ffn, mod3, ln2)


CHUNK_BATCHES = 1
CHUNK_LAG = 3


def _layer_chunk(x, b0, nb, mod3, slopes, lam_vecs, lambda_init, alpha, w):
    _, seq, d = x.shape
    proj = _inproj(x, mod3, w["w_in"], b0, nb)
    diff_out = _diff_attention(proj, slopes, lam_vecs, w["subln_g"], lambda_init)
    swa_out = _swa_attention(proj, slopes, w["sinks"])
    x1, h2, q = _mid(diff_out, swa_out, x, mod3, w["w_out"], w["ln1"], w["w_pq"], alpha, b0)
    idx, gate = _route(q.reshape(nb * seq, -1), w["sub_keys"])
    a = _sc_udot(w["u_pack"], idx, h2.reshape(nb * seq, d // 2))
    ffn = _sc_vsum(w["v_pack"], idx, _expert_weights(a, gate))
    return _final(x1, ffn.reshape(nb, seq, d), mod3, w["ln2"], alpha, b0)


def kernel(x, c, w_ada, b_ada, w_in, lambda_q1, lambda_k1, lambda_q2, lambda_k2, subln_g, sinks, w_out, ln1_g, ln1_b, w_pq, sub_keys, u_tab, v_tab, ln2_g, ln2_b):
    bsz, seq, d = x.shape
    depth = w_ada.shape[0]
    alpha = (2 * depth) ** 0.25
    slopes = jnp.exp2(-8.0 * jnp.arange(1, N_ATT_HEADS + 1, dtype=F32) / N_ATT_HEADS)
    nb = CHUNK_BATCHES
    for l in range(depth):
        lambda_init = 0.8 - 0.6 * math.exp(-0.3 * l)
        mod3 = _mod(c, w_ada[l], b_ada[l]).reshape(bsz, 6, d)
        lam_vecs = jnp.stack([lambda_q1[l], lambda_k1[l], lambda_q2[l], lambda_k2[l]])
        w = dict(w_in=w_in[l].astype(BF16), subln_g=subln_g[l], sinks=sinks[l], w_out=w_out[l].astype(BF16),
                 ln1=jnp.stack([ln1_g[l], ln1_b[l]]), w_pq=w_pq[l].astype(BF16),
                 sub_keys=sub_keys[l].astype(BF16), u_pack=_pack_table(u_tab[l]), v_pack=_pack_table(v_tab[l]),
                 ln2=jnp.stack([ln2_g[l], ln2_b[l]]))
        outs = []
        for ci, b0 in enumerate(range(0, bsz, nb)):
            mod_c = mod3
            if ci >= CHUNK_LAG:
                mod_c, outs[ci - CHUNK_LAG] = lax.optimization_barrier((mod3, outs[ci - CHUNK_LAG]))
            outs.append(_layer_chunk(x, b0, nb, mod_c, slopes, lam_vecs, lambda_init, alpha, w))
        x = jnp.concatenate(outs, axis=0).reshape(bsz, seq, d)
    return x
```

```python
import functools
import math

import jax
import jax.numpy as jnp
from jax import lax
from jax.experimental import pallas as pl
from jax.experimental.pallas import tpu as pltpu
from jax.experimental.pallas import tpu_sc as plsc

F32 = jnp.float32
BF16 = jnp.bfloat16
I32 = jnp.int32

HEAD_DIM = 64
DIFF_HEADS = 4
DIFF_V = 2 * HEAD_DIM
SWA_Q_HEADS = 8
SWA_KV_HEADS = 2
SWA_GROUP = SWA_Q_HEADS // SWA_KV_HEADS
WINDOW = 128
N_ATT_HEADS = SWA_Q_HEADS + DIFF_HEADS
PEER_HEADS = 8
N_KEYS = 128
PEER_HALF = 128
PEER_TOPK = 16
PEER_SLOTS = PEER_HEADS * PEER_TOPK
LN_EPS = 1e-5
NEG_INF = -1e30

LANES = 128
VMEM_LIMIT = 48 * 1024 * 1024

SC_CORES = 2
SC_SUBCORES = 16
SC_WORKERS = SC_CORES * SC_SUBCORES


def _nt_dot(a, b):
    return lax.dot_general(a, b, (((1,), (1,)), ((), ())), preferred_element_type=F32)


def _mod_kernel(c_ref, w_ref, b_ref, o_ref):
    c = c_ref[...]
    s = c * (1.0 / (1.0 + jnp.exp(-c)))
    o_ref[...] = jnp.dot(s.astype(BF16), w_ref[...].astype(BF16), preferred_element_type=F32) + b_ref[...]


def _mod(c, w, b):
    bsz, d = c.shape
    n = w.shape[1]
    tn = 768
    return pl.pallas_call(
        _mod_kernel,
        grid=(n // tn,),
        in_specs=[
            pl.BlockSpec((bsz, d), lambda j: (0, 0)),
            pl.BlockSpec((d, tn), lambda j: (0, j)),
            pl.BlockSpec((1, tn), lambda j: (0, j)),
        ],
        out_specs=pl.BlockSpec((bsz, tn), lambda j: (0, j)),
        out_shape=jax.ShapeDtypeStruct((bsz, n), F32),
        compiler_params=pltpu.CompilerParams(dimension_semantics=("arbitrary",), vmem_limit_bytes=VMEM_LIMIT),
        name="adaln_mod",
    )(c, w, b.reshape(1, n))


def _inproj_kernel(x_ref, mod_ref, w_ref, o_ref):
    h = x_ref[...] * (1.0 + mod_ref[1:2, :]) + mod_ref[0:1, :]
    o_ref[...] = jnp.dot(h.astype(BF16), w_ref[...], preferred_element_type=F32).astype(BF16)


def _inproj(x, mod3, w_bf16, b0, bsz):
    _, s, d = x.shape
    n = w_bf16.shape[1]
    tm = 512
    return pl.pallas_call(
        _inproj_kernel,
        grid=(bsz, s // tm),
        in_specs=[
            pl.BlockSpec((None, tm, d), lambda b, i: (b0 + b, i, 0)),
            pl.BlockSpec((None, 6, d), lambda b, i: (b0 + b, 0, 0)),
            pl.BlockSpec((d, n), lambda b, i: (0, 0)),
        ],
        out_specs=pl.BlockSpec((None, tm, n), lambda b, i: (b, i, 0)),
        out_shape=jax.ShapeDtypeStruct((bsz, s, n), BF16),
        compiler_params=pltpu.CompilerParams(
            dimension_semantics=("arbitrary", "arbitrary"), vmem_limit_bytes=VMEM_LIMIT),
        name="in_proj",
    )(x, mod3, w_bf16)


def _diff_kernel(slopes_ref, q_ref, k_ref, v_ref, lam_ref, g_ref, o_ref, *, tq, lambda_init):
    h = pl.program_id(1)
    i = pl.program_id(2)
    slope = slopes_ref[SWA_Q_HEADS + h]
    scale = HEAD_DIM ** -0.5
    q = q_ref[...]
    qs = (q[:, :HEAD_DIM], q[:, HEAD_DIM:])
    row = (i * tq + lax.broadcasted_iota(I32, (tq, 1), 0)).astype(F32)

    def body(j, carry):
        ks = k_ref[pl.ds(pl.multiple_of(j * tq, tq), tq), :]
        vs = v_ref[pl.ds(pl.multiple_of(j * tq, tq), tq), :]
        col = (j * tq + lax.broadcasted_iota(I32, (1, tq), 1)).astype(F32)
        dist = row - col
        valid = dist >= 0.0
        bias = -slope * dist
        new = []
        for m in range(2):
            mx, l, acc = carry[3 * m: 3 * m + 3]
            s = _nt_dot(qs[m], ks[:, m * HEAD_DIM:(m + 1) * HEAD_DIM]) * scale
            s = jnp.where(valid, s + bias, NEG_INF)
            mx_new = jnp.maximum(mx, jnp.max(s, axis=-1, keepdims=True))
            p = jnp.exp(s - mx_new)
            corr = jnp.exp(mx - mx_new)
            l = l * corr + jnp.sum(p, axis=-1, keepdims=True)
            acc = acc * corr + jnp.dot(p.astype(BF16), vs, preferred_element_type=F32)
            new += [mx_new, l, acc]
        return tuple(new)

    init = []
    for _ in range(2):
        init += [jnp.full((tq, 1), NEG_INF, F32), jnp.zeros((tq, 1), F32), jnp.zeros((tq, DIFF_V), F32)]
    m0, l0, a0, m1, l1, a1 = lax.fori_loop(0, i + 1, body, tuple(init))

    lam_v = lam_ref[...]
    lam = (jnp.exp(jnp.sum(lam_v[0:1, :] * lam_v[1:2, :], axis=-1, keepdims=True))
           - jnp.exp(jnp.sum(lam_v[2:3, :] * lam_v[3:4, :], axis=-1, keepdims=True)) + lambda_init)
    o = a0 / l0 - lam * (a1 / l1)
    o = o * lax.rsqrt(jnp.mean(o * o, axis=-1, keepdims=True) + LN_EPS)
    o_ref[...] = (o * g_ref[...] * (1.0 - lambda_init)).astype(BF16)


def _diff_attention(proj, slopes, lam_vecs, subln_g, lambda_init):
    bsz, s, _ = proj.shape
    tq = 256
    kcol = DIFF_HEADS
    vcol = 2 * DIFF_HEADS
    return pl.pallas_call(
        functools.partial(_diff_kernel, tq=tq, lambda_init=lambda_init),
        grid=(bsz, DIFF_HEADS, s // tq),
        in_specs=[
            pl.BlockSpec(memory_space=pltpu.SMEM),
            pl.BlockSpec((None, tq, DIFF_V), lambda b, h, i: (b, i, h)),
            pl.BlockSpec((None, s, DIFF_V), lambda b, h, i: (b, 0, kcol + h)),
            pl.BlockSpec((None, s, DIFF_V), lambda b, h, i: (b, 0, vcol + h)),
            pl.BlockSpec((4, HEAD_DIM), lambda b, h, i: (0, 0)),
            pl.BlockSpec((1, DIFF_V), lambda b, h, i: (0, 0)),
        ],
        out_specs=pl.BlockSpec((None, tq, DIFF_V), lambda b, h, i: (b, i, h)),
        out_shape=jax.ShapeDtypeStruct((bsz, s, DIFF_HEADS * DIFF_V), BF16),
        compiler_params=pltpu.CompilerParams(
            dimension_semantics=("arbitrary", "arbitrary", "arbitrary"), vmem_limit_bytes=VMEM_LIMIT),
        name="diff_attention",
    )(slopes, proj, proj, proj, lam_vecs, subln_g.reshape(1, DIFF_V))


def _swa_kernel(slopes_ref, sinks_ref, q_ref, k_ref, v_ref, o_ref, *, tq):
    i = pl.program_id(1)
    scale = HEAD_DIM ** -0.5
    blk = WINDOW
    ii = lax.broadcasted_iota(I32, (blk, 2 * blk), 0)
    jj = lax.broadcasted_iota(I32, (blk, 2 * blk), 1)
    for r in range(tq // blk):
        start = i * tq + r * blk
        kstart = jnp.maximum(start - blk, 0)
        kb = k_ref[pl.ds(pl.multiple_of(kstart, blk), 2 * blk), :]
        vb = v_ref[pl.ds(pl.multiple_of(kstart, blk), 2 * blk), :]
        dist = (start + ii) - (kstart + jj)
        valid = (dist >= 0) & (dist < WINDOW)
        distf = dist.astype(F32)
        outs = []
        for kvh in range(SWA_KV_HEADS):
            k = kb[:, kvh * HEAD_DIM:(kvh + 1) * HEAD_DIM]
            v = vb[:, kvh * HEAD_DIM:(kvh + 1) * HEAD_DIM]
            for g in range(SWA_GROUP):
                hq = kvh * SWA_GROUP + g
                qh = q_ref[r * blk:(r + 1) * blk, hq * HEAD_DIM:(hq + 1) * HEAD_DIM]
                s = _nt_dot(qh, k) * scale - slopes_ref[hq] * distf
                s = jnp.where(valid, s, NEG_INF)
                sink = sinks_ref[hq]
                m = jnp.maximum(jnp.max(s, axis=-1, keepdims=True), sink)
                p = jnp.exp(s - m)
                denom = jnp.sum(p, axis=-1, keepdims=True) + jnp.exp(sink - m)
                outs.append(jnp.dot(p.astype(BF16), v, preferred_element_type=F32) / denom)
        o_ref[r * blk:(r + 1) * blk, :] = jnp.concatenate(outs, axis=-1).astype(BF16)


def _swa_attention(proj, slopes, sinks):
    bsz, s, _ = proj.shape
    tq = 256
    width = SWA_Q_HEADS * HEAD_DIM
    qcol = (3 * DIFF_HEADS * DIFF_V) // width
    kcol = (3 * DIFF_HEADS * DIFF_V + width) // LANES
    return pl.pallas_call(
        functools.partial(_swa_kernel, tq=tq),
        grid=(bsz, s // tq),
        in_specs=[
            pl.BlockSpec(memory_space=pltpu.SMEM),
            pl.BlockSpec(memory_space=pltpu.SMEM),
            pl.BlockSpec((None, tq, width), lambda b, i: (b, i, qcol)),
            pl.BlockSpec((None, s, LANES), lambda b, i: (b, 0, kcol)),
            pl.BlockSpec((None, s, LANES), lambda b, i: (b, 0, kcol + 1)),
        ],
        out_specs=pl.BlockSpec((None, tq, width), lambda b, i: (b, i, 0)),
        out_shape=jax.ShapeDtypeStruct((bsz, s, width), BF16),
        compiler_params=pltpu.CompilerParams(
            dimension_semantics=("arbitrary", "arbitrary"), vmem_limit_bytes=VMEM_LIMIT),
        name="swa_attention",
    )(slopes, sinks, proj, proj, proj)


def _layer_norm(y, g, b):
    mu = jnp.mean(y, axis=-1, keepdims=True)
    yc = y - mu
    var = jnp.mean(yc * yc, axis=-1, keepdims=True)
    return yc * lax.rsqrt(var + LN_EPS) * g + b


def _pack_pairs(lo, hi):
    lo_bits = lax.bitcast_convert_type(lo.astype(F32), I32)
    hi_bits = lax.bitcast_convert_type(hi.astype(F32), I32)
    return lax.shift_right_logical(lo_bits, 16) | (hi_bits & jnp.int32(-65536))


def _mid_kernel(do_ref, so_ref, x_ref, mod_ref, wo_ref, ln_ref, wpq_ref, x1_ref, h2_ref, q_ref, *, alpha):
    nd = do_ref.shape[-1]
    mixed = (jnp.dot(do_ref[...], wo_ref[:nd, :], preferred_element_type=F32)
             + jnp.dot(so_ref[...], wo_ref[nd:, :], preferred_element_type=F32))
    y = alpha * x_ref[...] + mod_ref[2:3, :] * mixed
    x1 = _layer_norm(y, ln_ref[0:1, :], ln_ref[1:2, :])
    x1_ref[...] = x1
    h2 = (x1 * (1.0 + mod_ref[4:5, :]) + mod_ref[3:4, :]).astype(BF16)
    half = h2.shape[-1] // 2
    h2_ref[...] = _pack_pairs(h2[:, :half], h2[:, half:])
    q_ref[...] = jnp.dot(h2, wpq_ref[...], preferred_element_type=F32).astype(BF16)


def _mid(diff_out, swa_out, x, mod3, wo_bf16, ln1, wpq_bf16, alpha, b0):
    bsz = diff_out.shape[0]
    _, s, d = x.shape
    nq = wpq_bf16.shape[1]
    tm = 512
    row = lambda b, i: (b, i, 0)
    const = lambda b, i: (0, 0)
    return pl.pallas_call(
        functools.partial(_mid_kernel, alpha=alpha),
        grid=(bsz, s // tm),
        in_specs=[
            pl.BlockSpec((None, tm, diff_out.shape[-1]), row),
            pl.BlockSpec((None, tm, swa_out.shape[-1]), row),
            pl.BlockSpec((None, tm, d), lambda b, i: (b0 + b, i, 0)),
            pl.BlockSpec((None, 6, d), lambda b, i: (b0 + b, 0, 0)),
            pl.BlockSpec(wo_bf16.shape, const),
            pl.BlockSpec((2, d), const),
            pl.BlockSpec(wpq_bf16.shape, const),
        ],
        out_specs=[
            pl.BlockSpec((None, tm, d), row),
            pl.BlockSpec((None, tm, d // 2), row),
            pl.BlockSpec((None, tm, nq), row),
        ],
        out_shape=[
            jax.ShapeDtypeStruct((bsz, s, d), F32),
            jax.ShapeDtypeStruct((bsz, s, d // 2), I32),
            jax.ShapeDtypeStruct((bsz, s, nq), BF16),
        ],
        compiler_params=pltpu.CompilerParams(
            dimension_semantics=("arbitrary", "arbitrary"), vmem_limit_bytes=VMEM_LIMIT),
        name="outproj_ln1_peerq",
    )(diff_out, swa_out, x, mod3, wo_bf16, ln1, wpq_bf16)


def _topk_rows(vals, pos, payload, k):
    out_v, out_p = [], []
    for _ in range(k):
        m = jnp.max(vals, axis=0, keepdims=True)
        first = jnp.min(jnp.where(vals == m, pos, 1e9), axis=0, keepdims=True)
        sel = pos == first
        if payload is None:
            out_p.append(first)
        else:
            out_p.append(jnp.max(jnp.where(sel, payload, -1.0), axis=0, keepdims=True))
        out_v.append(m)
        vals = jnp.where(sel, -jnp.inf, vals)
    return jnp.concatenate(out_v, axis=0), jnp.concatenate(out_p, axis=0)


def _candidates(v0, i0, v1, i1):
    k, lanes = v0.shape
    vals, poss, eids = [], [], []
    for a in range(4):
        nb = k if a == 0 else k // 2
        b_iota = lax.broadcasted_iota(I32, (nb, lanes), 0).astype(F32)
        vals.append(v0[a:a + 1, :] + v1[:nb, :])
        poss.append(a * k + b_iota)
        eids.append(i0[a:a + 1, :] * N_KEYS + i1[:nb, :])
    for b in range(3):
        na = k if b == 0 else k // 2
        a_iota = lax.broadcasted_iota(I32, (na, lanes), 0).astype(F32)
        vals.append(jnp.where(a_iota >= 4.0, v0[:na, :] + v1[b:b + 1, :], -jnp.inf))
        poss.append(a_iota * k + b)
        eids.append(i0[:na, :] * N_KEYS + i1[b:b + 1, :])
    return jnp.concatenate(vals, axis=0), jnp.concatenate(poss, axis=0), jnp.concatenate(eids, axis=0)


def _route_kernel(q_ref, keys_ref, idx_ref, gate_ref, idx_t, gate_t):
    tt = q_ref.shape[0]
    key_pos = lax.broadcasted_iota(I32, (N_KEYS, tt), 0).astype(F32)

    def head(h, carry):
        halves = []
        for p in range(2):
            qh = q_ref[:, pl.ds(pl.multiple_of((2 * h + p) * PEER_HALF, PEER_HALF), PEER_HALF)]
            sc = _nt_dot(keys_ref[h, p], qh)
            halves.append(_topk_rows(sc, key_pos, None, PEER_TOPK))
        (v0, i0), (v1, i1) = halves
        cv, cp, ce = _candidates(v0, i0, v1, i1)
        top_s, top_e = _topk_rows(cv, cp, ce, PEER_TOPK)
        e = jnp.exp(top_s - top_s[0:1, :])
        gate = e / jnp.sum(e, axis=0, keepdims=True)
        rows = pl.ds(pl.multiple_of(h * PEER_TOPK, PEER_TOPK), PEER_TOPK)
        idx_t[rows, :] = top_e
        gate_t[rows, :] = gate
        return carry

    lax.fori_loop(0, PEER_HEADS, head, 0)
    idx_ref[...] = idx_t[...].T.astype(I32)
    gate_ref[...] = gate_t[...].T


def _route(q2d, keys_bf16):
    t, nq = q2d.shape
    tt = 2 * LANES
    return pl.pallas_call(
        _route_kernel,
        grid=(t // tt,),
        in_specs=[
            pl.BlockSpec((tt, nq), lambda i: (i, 0)),
            pl.BlockSpec(keys_bf16.shape, lambda i: (0, 0, 0, 0)),
        ],
        out_specs=[
            pl.BlockSpec((tt, PEER_SLOTS), lambda i: (i, 0)),
            pl.BlockSpec((tt, PEER_SLOTS), lambda i: (i, 0)),
        ],
        out_shape=[
            jax.ShapeDtypeStruct((t, PEER_SLOTS), I32),
            jax.ShapeDtypeStruct((t, PEER_SLOTS), F32),
        ],
        scratch_shapes=[pltpu.VMEM((PEER_SLOTS, tt), F32), pltpu.VMEM((PEER_SLOTS, tt), F32)],
        compiler_params=pltpu.CompilerParams(dimension_semantics=("arbitrary",), vmem_limit_bytes=VMEM_LIMIT),
        name="peer_route",
    )(q2d, keys_bf16)


def _pack_table(tab):
    half = tab.shape[1] // 2
    bits = lax.bitcast_convert_type(tab.astype(BF16), jnp.uint16).astype(jnp.uint32)
    return lax.bitcast_convert_type(bits[:, :half] | (bits[:, half:] << 16), I32)


SC_LANES = 16
RING_ROWS = 32
RING_AHEAD = 3
UDOT_GROUP = 16


BF16_GROUP = 4


def _widen_pair_sum(acc, packed_bf16):
    bits = plsc.bitcast(packed_bf16, I32)
    lo = lax.bitcast_convert_type(bits << 16, F32)
    hi = lax.bitcast_convert_type(bits & jnp.int32(-65536), F32)
    return acc + lo + hi


def _udot_rows(rows_ref, h_ref, out_ref, out_base):
    nrows, width = rows_ref.shape
    span = BF16_GROUP * SC_LANES
    lane = lax.iota(I32, SC_LANES)

    @pl.loop(0, nrows // UDOT_GROUP)
    def _(g):
        r0 = g * UDOT_GROUP

        def chunk(c, accs):
            off = pl.multiple_of(c * span, span)
            hs = [plsc.bitcast(h_ref[pl.ds(off + i * SC_LANES, SC_LANES)], BF16) for i in range(BF16_GROUP)]
            out = []
            for r in range(UDOT_GROUP):
                part = None
                for i in range(BF16_GROUP):
                    w = plsc.bitcast(rows_ref[r0 + r, pl.ds(off + i * SC_LANES, SC_LANES)], BF16)
                    part = w * hs[i] if part is None else part + w * hs[i]
                out.append(_widen_pair_sum(accs[r], part))
            return tuple(out)

        zero = jnp.zeros((SC_LANES,), F32)
        accs = lax.fori_loop(0, width // span, chunk, (zero,) * UDOT_GROUP)
        vec = zero
        for r in range(UDOT_GROUP):
            vec = jnp.where(lane == r, jnp.sum(accs[r]), vec)
        out_ref[pl.ds(pl.multiple_of(out_base + r0, UDOT_GROUP), UDOT_GROUP)] = vec


def _sc_udot(table, idx, h):
    t, k = idx.shape
    width = table.shape[1]
    tpw = t // SC_WORKERS
    slots = k // RING_ROWS
    assert tpw * SC_WORKERS == t and tpw % 2 == 0 and slots == RING_AHEAD + 1
    nbuf = tpw * slots
    mesh = plsc.VectorSubcoreMesh(core_axis_name="c", subcore_axis_name="s")

    def body(table_hbm, idx_hbm, h_hbm, out_hbm, idx_v, rows_v, h_v, a_v, gsem, hsem):
        wid = lax.axis_index("s") * SC_CORES + lax.axis_index("c")
        tok0 = wid * tpw
        pltpu.sync_copy(idx_hbm.at[wid], idx_v)

        def h_slot(slot):
            return h_v.at[pl.ds(slot * width, width)]

        def gather(j, slot):
            rows = idx_v.at[pl.ds(pl.multiple_of(j * RING_ROWS, RING_ROWS), RING_ROWS)]
            return pltpu.make_async_copy(table_hbm.at[rows], rows_v.at[slot], gsem.at[slot])

        def h_copy(tok, slot):
            return pltpu.make_async_copy(h_hbm.at[tok0 + tok], h_slot(slot), hsem.at[slot])

        for j in range(RING_AHEAD):
            gather(j, j).start()
        h_copy(0, 0).start()

        @pl.loop(0, tpw, step=2)
        def _(t0):
            for hs in range(2):
                tok = t0 + hs
                h_copy(tok, hs).wait()

                @pl.when(tok + 1 < tpw)
                def _():
                    h_copy(tok + 1, 1 - hs).start()

                for slot in range(slots):
                    j = tok * slots + slot
                    gather(j, slot).wait()

                    @pl.when(j + RING_AHEAD < nbuf)
                    def _():
                        gather(j + RING_AHEAD, (slot + RING_AHEAD) % slots).start()

                    _udot_rows(rows_v.at[slot], h_slot(hs), a_v, tok * k + slot * RING_ROWS)

        pltpu.sync_copy(a_v, out_hbm.at[pl.ds(pl.multiple_of(tok0 * k, 8), tpw * k)])

    out = pl.kernel(
        body,
        out_type=jax.ShapeDtypeStruct((t * k,), F32),
        mesh=mesh,
        scratch_types=[
            pltpu.VMEM((nbuf * RING_ROWS,), I32),
            pltpu.VMEM((slots, RING_ROWS, width), table.dtype),
            pltpu.VMEM((2 * width,), I32),
            pltpu.VMEM((tpw * k,), F32),
            pltpu.SemaphoreType.DMA((slots,)),
            pltpu.SemaphoreType.DMA((2,)),
        ],
        compiler_params=pltpu.CompilerParams(needs_layout_passes=False),
        name="peer_udot",
    )(table, idx.reshape(SC_WORKERS, nbuf * RING_ROWS), h)
    return out.reshape(t, k)


VSUM_CHUNKS = 8


def _vsum_rows(rows_ref, wgt_ref, wgt_base, out_ref, out_base, first):
    nrows, width = rows_ref.shape
    span = VSUM_CHUNKS * SC_LANES

    @pl.loop(0, width // span)
    def _(blk):
        col0 = pl.multiple_of(blk * span, span)

        def row_group(rg, accs):
            r0 = rg * BF16_GROUP
            parts = [None] * VSUM_CHUNKS
            for i in range(BF16_GROUP):
                splat = jnp.full((SC_LANES,), wgt_base + r0 + i, I32)
                wv = plsc.bitcast(plsc.load_gather(wgt_ref, [splat]), BF16)
                for c in range(VSUM_CHUNKS):
                    w = plsc.bitcast(rows_ref[r0 + i, pl.ds(col0 + c * SC_LANES, SC_LANES)], BF16)
                    parts[c] = w * wv if parts[c] is None else parts[c] + w * wv
            out = []
            for c in range(VSUM_CHUNKS):
                bits = plsc.bitcast(parts[c], I32)
                out.append(accs[2 * c] + lax.bitcast_convert_type(bits << 16, F32))
                out.append(accs[2 * c + 1] + lax.bitcast_convert_type(bits & jnp.int32(-65536), F32))
            return tuple(out)

        zero = jnp.zeros((SC_LANES,), F32)
        accs = lax.fori_loop(0, nrows // BF16_GROUP, row_group, (zero,) * (2 * VSUM_CHUNKS))
        for c in range(VSUM_CHUNKS):
            for half in range(2):
                dst = pl.ds(pl.multiple_of(out_base + half * width + col0 + c * SC_LANES, SC_LANES), SC_LANES)
                if first:
                    out_ref[dst] = accs[2 * c + half]
                else:
                    out_ref[dst] = out_ref[dst] + accs[2 * c + half]


def _sc_vsum(table, idx, wgt):
    t, k = idx.shape
    width = table.shape[1]
    d = 2 * width
    tpw = t // SC_WORKERS
    slots = k // RING_ROWS
    assert tpw * SC_WORKERS == t and tpw % 2 == 0 and slots == RING_AHEAD + 1
    nbuf = tpw * slots
    mesh = plsc.VectorSubcoreMesh(core_axis_name="c", subcore_axis_name="s")

    def body(table_hbm, idx_hbm, wgt_hbm, out_hbm, idx_v, rows_v, wgt_v, out_v, gsem, osem):
        wid = lax.axis_index("s") * SC_CORES + lax.axis_index("c")
        tok0 = wid * tpw
        pltpu.sync_copy(idx_hbm.at[wid], idx_v)
        pltpu.sync_copy(wgt_hbm.at[wid], wgt_v)

        def gather(j, slot):
            rows = idx_v.at[pl.ds(pl.multiple_of(j * RING_ROWS, RING_ROWS), RING_ROWS)]
            return pltpu.make_async_copy(table_hbm.at[rows], rows_v.at[slot], gsem.at[slot])

        def put(tok, slot):
            return pltpu.make_async_copy(out_v.at[pl.ds(slot * d, d)], out_hbm.at[tok0 + tok], osem.at[slot])

        for j in range(RING_AHEAD):
            gather(j, j).start()

        @pl.loop(0, tpw, step=2)
        def _(t0):
            for os_ in range(2):
                tok = t0 + os_

                @pl.when(tok >= 2)
                def _():
                    put(tok - 2, os_).wait()

                for slot in range(slots):
                    j = tok * slots + slot
                    gather(j, slot).wait()

                    @pl.when(j + RING_AHEAD < nbuf)
                    def _():
                        gather(j + RING_AHEAD, (slot + RING_AHEAD) % slots).start()

                    _vsum_rows(rows_v.at[slot], wgt_v, tok * k + slot * RING_ROWS, out_v, os_ * d, slot == 0)
                put(tok, os_).start()

        put(tpw - 2, 0).wait()
        put(tpw - 1, 1).wait()

    return pl.kernel(
        body,
        out_type=jax.ShapeDtypeStruct((t, d), F32),
        mesh=mesh,
        scratch_types=[
            pltpu.VMEM((nbuf * RING_ROWS,), I32),
            pltpu.VMEM((slots, RING_ROWS, width), table.dtype),
            pltpu.VMEM((tpw * k,), I32),
            pltpu.VMEM((2 * d,), F32),
            pltpu.SemaphoreType.DMA((slots,)),
            pltpu.SemaphoreType.DMA((2,)),
        ],
        compiler_params=pltpu.CompilerParams(needs_layout_passes=False),
        name="peer_vsum",
    )(table, idx.reshape(SC_WORKERS, nbuf * RING_ROWS), wgt.reshape(SC_WORKERS, tpw * k))


def _wgt_kernel(a_ref, gate_ref, o_ref):
    a = a_ref[...]
    w = (gate_ref[...] * (0.5 * a * (1.0 + lax.erf(a * (2.0 ** -0.5))))).astype(BF16)
    o_ref[...] = _pack_pairs(w, w)


def _expert_weights(a, gate):
    t, k = a.shape
    tm = math.gcd(t, 1024)
    spec = pl.BlockSpec((tm, k), lambda i: (i, 0))
    return pl.pallas_call(
        _wgt_kernel,
        grid=(t // tm,),
        in_specs=[spec, spec],
        out_specs=spec,
        out_shape=jax.ShapeDtypeStruct((t, k), I32),
        compiler_params=pltpu.CompilerParams(dimension_semantics=("arbitrary",), vmem_limit_bytes=VMEM_LIMIT),
        name="peer_weights",
    )(a, gate)


def _final_kernel(x1_ref, ffn_ref, mod_ref, ln_ref, o_ref, *, alpha):
    y = alpha * x1_ref[...] + mod_ref[5:6, :] * ffn_ref[...]
    o_ref[...] = _layer_norm(y, ln_ref[0:1, :], ln_ref[1:2, :])


def _final(x1, ffn, mod3, ln2, alpha, b0):
    nb, s, d = x1.shape
    tm = 512
    row = lambda b, i: (b, i, 0)
    return pl.pallas_call(
        functools.partial(_final_kernel, alpha=alpha),
        grid=(nb, s // tm),
        in_specs=[
            pl.BlockSpec((None, tm, d), row),
            pl.BlockSpec((None, tm, d), row),
            pl.BlockSpec((None, 6, d), lambda b, i: (b0 + b, 0, 0)),
            pl.BlockSpec((2, d), lambda b, i: (0, 0)),
        ],
        out_specs=pl.BlockSpec((None, tm, d), row),
        out_shape=jax.ShapeDtypeStruct((nb, s, d), F32),
        compiler_params=pltpu.CompilerParams(
            dimension_semantics=("arbitrary", "arbitrary"), vmem_limit_bytes=VMEM_LIMIT),
        name="deepnorm_ln2",
    )(x1, ffn, mod3, ln2)


CHUNK_BATCHES = 1
CHUNK_LAG = 3


def _layer_chunk(x, b0, nb, mod3, slopes, lam_vecs, lambda_init, alpha, w):
    _, seq, d = x.shape
    proj = _inproj(x, mod3, w["w_in"], b0, nb)
    diff_out = _diff_attention(proj, slopes, lam_vecs, w["subln_g"], lambda_init)
    swa_out = _swa_attention(proj, slopes, w["sinks"])
    x1, h2, q = _mid(diff_out, swa_out, x, mod3, w["w_out"], w["ln1"], w["w_pq"], alpha, b0)
    idx, gate = _route(q.reshape(nb * seq, -1), w["sub_keys"])
    a = _sc_udot(w["u_pack"], idx, h2.reshape(nb * seq, d // 2))
    ffn = _sc_vsum(w["v_pack"], idx, _expert_weights(a, gate))
    return _final(x1, ffn.reshape(nb, seq, d), mod3, w["ln2"], alpha, b0)


def kernel(x, c, w_ada, b_ada, w_in, lambda_q1, lambda_k1, lambda_q2, lambda_k2, subln_g, sinks, w_out, ln1_g, ln1_b, w_pq, sub_keys, u_tab, v_tab, ln2_g, ln2_b):
    bsz, seq, d = x.shape
    depth = w_ada.shape[0]
    alpha = (2 * depth) ** 0.25
    slopes = jnp.exp2(-8.0 * jnp.arange(1, N_ATT_HEADS + 1, dtype=F32) / N_ATT_HEADS)
    nb = CHUNK_BATCHES
    for l in range(depth):
        lambda_init = 0.8 - 0.6 * math.exp(-0.3 * l)
        mod3 = _mod(c, w_ada[l], b_ada[l]).reshape(bsz, 6, d)
        lam_vecs = jnp.stack([lambda_q1[l], lambda_k1[l], lambda_q2[l], lambda_k2[l]])
        w = dict(w_in=w_in[l].astype(BF16), subln_g=subln_g[l], sinks=sinks[l], w_out=w_out[l].astype(BF16),
                 ln1=jnp.stack([ln1_g[l], ln1_b[l]]), w_pq=w_pq[l].astype(BF16),
                 sub_keys=sub_keys[l].astype(BF16), u_pack=_pack_table(u_tab[l]), v_pack=_pack_table(v_tab[l]),
                 ln2=jnp.stack([ln2_g[l], ln2_b[l]]))
        outs = []
        for ci, b0 in enumerate(range(0, bsz, nb)):
            mod_c = mod3
            if ci >= CHUNK_LAG:
                mod_c, outs[ci - CHUNK_LAG] = lax.optimization_barrier((mod3, outs[ci - CHUNK_LAG]))
            outs.append(_layer_chunk(x, b0, nb, mod_c, slopes, lam_vecs, lambda_init, alpha, w))
        x = jnp.concatenate(outs, axis=0).reshape(bsz, seq, d)
    return x
```

```python
import functools
import math

import jax
import jax.numpy as jnp
from jax import lax
from jax.experimental import pallas as pl
from jax.experimental.pallas import tpu as pltpu
from jax.experimental.pallas import tpu_sc as plsc

F32 = jnp.float32
BF16 = jnp.bfloat16
I32 = jnp.int32

HEAD_DIM = 64
DIFF_HEADS = 4
DIFF_V = 2 * HEAD_DIM
SWA_Q_HEADS = 8
SWA_KV_HEADS = 2
SWA_GROUP = SWA_Q_HEADS // SWA_KV_HEADS
WINDOW = 128
N_ATT_HEADS = SWA_Q_HEADS + DIFF_HEADS
PEER_HEADS = 8
N_KEYS = 128
PEER_HALF = 128
PEER_TOPK = 16
PEER_SLOTS = PEER_HEADS * PEER_TOPK
LN_EPS = 1e-5
NEG_INF = -1e30

LANES = 128
VMEM_LIMIT = 48 * 1024 * 1024

SC_CORES = 2
SC_SUBCORES = 16
SC_WORKERS = SC_CORES * SC_SUBCORES


def _nt_dot(a, b):
    return lax.dot_general(a, b, (((1,), (1,)), ((), ())), preferred_element_type=F32)


def _mod_kernel(c_ref, w_ref, b_ref, o_ref):
    c = c_ref[...]
    s = c * (1.0 / (1.0 + jnp.exp(-c)))
    o_ref[...] = jnp.dot(s.astype(BF16), w_ref[...].astype(BF16), preferred_element_type=F32) + b_ref[...]


def _mod(c, w, b):
    bsz, d = c.shape
    n = w.shape[1]
    tn = 768
    return pl.pallas_call(
        _mod_kernel,
        grid=(n // tn,),
        in_specs=[
            pl.BlockSpec((bsz, d), lambda j: (0, 0)),
            pl.BlockSpec((d, tn), lambda j: (0, j)),
            pl.BlockSpec((1, tn), lambda j: (0, j)),
        ],
        out_specs=pl.BlockSpec((bsz, tn), lambda j: (0, j)),
        out_shape=jax.ShapeDtypeStruct((bsz, n), F32),
        compiler_params=pltpu.CompilerParams(dimension_semantics=("arbitrary",), vmem_limit_bytes=VMEM_LIMIT),
        name="adaln_mod",
    )(c, w, b.reshape(1, n))


def _inproj_kernel(x_ref, mod_ref, w_ref, o_ref):
    h = x_ref[...] * (1.0 + mod_ref[1:2, :]) + mod_ref[0:1, :]
    o_ref[...] = jnp.dot(h.astype(BF16), w_ref[...], preferred_element_type=F32).astype(BF16)


def _inproj(x, mod3, w_bf16, b0, bsz):
    _, s, d = x.shape
    n = w_bf16.shape[1]
    tm = 512
    return pl.pallas_call(
        _inproj_kernel,
        grid=(bsz, s // tm),
        in_specs=[
            pl.BlockSpec((None, tm, d), lambda b, i: (b0 + b, i, 0)),
            pl.BlockSpec((None, 6, d), lambda b, i: (b0 + b, 0, 0)),
            pl.BlockSpec((d, n), lambda b, i: (0, 0)),
        ],
        out_specs=pl.BlockSpec((None, tm, n), lambda b, i: (b, i, 0)),
        out_shape=jax.ShapeDtypeStruct((bsz, s, n), BF16),
        compiler_params=pltpu.CompilerParams(
            dimension_semantics=("arbitrary", "arbitrary"), vmem_limit_bytes=VMEM_LIMIT),
        name="in_proj",
    )(x, mod3, w_bf16)


def _diff_kernel(slopes_ref, q_ref, k_ref, v_ref, lam_ref, g_ref, o_ref, *, tq, lambda_init):
    h = pl.program_id(1)
    i = pl.program_id(2)
    slope = slopes_ref[SWA_Q_HEADS + h]
    q = q_ref[...] * (HEAD_DIM ** -0.5)
    qs = (q[:, :HEAD_DIM], q[:, HEAD_DIM:])

    def tile(j, carry, diagonal):
        ks = k_ref[pl.ds(pl.multiple_of(j * tq, tq), tq), :]
        vs = v_ref[pl.ds(pl.multiple_of(j * tq, tq), tq), :]
        col_bias = slope * (j * tq + lax.broadcasted_iota(I32, (1, tq), 1)).astype(F32)
        if diagonal:
            valid = lax.broadcasted_iota(I32, (tq, tq), 0) >= lax.broadcasted_iota(I32, (tq, tq), 1)
        new = []
        for m in range(2):
            mx, l, acc = carry[3 * m: 3 * m + 3]
            s = _nt_dot(qs[m], ks[:, m * HEAD_DIM:(m + 1) * HEAD_DIM]) + col_bias
            if diagonal:
                s = jnp.where(valid, s, NEG_INF)
            mx_new = jnp.maximum(mx, jnp.max(s, axis=-1, keepdims=True))
            p = jnp.exp(s - mx_new)
            corr = jnp.exp(mx - mx_new)
            l = l * corr + jnp.sum(p, axis=-1, keepdims=True)
            acc = acc * corr + jnp.dot(p.astype(BF16), vs, preferred_element_type=F32)
            new += [mx_new, l, acc]
        return tuple(new)

    init = []
    for _ in range(2):
        init += [jnp.full((tq, 1), NEG_INF, F32), jnp.zeros((tq, 1), F32), jnp.zeros((tq, DIFF_V), F32)]
    carry = lax.fori_loop(0, i, lambda j, c: tile(j, c, False), tuple(init))
    m0, l0, a0, m1, l1, a1 = tile(i, carry, True)

    lam_v = lam_ref[...]
    lam = (jnp.exp(jnp.sum(lam_v[0:1, :] * lam_v[1:2, :], axis=-1, keepdims=True))
           - jnp.exp(jnp.sum(lam_v[2:3, :] * lam_v[3:4, :], axis=-1, keepdims=True)) + lambda_init)
    o = a0 / l0 - lam * (a1 / l1)
    o = o * lax.rsqrt(jnp.mean(o * o, axis=-1, keepdims=True) + LN_EPS)
    o_ref[...] = (o * g_ref[...] * (1.0 - lambda_init)).astype(BF16)


def _diff_attention(proj, slopes, lam_vecs, subln_g, lambda_init):
    bsz, s, _ = proj.shape
    tq = 256
    kcol = DIFF_HEADS
    vcol = 2 * DIFF_HEADS
    return pl.pallas_call(
        functools.partial(_diff_kernel, tq=tq, lambda_init=lambda_init),
        grid=(bsz, DIFF_HEADS, s // tq),
        in_specs=[
            pl.BlockSpec(memory_space=pltpu.SMEM),
            pl.BlockSpec((None, tq, DIFF_V), lambda b, h, i: (b, i, h)),
            pl.BlockSpec((None, s, DIFF_V), lambda b, h, i: (b, 0, kcol + h)),
            pl.BlockSpec((None, s, DIFF_V), lambda b, h, i: (b, 0, vcol + h)),
            pl.BlockSpec((4, HEAD_DIM), lambda b, h, i: (0, 0)),
            pl.BlockSpec((1, DIFF_V), lambda b, h, i: (0, 0)),
        ],
        out_specs=pl.BlockSpec((None, tq, DIFF_V), lambda b, h, i: (b, i, h)),
        out_shape=jax.ShapeDtypeStruct((bsz, s, DIFF_HEADS * DIFF_V), BF16),
        compiler_params=pltpu.CompilerParams(
            dimension_semantics=("arbitrary", "arbitrary", "arbitrary"), vmem_limit_bytes=VMEM_LIMIT),
        name="diff_attention",
    )(slopes, proj, proj, proj, lam_vecs, subln_g.reshape(1, DIFF_V))


def _swa_kernel(slopes_ref, sinks_ref, q_ref, k_ref, v_ref, o_ref, *, tq):
    i = pl.program_id(1)
    scale = HEAD_DIM ** -0.5
    blk = WINDOW
    ii = lax.broadcasted_iota(I32, (blk, 2 * blk), 0)
    jj = lax.broadcasted_iota(I32, (blk, 2 * blk), 1)
    for r in range(tq // blk):
        start = i * tq + r * blk
        kstart = jnp.maximum(start - blk, 0)
        kb = k_ref[pl.ds(pl.multiple_of(kstart, blk), 2 * blk), :]
        vb = v_ref[pl.ds(pl.multiple_of(kstart, blk), 2 * blk), :]
        dist = (start + ii) - (kstart + jj)
        valid = (dist >= 0) & (dist < WINDOW)
        distf = dist.astype(F32)
        outs = []
        for kvh in range(SWA_KV_HEADS):
            k = kb[:, kvh * HEAD_DIM:(kvh + 1) * HEAD_DIM]
            v = vb[:, kvh * HEAD_DIM:(kvh + 1) * HEAD_DIM]
            for g in range(SWA_GROUP):
                hq = kvh * SWA_GROUP + g
                qh = q_ref[r * blk:(r + 1) * blk, hq * HEAD_DIM:(hq + 1) * HEAD_DIM]
                s = _nt_dot(qh, k) * scale - slopes_ref[hq] * distf
                s = jnp.where(valid, s, NEG_INF)
                sink = sinks_ref[hq]
                m = jnp.maximum(jnp.max(s, axis=-1, keepdims=True), sink)
                p = jnp.exp(s - m)
                denom = jnp.sum(p, axis=-1, keepdims=True) + jnp.exp(sink - m)
                outs.append(jnp.dot(p.astype(BF16), v, preferred_element_type=F32) / denom)
        o_ref[r * blk:(r + 1) * blk, :] = jnp.concatenate(outs, axis=-1).astype(BF16)


def _swa_attention(proj, slopes, sinks):
    bsz, s, _ = proj.shape
    tq = 256
    width = SWA_Q_HEADS * HEAD_DIM
    qcol = (3 * DIFF_HEADS * DIFF_V) // width
    kcol = (3 * DIFF_HEADS * DIFF_V + width) // LANES
    return pl.pallas_call(
        functools.partial(_swa_kernel, tq=tq),
        grid=(bsz, s // tq),
        in_specs=[
            pl.BlockSpec(memory_space=pltpu.SMEM),
            pl.BlockSpec(memory_space=pltpu.SMEM),
            pl.BlockSpec((None, tq, width), lambda b, i: (b, i, qcol)),
            pl.BlockSpec((None, s, LANES), lambda b, i: (b, 0, kcol)),
            pl.BlockSpec((None, s, LANES), lambda b, i: (b, 0, kcol + 1)),
        ],
        out_specs=pl.BlockSpec((None, tq, width), lambda b, i: (b, i, 0)),
        out_shape=jax.ShapeDtypeStruct((bsz, s, width), BF16),
        compiler_params=pltpu.CompilerParams(
            dimension_semantics=("arbitrary", "arbitrary"), vmem_limit_bytes=VMEM_LIMIT),
        name="swa_attention",
    )(slopes, sinks, proj, proj, proj)


def _layer_norm(y, g, b):
    mu = jnp.mean(y, axis=-1, keepdims=True)
    yc = y - mu
    var = jnp.mean(yc * yc, axis=-1, keepdims=True)
    return yc * lax.rsqrt(var + LN_EPS) * g + b


def _pack_pairs(lo, hi):
    lo_bits = lax.bitcast_convert_type(lo.astype(F32), I32)
    hi_bits = lax.bitcast_convert_type(hi.astype(F32), I32)
    return lax.shift_right_logical(lo_bits, 16) | (hi_bits & jnp.int32(-65536))


def _mid_kernel(do_ref, so_ref, x_ref, mod_ref, wo_ref, ln_ref, wpq_ref, x1_ref, h2_ref, q_ref, *, alpha):
    nd = do_ref.shape[-1]
    mixed = (jnp.dot(do_ref[...], wo_ref[:nd, :], preferred_element_type=F32)
             + jnp.dot(so_ref[...], wo_ref[nd:, :], preferred_element_type=F32))
    y = alpha * x_ref[...] + mod_ref[2:3, :] * mixed
    x1 = _layer_norm(y, ln_ref[0:1, :], ln_ref[1:2, :])
    x1_ref[...] = x1
    h2 = (x1 * (1.0 + mod_ref[4:5, :]) + mod_ref[3:4, :]).astype(BF16)
    half = h2.shape[-1] // 2
    h2_ref[...] = _pack_pairs(h2[:, :half], h2[:, half:])
    q_ref[...] = jnp.dot(h2, wpq_ref[...], preferred_element_type=F32).astype(BF16)


def _mid(diff_out, swa_out, x, mod3, wo_bf16, ln1, wpq_bf16, alpha, b0):
    bsz = diff_out.shape[0]
    _, s, d = x.shape
    nq = wpq_bf16.shape[1]
    tm = 512
    row = lambda b, i: (b, i, 0)
    const = lambda b, i: (0, 0)
    return pl.pallas_call(
        functools.partial(_mid_kernel, alpha=alpha),
        grid=(bsz, s // tm),
        in_specs=[
            pl.BlockSpec((None, tm, diff_out.shape[-1]), row),
            pl.BlockSpec((None, tm, swa_out.shape[-1]), row),
            pl.BlockSpec((None, tm, d), lambda b, i: (b0 + b, i, 0)),
            pl.BlockSpec((None, 6, d), lambda b, i: (b0 + b, 0, 0)),
            pl.BlockSpec(wo_bf16.shape, const),
            pl.BlockSpec((2, d), const),
            pl.BlockSpec(wpq_bf16.shape, const),
        ],
        out_specs=[
            pl.BlockSpec((None, tm, d), row),
            pl.BlockSpec((None, tm, d // 2), row),
            pl.BlockSpec((None, tm, nq), row),
        ],
        out_shape=[
            jax.ShapeDtypeStruct((bsz, s, d), F32),
            jax.ShapeDtypeStruct((bsz, s, d // 2), I32),
            jax.ShapeDtypeStruct((bsz, s, nq), BF16),
        ],
        compiler_params=pltpu.CompilerParams(
            dimension_semantics=("arbitrary", "arbitrary"), vmem_limit_bytes=VMEM_LIMIT),
        name="outproj_ln1_peerq",
    )(diff_out, swa_out, x, mod3, wo_bf16, ln1, wpq_bf16)


def _topk_rows(vals, pos, payload, k):
    out_v, out_p = [], []
    for _ in range(k):
        m = jnp.max(vals, axis=0, keepdims=True)
        first = jnp.min(jnp.where(vals == m, pos, 1e9), axis=0, keepdims=True)
        sel = pos == first
        if payload is None:
            out_p.append(first)
        else:
            out_p.append(jnp.max(jnp.where(sel, payload, -1.0), axis=0, keepdims=True))
        out_v.append(m)
        vals = jnp.where(sel, -jnp.inf, vals)
    return jnp.concatenate(out_v, axis=0), jnp.concatenate(out_p, axis=0)


def _candidates(v0, i0, v1, i1):
    k, lanes = v0.shape
    vals, poss, eids = [], [], []
    for a in range(4):
        nb = k if a == 0 else k // 2
        b_iota = lax.broadcasted_iota(I32, (nb, lanes), 0).astype(F32)
        vals.append(v0[a:a + 1, :] + v1[:nb, :])
        poss.append(a * k + b_iota)
        eids.append(i0[a:a + 1, :] * N_KEYS + i1[:nb, :])
    for b in range(3):
        na = k if b == 0 else k // 2
        a_iota = lax.broadcasted_iota(I32, (na, lanes), 0).astype(F32)
        vals.append(jnp.where(a_iota >= 4.0, v0[:na, :] + v1[b:b + 1, :], -jnp.inf))
        poss.append(a_iota * k + b)
        eids.append(i0[:na, :] * N_KEYS + i1[b:b + 1, :])
    return jnp.concatenate(vals, axis=0), jnp.concatenate(poss, axis=0), jnp.concatenate(eids, axis=0)


def _route_kernel(q_ref, keys_ref, idx_ref, gate_ref, idx_t, gate_t):
    tt = q_ref.shape[0]
    key_pos = lax.broadcasted_iota(I32, (N_KEYS, tt), 0).astype(F32)

    def head(h, carry):
        halves = []
        for p in range(2):
            qh = q_ref[:, pl.ds(pl.multiple_of((2 * h + p) * PEER_HALF, PEER_HALF), PEER_HALF)]
            sc = _nt_dot(keys_ref[h, p], qh)
            halves.append(_topk_rows(sc, key_pos, None, PEER_TOPK))
        (v0, i0), (v1, i1) = halves
        cv, cp, ce = _candidates(v0, i0, v1, i1)
        top_s, top_e = _topk_rows(cv, cp, ce, PEER_TOPK)
        e = jnp.exp(top_s - top_s[0:1, :])
        gate = e / jnp.sum(e, axis=0, keepdims=True)
        rows = pl.ds(pl.multiple_of(h * PEER_TOPK, PEER_TOPK), PEER_TOPK)
        idx_t[rows, :] = top_e
        gate_t[rows, :] = gate
        return carry

    lax.fori_loop(0, PEER_HEADS, head, 0)
    idx_ref[...] = idx_t[...].T.astype(I32)
    gate_ref[...] = gate_t[...].T


def _route(q2d, keys_bf16):
    t, nq = q2d.shape
    tt = 2 * LANES
    return pl.pallas_call(
        _route_kernel,
        grid=(t // tt,),
        in_specs=[
            pl.BlockSpec((tt, nq), lambda i: (i, 0)),
            pl.BlockSpec(keys_bf16.shape, lambda i: (0, 0, 0, 0)),
        ],
        out_specs=[
            pl.BlockSpec((tt, PEER_SLOTS), lambda i: (i, 0)),
            pl.BlockSpec((tt, PEER_SLOTS), lambda i: (i, 0)),
        ],
        out_shape=[
            jax.ShapeDtypeStruct((t, PEER_SLOTS), I32),
            jax.ShapeDtypeStruct((t, PEER_SLOTS), F32),
        ],
        scratch_shapes=[pltpu.VMEM((PEER_SLOTS, tt), F32), pltpu.VMEM((PEER_SLOTS, tt), F32)],
        compiler_params=pltpu.CompilerParams(dimension_semantics=("arbitrary",), vmem_limit_bytes=VMEM_LIMIT),
        name="peer_route",
    )(q2d, keys_bf16)


def _pack_table(tab):
    half = tab.shape[1] // 2
    bits = lax.bitcast_convert_type(tab.astype(BF16), jnp.uint16).astype(jnp.uint32)
    return lax.bitcast_convert_type(bits[:, :half] | (bits[:, half:] << 16), I32)


SC_LANES = 16
RING_ROWS = 32
RING_AHEAD = 3
UDOT_GROUP = 16


BF16_GROUP = 4


def _widen_pair_sum(acc, packed_bf16):
    bits = plsc.bitcast(packed_bf16, I32)
    lo = lax.bitcast_convert_type(bits << 16, F32)
    hi = lax.bitcast_convert_type(bits & jnp.int32(-65536), F32)
    return acc + lo + hi


def _udot_rows(rows_ref, h_ref, out_ref, out_base):
    nrows, width = rows_ref.shape
    span = BF16_GROUP * SC_LANES
    lane = lax.iota(I32, SC_LANES)

    @pl.loop(0, nrows // UDOT_GROUP)
    def _(g):
        r0 = g * UDOT_GROUP

        def chunk(c, accs):
            off = pl.multiple_of(c * span, span)
            hs = [plsc.bitcast(h_ref[pl.ds(off + i * SC_LANES, SC_LANES)], BF16) for i in range(BF16_GROUP)]
            out = []
            for r in range(UDOT_GROUP):
                part = None
                for i in range(BF16_GROUP):
                    w = plsc.bitcast(rows_ref[r0 + r, pl.ds(off + i * SC_LANES, SC_LANES)], BF16)
                    part = w * hs[i] if part is None else part + w * hs[i]
                out.append(_widen_pair_sum(accs[r], part))
            return tuple(out)

        zero = jnp.zeros((SC_LANES,), F32)
        accs = lax.fori_loop(0, width // span, chunk, (zero,) * UDOT_GROUP)
        vec = zero
        for r in range(UDOT_GROUP):
            vec = jnp.where(lane == r, jnp.sum(accs[r]), vec)
        out_ref[pl.ds(pl.multiple_of(out_base + r0, UDOT_GROUP), UDOT_GROUP)] = vec


def _sc_udot(table, idx, h):
    t, k = idx.shape
    width = table.shape[1]
    tpw = t // SC_WORKERS
    slots = k // RING_ROWS
    assert tpw * SC_WORKERS == t and tpw % 2 == 0 and slots == RING_AHEAD + 1
    nbuf = tpw * slots
    mesh = plsc.VectorSubcoreMesh(core_axis_name="c", subcore_axis_name="s")

    def body(table_hbm, idx_hbm, h_hbm, out_hbm, idx_v, rows_v, h_v, a_v, gsem, hsem):
        wid = lax.axis_index("s") * SC_CORES + lax.axis_index("c")
        tok0 = wid * tpw
        pltpu.sync_copy(idx_hbm.at[wid], idx_v)

        def h_slot(slot):
            return h_v.at[pl.ds(slot * width, width)]

        def gather(j, slot):
            rows = idx_v.at[pl.ds(pl.multiple_of(j * RING_ROWS, RING_ROWS), RING_ROWS)]
            return pltpu.make_async_copy(table_hbm.at[rows], rows_v.at[slot], gsem.at[slot])

        def h_copy(tok, slot):
            return pltpu.make_async_copy(h_hbm.at[tok0 + tok], h_slot(slot), hsem.at[slot])

        for j in range(RING_AHEAD):
            gather(j, j).start()
        h_copy(0, 0).start()

        @pl.loop(0, tpw, step=2)
        def _(t0):
            for hs in range(2):
                tok = t0 + hs
                h_copy(tok, hs).wait()

                @pl.when(tok + 1 < tpw)
                def _():
                    h_copy(tok + 1, 1 - hs).start()

                for slot in range(slots):
                    j = tok * slots + slot
                    gather(j, slot).wait()

                    @pl.when(j + RING_AHEAD < nbuf)
                    def _():
                        gather(j + RING_AHEAD, (slot + RING_AHEAD) % slots).start()

                    _udot_rows(rows_v.at[slot], h_slot(hs), a_v, tok * k + slot * RING_ROWS)

        pltpu.sync_copy(a_v, out_hbm.at[pl.ds(pl.multiple_of(tok0 * k, 8), tpw * k)])

    out = pl.kernel(
        body,
        out_type=jax.ShapeDtypeStruct((t * k,), F32),
        mesh=mesh,
        scratch_types=[
            pltpu.VMEM((nbuf * RING_ROWS,), I32),
            pltpu.VMEM((slots, RING_ROWS, width), table.dtype),
            pltpu.VMEM((2 * width,), I32),
            pltpu.VMEM((tpw * k,), F32),
            pltpu.SemaphoreType.DMA((slots,)),
            pltpu.SemaphoreType.DMA((2,)),
        ],
        compiler_params=pltpu.CompilerParams(needs_layout_passes=False),
        name="peer_udot",
    )(table, idx.reshape(SC_WORKERS, nbuf * RING_ROWS), h)
    return out.reshape(t, k)


VSUM_CHUNKS = 8


def _vsum_rows(rows_ref, wgt_ref, wgt_base, out_ref, out_base, first):
    nrows, width = rows_ref.shape
    span = VSUM_CHUNKS * SC_LANES

    @pl.loop(0, width // span)
    def _(blk):
        col0 = pl.multiple_of(blk * span, span)

        def row_group(rg, accs):
            r0 = rg * BF16_GROUP
            parts = [None] * VSUM_CHUNKS
            for i in range(BF16_GROUP):
                splat = jnp.full((SC_LANES,), wgt_base + r0 + i, I32)
                wv = plsc.bitcast(plsc.load_gather(wgt_ref, [splat]), BF16)
                for c in range(VSUM_CHUNKS):
                    w = plsc.bitcast(rows_ref[r0 + i, pl.ds(col0 + c * SC_LANES, SC_LANES)], BF16)
                    parts[c] = w * wv if parts[c] is None else parts[c] + w * wv
            out = []
            for c in range(VSUM_CHUNKS):
                bits = plsc.bitcast(parts[c], I32)
                out.append(accs[2 * c] + lax.bitcast_convert_type(bits << 16, F32))
                out.append(accs[2 * c + 1] + lax.bitcast_convert_type(bits & jnp.int32(-65536), F32))
            return tuple(out)

        zero = jnp.zeros((SC_LANES,), F32)
        accs = lax.fori_loop(0, nrows // BF16_GROUP, row_group, (zero,) * (2 * VSUM_CHUNKS))
        for c in range(VSUM_CHUNKS):
            for half in range(2):
                dst = pl.ds(pl.multiple_of(out_base + half * width + col0 + c * SC_LANES, SC_LANES), SC_LANES)
                if first:
                    out_ref[dst] = accs[2 * c + half]
                else:
                    out_ref[dst] = out_ref[dst] + accs[2 * c + half]


def _sc_vsum(table, idx, wgt):
    t, k = idx.shape
    width = table.shape[1]
    d = 2 * width
    tpw = t // SC_WORKERS
    slots = k // RING_ROWS
    assert tpw * SC_WORKERS == t and tpw % 2 == 0 and slots == RING_AHEAD + 1
    nbuf = tpw * slots
    mesh = plsc.VectorSubcoreMesh(core_axis_name="c", subcore_axis_name="s")

    def body(table_hbm, idx_hbm, wgt_hbm, out_hbm, idx_v, rows_v, wgt_v, out_v, gsem, osem):
        wid = lax.axis_index("s") * SC_CORES + lax.axis_index("c")
        tok0 = wid * tpw
        pltpu.sync_copy(idx_hbm.at[wid], idx_v)
        pltpu.sync_copy(wgt_hbm.at[wid], wgt_v)

        def gather(j, slot):
            rows = idx_v.at[pl.ds(pl.multiple_of(j * RING_ROWS, RING_ROWS), RING_ROWS)]
            return pltpu.make_async_copy(table_hbm.at[rows], rows_v.at[slot], gsem.at[slot])

        def put(tok, slot):
            return pltpu.make_async_copy(out_v.at[pl.ds(slot * d, d)], out_hbm.at[tok0 + tok], osem.at[slot])

        for j in range(RING_AHEAD):
            gather(j, j).start()

        @pl.loop(0, tpw, step=2)
        def _(t0):
            for os_ in range(2):
                tok = t0 + os_

                @pl.when(tok >= 2)
                def _():
                    put(tok - 2, os_).wait()

                for slot in range(slots):
                    j = tok * slots + slot
                    gather(j, slot).wait()

                    @pl.when(j + RING_AHEAD < nbuf)
                    def _():
                        gather(j + RING_AHEAD, (slot + RING_AHEAD) % slots).start()

                    _vsum_rows(rows_v.at[slot], wgt_v, tok * k + slot * RING_ROWS, out_v, os_ * d, slot == 0)
                put(tok, os_).start()

        put(tpw - 2, 0).wait()
        put(tpw - 1, 1).wait()

    return pl.kernel(
        body,
        out_type=jax.ShapeDtypeStruct((t, d), F32),
        mesh=mesh,
        scratch_types=[
            pltpu.VMEM((nbuf * RING_ROWS,), I32),
            pltpu.VMEM((slots, RING_ROWS, width), table.dtype),
            pltpu.VMEM((tpw * k,), I32),
            pltpu.VMEM((2 * d,), F32),
            pltpu.SemaphoreType.DMA((slots,)),
            pltpu.SemaphoreType.DMA((2,)),
        ],
        compiler_params=pltpu.CompilerParams(needs_layout_passes=False),
        name="peer_vsum",
    )(table, idx.reshape(SC_WORKERS, nbuf * RING_ROWS), wgt.reshape(SC_WORKERS, tpw * k))


def _wgt_kernel(a_ref, gate_ref, o_ref):
    a = a_ref[...]
    w = (gate_ref[...] * (0.5 * a * (1.0 + lax.erf(a * (2.0 ** -0.5))))).astype(BF16)
    o_ref[...] = _pack_pairs(w, w)


def _expert_weights(a, gate):
    t, k = a.shape
    tm = math.gcd(t, 1024)
    spec = pl.BlockSpec((tm, k), lambda i: (i, 0))
    return pl.pallas_call(
        _wgt_kernel,
        grid=(t // tm,),
        in_specs=[spec, spec],
        out_specs=spec,
        out_shape=jax.ShapeDtypeStruct((t, k), I32),
        compiler_params=pltpu.CompilerParams(dimension_semantics=("arbitrary",), vmem_limit_bytes=VMEM_LIMIT),
        name="peer_weights",
    )(a, gate)


def _final_kernel(x1_ref, ffn_ref, mod_ref, ln_ref, o_ref, *, alpha):
    y = alpha * x1_ref[...] + mod_ref[5:6, :] * ffn_ref[...]
    o_ref[...] = _layer_norm(y, ln_ref[0:1, :], ln_ref[1:2, :])


def _final(x1, ffn, mod3, ln2, alpha, b0):
    nb, s, d = x1.shape
    tm = 512
    row = lambda b, i: (b, i, 0)
    return pl.pallas_call(
        functools.partial(_final_kernel, alpha=alpha),
        grid=(nb, s // tm),
        in_specs=[
            pl.BlockSpec((None, tm, d), row),
            pl.BlockSpec((None, tm, d), row),
            pl.BlockSpec((None, 6, d), lambda b, i: (b0 + b, 0, 0)),
            pl.BlockSpec((2, d), lambda b, i: (0, 0)),
        ],
        out_specs=pl.BlockSpec((None, tm, d), row),
        out_shape=jax.ShapeDtypeStruct((nb, s, d), F32),
        compiler_params=pltpu.CompilerParams(
            dimension_semantics=("arbitrary", "arbitrary"), vmem_limit_bytes=VMEM_LIMIT),
        name="deepnorm_ln2",
    )(x1, ffn, mod3, ln2)


CHUNK_BATCHES = 1
CHUNK_LAG = 6


def _layer_chunk(x, b0, nb, mod3, slopes, lam_vecs, lambda_init, alpha, w):
    _, seq, d = x.shape
    proj = _inproj(x, mod3, w["w_in"], b0, nb)
    diff_out = _diff_attention(proj, slopes, lam_vecs, w["subln_g"], lambda_init)
    swa_out = _swa_attention(proj, slopes, w["sinks"])
    x1, h2, q = _mid(diff_out, swa_out, x, mod3, w["w_out"], w["ln1"], w["w_pq"], alpha, b0)
    idx, gate = _route(q.reshape(nb * seq, -1), w["sub_keys"])
    a = _sc_udot(w["u_pack"], idx, h2.reshape(nb * seq, d // 2))
    ffn = _sc_vsum(w["v_pack"], idx, _expert_weights(a, gate))
    return _final(x1, ffn.reshape(nb, seq, d), mod3, w["ln2"], alpha, b0)


def kernel(x, c, w_ada, b_ada, w_in, lambda_q1, lambda_k1, lambda_q2, lambda_k2, subln_g, sinks, w_out, ln1_g, ln1_b, w_pq, sub_keys, u_tab, v_tab, ln2_g, ln2_b):
    bsz, seq, d = x.shape
    depth = w_ada.shape[0]
    alpha = (2 * depth) ** 0.25
    slopes = jnp.exp2(-8.0 * jnp.arange(1, N_ATT_HEADS + 1, dtype=F32) / N_ATT_HEADS)
    nb = CHUNK_BATCHES
    for l in range(depth):
        lambda_init = 0.8 - 0.6 * math.exp(-0.3 * l)
        mod3 = _mod(c, w_ada[l], b_ada[l]).reshape(bsz, 6, d)
        lam_vecs = jnp.stack([lambda_q1[l], lambda_k1[l], lambda_q2[l], lambda_k2[l]])
        w = dict(w_in=w_in[l].astype(BF16), subln_g=subln_g[l], sinks=sinks[l], w_out=w_out[l].astype(BF16),
                 ln1=jnp.stack([ln1_g[l], ln1_b[l]]), w_pq=w_pq[l].astype(BF16),
                 sub_keys=sub_keys[l].astype(BF16), u_pack=_pack_table(u_tab[l]), v_pack=_pack_table(v_tab[l]),
                 ln2=jnp.stack([ln2_g[l], ln2_b[l]]))
        outs = []
        for ci, b0 in enumerate(range(0, bsz, nb)):
            mod_c = mod3
            if ci >= CHUNK_LAG:
                mod_c, outs[ci - CHUNK_LAG] = lax.optimization_barrier((mod3, outs[ci - CHUNK_LAG]))
            outs.append(_layer_chunk(x, b0, nb, mod_c, slopes, lam_vecs, lambda_init, alpha, w))
        x = jnp.concatenate(outs, axis=0).reshape(bsz, seq, d)
    return x
```

```python
import functools
import math

import jax
import jax.numpy as jnp
from jax import lax
from jax.experimental import pallas as pl
from jax.experimental.pallas import tpu as pltpu
from jax.experimental.pallas import tpu_sc as plsc

F32 = jnp.float32
BF16 = jnp.bfloat16
I32 = jnp.int32

HEAD_DIM = 64
DIFF_HEADS = 4
DIFF_V = 2 * HEAD_DIM
SWA_Q_HEADS = 8
SWA_KV_HEADS = 2
SWA_GROUP = SWA_Q_HEADS // SWA_KV_HEADS
WINDOW = 128
N_ATT_HEADS = SWA_Q_HEADS + DIFF_HEADS
PEER_HEADS = 8
N_KEYS = 128
PEER_HALF = 128
PEER_TOPK = 16
PEER_SLOTS = PEER_HEADS * PEER_TOPK
LN_EPS = 1e-5
NEG_INF = -1e30

LANES = 128
VMEM_LIMIT = 48 * 1024 * 1024

SC_CORES = 2
SC_SUBCORES = 16
SC_WORKERS = SC_CORES * SC_SUBCORES


def _nt_dot(a, b):
    return lax.dot_general(a, b, (((1,), (1,)), ((), ())), preferred_element_type=F32)


def _mod_kernel(c_ref, w_ref, b_ref, o_ref):
    c = c_ref[...]
    s = c * (1.0 / (1.0 + jnp.exp(-c)))
    o_ref[...] = jnp.dot(s.astype(BF16), w_ref[...].astype(BF16), preferred_element_type=F32) + b_ref[...]


def _mod(c, w, b):
    bsz, d = c.shape
    n = w.shape[1]
    tn = 768
    return pl.pallas_call(
        _mod_kernel,
        grid=(n // tn,),
        in_specs=[
            pl.BlockSpec((bsz, d), lambda j: (0, 0)),
            pl.BlockSpec((d, tn), lambda j: (0, j)),
            pl.BlockSpec((1, tn), lambda j: (0, j)),
        ],
        out_specs=pl.BlockSpec((bsz, tn), lambda j: (0, j)),
        out_shape=jax.ShapeDtypeStruct((bsz, n), F32),
        compiler_params=pltpu.CompilerParams(dimension_semantics=("arbitrary",), vmem_limit_bytes=VMEM_LIMIT),
        name="adaln_mod",
    )(c, w, b.reshape(1, n))


def _inproj_kernel(x_ref, mod_ref, w_ref, o_ref):
    h = x_ref[...] * (1.0 + mod_ref[1:2, :]) + mod_ref[0:1, :]
    o_ref[...] = jnp.dot(h.astype(BF16), w_ref[...], preferred_element_type=F32).astype(BF16)


def _inproj(x, mod3, w_bf16, b0, bsz):
    _, s, d = x.shape
    n = w_bf16.shape[1]
    tm = 512
    return pl.pallas_call(
        _inproj_kernel,
        grid=(bsz, s // tm),
        in_specs=[
            pl.BlockSpec((None, tm, d), lambda b, i: (b0 + b, i, 0)),
            pl.BlockSpec((None, 6, d), lambda b, i: (b0 + b, 0, 0)),
            pl.BlockSpec((d, n), lambda b, i: (0, 0)),
        ],
        out_specs=pl.BlockSpec((None, tm, n), lambda b, i: (b, i, 0)),
        out_shape=jax.ShapeDtypeStruct((bsz, s, n), BF16),
        compiler_params=pltpu.CompilerParams(
            dimension_semantics=("arbitrary", "arbitrary"), vmem_limit_bytes=VMEM_LIMIT),
        name="in_proj",
    )(x, mod3, w_bf16)


def _diff_kernel(slopes_ref, q_ref, k_ref, v_ref, lam_ref, g_ref, o_ref, *, tq, lambda_init):
    h = pl.program_id(1)
    i = pl.program_id(2)
    slope = slopes_ref[SWA_Q_HEADS + h]
    q = q_ref[...] * (HEAD_DIM ** -0.5)
    qs = (q[:, :HEAD_DIM], q[:, HEAD_DIM:])

    def tile(j, carry, diagonal):
        ks = k_ref[pl.ds(pl.multiple_of(j * tq, tq), tq), :]
        vs = v_ref[pl.ds(pl.multiple_of(j * tq, tq), tq), :]
        col_bias = slope * (j * tq + lax.broadcasted_iota(I32, (1, tq), 1)).astype(F32)
        if diagonal:
            valid = lax.broadcasted_iota(I32, (tq, tq), 0) >= lax.broadcasted_iota(I32, (tq, tq), 1)
        new = []
        for m in range(2):
            mx, l, acc = carry[3 * m: 3 * m + 3]
            s = _nt_dot(qs[m], ks[:, m * HEAD_DIM:(m + 1) * HEAD_DIM]) + col_bias
            if diagonal:
                s = jnp.where(valid, s, NEG_INF)
            mx_new = jnp.maximum(mx, jnp.max(s, axis=-1, keepdims=True))
            p = jnp.exp(s - mx_new)
            corr = jnp.exp(mx - mx_new)
            l = l * corr + jnp.sum(p, axis=-1, keepdims=True)
            acc = acc * corr + jnp.dot(p.astype(BF16), vs, preferred_element_type=F32)
            new += [mx_new, l, acc]
        return tuple(new)

    init = []
    for _ in range(2):
        init += [jnp.full((tq, 1), NEG_INF, F32), jnp.zeros((tq, 1), F32), jnp.zeros((tq, DIFF_V), F32)]
    carry = lax.fori_loop(0, i, lambda j, c: tile(j, c, False), tuple(init))
    m0, l0, a0, m1, l1, a1 = tile(i, carry, True)

    lam_v = lam_ref[...]
    lam = (jnp.exp(jnp.sum(lam_v[0:1, :] * lam_v[1:2, :], axis=-1, keepdims=True))
           - jnp.exp(jnp.sum(lam_v[2:3, :] * lam_v[3:4, :], axis=-1, keepdims=True)) + lambda_init)
    o = a0 / l0 - lam * (a1 / l1)
    o = o * lax.rsqrt(jnp.mean(o * o, axis=-1, keepdims=True) + LN_EPS)
    o_ref[...] = (o * g_ref[...] * (1.0 - lambda_init)).astype(BF16)


def _diff_attention(proj, slopes, lam_vecs, subln_g, lambda_init):
    bsz, s, _ = proj.shape
    tq = 256
    kcol = DIFF_HEADS
    vcol = 2 * DIFF_HEADS
    return pl.pallas_call(
        functools.partial(_diff_kernel, tq=tq, lambda_init=lambda_init),
        grid=(bsz, DIFF_HEADS, s // tq),
        in_specs=[
            pl.BlockSpec(memory_space=pltpu.SMEM),
            pl.BlockSpec((None, tq, DIFF_V), lambda b, h, i: (b, i, h)),
            pl.BlockSpec((None, s, DIFF_V), lambda b, h, i: (b, 0, kcol + h)),
            pl.BlockSpec((None, s, DIFF_V), lambda b, h, i: (b, 0, vcol + h)),
            pl.BlockSpec((4, HEAD_DIM), lambda b, h, i: (0, 0)),
            pl.BlockSpec((1, DIFF_V), lambda b, h, i: (0, 0)),
        ],
        out_specs=pl.BlockSpec((None, tq, DIFF_V), lambda b, h, i: (b, i, h)),
        out_shape=jax.ShapeDtypeStruct((bsz, s, DIFF_HEADS * DIFF_V), BF16),
        compiler_params=pltpu.CompilerParams(
            dimension_semantics=("arbitrary", "arbitrary", "arbitrary"), vmem_limit_bytes=VMEM_LIMIT),
        name="diff_attention",
    )(slopes, proj, proj, proj, lam_vecs, subln_g.reshape(1, DIFF_V))


def _swa_kernel(slopes_ref, sinks_ref, q_ref, k_ref, v_ref, o_ref, *, tq):
    i = pl.program_id(1)
    scale = HEAD_DIM ** -0.5
    blk = WINDOW
    ii = lax.broadcasted_iota(I32, (blk, 2 * blk), 0)
    jj = lax.broadcasted_iota(I32, (blk, 2 * blk), 1)
    for r in range(tq // blk):
        start = i * tq + r * blk
        kstart = jnp.maximum(start - blk, 0)
        kb = k_ref[pl.ds(pl.multiple_of(kstart, blk), 2 * blk), :]
        vb = v_ref[pl.ds(pl.multiple_of(kstart, blk), 2 * blk), :]
        dist = (start + ii) - (kstart + jj)
        valid = (dist >= 0) & (dist < WINDOW)
        distf = dist.astype(F32)
        outs = []
        for kvh in range(SWA_KV_HEADS):
            k = kb[:, kvh * HEAD_DIM:(kvh + 1) * HEAD_DIM]
            v = vb[:, kvh * HEAD_DIM:(kvh + 1) * HEAD_DIM]
            for g in range(SWA_GROUP):
                hq = kvh * SWA_GROUP + g
                qh = q_ref[r * blk:(r + 1) * blk, hq * HEAD_DIM:(hq + 1) * HEAD_DIM]
                s = _nt_dot(qh, k) * scale - slopes_ref[hq] * distf
                s = jnp.where(valid, s, NEG_INF)
                sink = sinks_ref[hq]
                m = jnp.maximum(jnp.max(s, axis=-1, keepdims=True), sink)
                p = jnp.exp(s - m)
                denom = jnp.sum(p, axis=-1, keepdims=True) + jnp.exp(sink - m)
                outs.append(jnp.dot(p.astype(BF16), v, preferred_element_type=F32) / denom)
        o_ref[r * blk:(r + 1) * blk, :] = jnp.concatenate(outs, axis=-1).astype(BF16)


def _swa_attention(proj, slopes, sinks):
    bsz, s, _ = proj.shape
    tq = 256
    width = SWA_Q_HEADS * HEAD_DIM
    qcol = (3 * DIFF_HEADS * DIFF_V) // width
    kcol = (3 * DIFF_HEADS * DIFF_V + width) // LANES
    return pl.pallas_call(
        functools.partial(_swa_kernel, tq=tq),
        grid=(bsz, s // tq),
        in_specs=[
            pl.BlockSpec(memory_space=pltpu.SMEM),
            pl.BlockSpec(memory_space=pltpu.SMEM),
            pl.BlockSpec((None, tq, width), lambda b, i: (b, i, qcol)),
            pl.BlockSpec((None, s, LANES), lambda b, i: (b, 0, kcol)),
            pl.BlockSpec((None, s, LANES), lambda b, i: (b, 0, kcol + 1)),
        ],
        out_specs=pl.BlockSpec((None, tq, width), lambda b, i: (b, i, 0)),
        out_shape=jax.ShapeDtypeStruct((bsz, s, width), BF16),
        compiler_params=pltpu.CompilerParams(
            dimension_semantics=("arbitrary", "arbitrary"), vmem_limit_bytes=VMEM_LIMIT),
        name="swa_attention",
    )(slopes, sinks, proj, proj, proj)


def _layer_norm(y, g, b):
    mu = jnp.mean(y, axis=-1, keepdims=True)
    yc = y - mu
    var = jnp.mean(yc * yc, axis=-1, keepdims=True)
    return yc * lax.rsqrt(var + LN_EPS) * g + b


def _pack_pairs(lo, hi):
    lo_bits = lax.bitcast_convert_type(lo.astype(F32), I32)
    hi_bits = lax.bitcast_convert_type(hi.astype(F32), I32)
    return lax.shift_right_logical(lo_bits, 16) | (hi_bits & jnp.int32(-65536))


def _mid_kernel(do_ref, so_ref, x_ref, mod_ref, wo_ref, ln_ref, wpq_ref, x1_ref, h2_ref, q_ref, *, alpha):
    nd = do_ref.shape[-1]
    mixed = (jnp.dot(do_ref[...], wo_ref[:nd, :], preferred_element_type=F32)
             + jnp.dot(so_ref[...], wo_ref[nd:, :], preferred_element_type=F32))
    y = alpha * x_ref[...] + mod_ref[2:3, :] * mixed
    x1 = _layer_norm(y, ln_ref[0:1, :], ln_ref[1:2, :])
    x1_ref[...] = x1
    h2 = (x1 * (1.0 + mod_ref[4:5, :]) + mod_ref[3:4, :]).astype(BF16)
    half = h2.shape[-1] // 2
    h2_ref[...] = _pack_pairs(h2[:, :half], h2[:, half:])
    q_ref[...] = jnp.dot(h2, wpq_ref[...], preferred_element_type=F32).astype(BF16)


def _mid(diff_out, swa_out, x, mod3, wo_bf16, ln1, wpq_bf16, alpha, b0):
    bsz = diff_out.shape[0]
    _, s, d = x.shape
    nq = wpq_bf16.shape[1]
    tm = 512
    row = lambda b, i: (b, i, 0)
    const = lambda b, i: (0, 0)
    return pl.pallas_call(
        functools.partial(_mid_kernel, alpha=alpha),
        grid=(bsz, s // tm),
        in_specs=[
            pl.BlockSpec((None, tm, diff_out.shape[-1]), row),
            pl.BlockSpec((None, tm, swa_out.shape[-1]), row),
            pl.BlockSpec((None, tm, d), lambda b, i: (b0 + b, i, 0)),
            pl.BlockSpec((None, 6, d), lambda b, i: (b0 + b, 0, 0)),
            pl.BlockSpec(wo_bf16.shape, const),
            pl.BlockSpec((2, d), const),
            pl.BlockSpec(wpq_bf16.shape, const),
        ],
        out_specs=[
            pl.BlockSpec((None, tm, d), row),
            pl.BlockSpec((None, tm, d // 2), row),
            pl.BlockSpec((None, tm, nq), row),
        ],
        out_shape=[
            jax.ShapeDtypeStruct((bsz, s, d), F32),
            jax.ShapeDtypeStruct((bsz, s, d // 2), I32),
            jax.ShapeDtypeStruct((bsz, s, nq), BF16),
        ],
        compiler_params=pltpu.CompilerParams(
            dimension_semantics=("arbitrary", "arbitrary"), vmem_limit_bytes=VMEM_LIMIT),
        name="outproj_ln1_peerq",
    )(diff_out, swa_out, x, mod3, wo_bf16, ln1, wpq_bf16)


def _topk_rows(vals, pos, payload, k):
    out_v, out_p = [], []
    for _ in range(k):
        m = jnp.max(vals, axis=0, keepdims=True)
        first = jnp.min(jnp.where(vals == m, pos, 1e9), axis=0, keepdims=True)
        sel = pos == first
        if payload is None:
            out_p.append(first)
        else:
            out_p.append(jnp.max(jnp.where(sel, payload, -1.0), axis=0, keepdims=True))
        out_v.append(m)
        vals = jnp.where(sel, -jnp.inf, vals)
    return jnp.concatenate(out_v, axis=0), jnp.concatenate(out_p, axis=0)


def _candidates(v0, i0, v1, i1):
    k, lanes = v0.shape
    vals, poss, eids = [], [], []
    for a in range(4):
        nb = k if a == 0 else k // 2
        b_iota = lax.broadcasted_iota(I32, (nb, lanes), 0).astype(F32)
        vals.append(v0[a:a + 1, :] + v1[:nb, :])
        poss.append(a * k + b_iota)
        eids.append(i0[a:a + 1, :] * N_KEYS + i1[:nb, :])
    for b in range(3):
        na = k if b == 0 else k // 2
        a_iota = lax.broadcasted_iota(I32, (na, lanes), 0).astype(F32)
        vals.append(jnp.where(a_iota >= 4.0, v0[:na, :] + v1[b:b + 1, :], -jnp.inf))
        poss.append(a_iota * k + b)
        eids.append(i0[:na, :] * N_KEYS + i1[b:b + 1, :])
    return jnp.concatenate(vals, axis=0), jnp.concatenate(poss, axis=0), jnp.concatenate(eids, axis=0)


def _route_kernel(q_ref, keys_ref, idx_ref, gate_ref, idx_t, gate_t):
    tt = q_ref.shape[0]
    key_pos = lax.broadcasted_iota(I32, (N_KEYS, tt), 0).astype(F32)

    def head(h, carry):
        halves = []
        for p in range(2):
            qh = q_ref[:, pl.ds(pl.multiple_of((2 * h + p) * PEER_HALF, PEER_HALF), PEER_HALF)]
            sc = _nt_dot(keys_ref[h, p], qh)
            halves.append(_topk_rows(sc, key_pos, None, PEER_TOPK))
        (v0, i0), (v1, i1) = halves
        cv, cp, ce = _candidates(v0, i0, v1, i1)
        top_s, top_e = _topk_rows(cv, cp, ce, PEER_TOPK)
        e = jnp.exp(top_s - top_s[0:1, :])
        gate = e / jnp.sum(e, axis=0, keepdims=True)
        rows = pl.ds(pl.multiple_of(h * PEER_TOPK, PEER_TOPK), PEER_TOPK)
        idx_t[rows, :] = top_e
        gate_t[rows, :] = gate
        return carry

    lax.fori_loop(0, PEER_HEADS, head, 0)
    idx_ref[...] = idx_t[...].T.astype(I32)
    gate_ref[...] = gate_t[...].T


def _route(q2d, keys_bf16):
    t, nq = q2d.shape
    tt = 2 * LANES
    return pl.pallas_call(
        _route_kernel,
        grid=(t // tt,),
        in_specs=[
            pl.BlockSpec((tt, nq), lambda i: (i, 0)),
            pl.BlockSpec(keys_bf16.shape, lambda i: (0, 0, 0, 0)),
        ],
        out_specs=[
            pl.BlockSpec((tt, PEER_SLOTS), lambda i: (i, 0)),
            pl.BlockSpec((tt, PEER_SLOTS), lambda i: (i, 0)),
        ],
        out_shape=[
            jax.ShapeDtypeStruct((t, PEER_SLOTS), I32),
            jax.ShapeDtypeStruct((t, PEER_SLOTS), F32),
        ],
        scratch_shapes=[pltpu.VMEM((PEER_SLOTS, tt), F32), pltpu.VMEM((PEER_SLOTS, tt), F32)],
        compiler_params=pltpu.CompilerParams(dimension_semantics=("arbitrary",), vmem_limit_bytes=VMEM_LIMIT),
        name="peer_route",
    )(q2d, keys_bf16)


def _pack_table(tab):
    half = tab.shape[1] // 2
    bits = lax.bitcast_convert_type(tab.astype(BF16), jnp.uint16).astype(jnp.uint32)
    return lax.bitcast_convert_type(bits[:, :half] | (bits[:, half:] << 16), I32)


SC_LANES = 16
RING_ROWS = 32
RING_AHEAD = 3
UDOT_GROUP = 16


BF16_GROUP = 4


def _widen_pair_sum(acc, packed_bf16):
    bits = plsc.bitcast(packed_bf16, I32)
    lo = lax.bitcast_convert_type(bits << 16, F32)
    hi = lax.bitcast_convert_type(bits & jnp.int32(-65536), F32)
    return acc + lo + hi


def _udot_rows(rows_ref, h_ref, out_ref, out_base):
    nrows, width = rows_ref.shape
    span = BF16_GROUP * SC_LANES
    lane = lax.iota(I32, SC_LANES)

    @pl.loop(0, nrows // UDOT_GROUP)
    def _(g):
        r0 = g * UDOT_GROUP

        def chunk(c, accs):
            off = pl.multiple_of(c * span, span)
            hs = [plsc.bitcast(h_ref[pl.ds(off + i * SC_LANES, SC_LANES)], BF16) for i in range(BF16_GROUP)]
            out = []
            for r in range(UDOT_GROUP):
                part = None
                for i in range(BF16_GROUP):
                    w = plsc.bitcast(rows_ref[r0 + r, pl.ds(off + i * SC_LANES, SC_LANES)], BF16)
                    part = w * hs[i] if part is None else part + w * hs[i]
                out.append(_widen_pair_sum(accs[r], part))
            return tuple(out)

        zero = jnp.zeros((SC_LANES,), F32)
        accs = lax.fori_loop(0, width // span, chunk, (zero,) * UDOT_GROUP)
        vec = zero
        for r in range(UDOT_GROUP):
            vec = jnp.where(lane == r, jnp.sum(accs[r]), vec)
        out_ref[pl.ds(pl.multiple_of(out_base + r0, UDOT_GROUP), UDOT_GROUP)] = vec


def _sc_udot(table, idx, h):
    t, k = idx.shape
    width = table.shape[1]
    tpw = t // SC_WORKERS
    slots = k // RING_ROWS
    assert tpw * SC_WORKERS == t and tpw % 2 == 0 and slots == RING_AHEAD + 1
    nbuf = tpw * slots
    mesh = plsc.VectorSubcoreMesh(core_axis_name="c", subcore_axis_name="s")

    def body(table_hbm, idx_hbm, h_hbm, out_hbm, idx_v, rows_v, h_v, a_v, gsem, hsem):
        wid = lax.axis_index("s") * SC_CORES + lax.axis_index("c")
        tok0 = wid * tpw
        pltpu.sync_copy(idx_hbm.at[wid], idx_v)

        def h_slot(slot):
            return h_v.at[pl.ds(slot * width, width)]

        def gather(j, slot):
            rows = idx_v.at[pl.ds(pl.multiple_of(j * RING_ROWS, RING_ROWS), RING_ROWS)]
            return pltpu.make_async_copy(table_hbm.at[rows], rows_v.at[slot], gsem.at[slot])

        def h_copy(tok, slot):
            return pltpu.make_async_copy(h_hbm.at[tok0 + tok], h_slot(slot), hsem.at[slot])

        for j in range(RING_AHEAD):
            gather(j, j).start()
        h_copy(0, 0).start()

        @pl.loop(0, tpw, step=2)
        def _(t0):
            for hs in range(2):
                tok = t0 + hs
                h_copy(tok, hs).wait()

                @pl.when(tok + 1 < tpw)
                def _():
                    h_copy(tok + 1, 1 - hs).start()

                for slot in range(slots):
                    j = tok * slots + slot
                    gather(j, slot).wait()

                    @pl.when(j + RING_AHEAD < nbuf)
                    def _():
                        gather(j + RING_AHEAD, (slot + RING_AHEAD) % slots).start()

                    _udot_rows(rows_v.at[slot], h_slot(hs), a_v, tok * k + slot * RING_ROWS)

        pltpu.sync_copy(a_v, out_hbm.at[pl.ds(pl.multiple_of(tok0 * k, 8), tpw * k)])

    out = pl.kernel(
        body,
        out_type=jax.ShapeDtypeStruct((t * k,), F32),
        mesh=mesh,
        scratch_types=[
            pltpu.VMEM((nbuf * RING_ROWS,), I32),
            pltpu.VMEM((slots, RING_ROWS, width), table.dtype),
            pltpu.VMEM((2 * width,), I32),
            pltpu.VMEM((tpw * k,), F32),
            pltpu.SemaphoreType.DMA((slots,)),
            pltpu.SemaphoreType.DMA((2,)),
        ],
        compiler_params=pltpu.CompilerParams(needs_layout_passes=False),
        name="peer_udot",
    )(table, idx.reshape(SC_WORKERS, nbuf * RING_ROWS), h)
    return out.reshape(t, k)


VSUM_CHUNKS = 8


def _vsum_rows(rows_ref, wgt_ref, wgt_base, out_ref, out_base, first):
    nrows, width = rows_ref.shape
    span = VSUM_CHUNKS * SC_LANES

    @pl.loop(0, width // span)
    def _(blk):
        col0 = pl.multiple_of(blk * span, span)

        def row_group(rg, accs):
            r0 = rg * BF16_GROUP
            parts = [None] * VSUM_CHUNKS
            wvec = wgt_ref[pl.ds(pl.multiple_of(wgt_base + (r0 // SC_LANES) * SC_LANES, SC_LANES), SC_LANES)]
            for i in range(BF16_GROUP):
                lane = jnp.full((SC_LANES,), r0 % SC_LANES + i, I32)
                wv = plsc.bitcast(wvec.at[lane].get(mode="promise_in_bounds"), BF16)
                for c in range(VSUM_CHUNKS):
                    w = plsc.bitcast(rows_ref[r0 + i, pl.ds(col0 + c * SC_LANES, SC_LANES)], BF16)
                    parts[c] = w * wv if parts[c] is None else parts[c] + w * wv
            out = []
            for c in range(VSUM_CHUNKS):
                bits = plsc.bitcast(parts[c], I32)
                out.append(accs[2 * c] + lax.bitcast_convert_type(bits << 16, F32))
                out.append(accs[2 * c + 1] + lax.bitcast_convert_type(bits & jnp.int32(-65536), F32))
            return tuple(out)

        zero = jnp.zeros((SC_LANES,), F32)
        accs = lax.fori_loop(0, nrows // BF16_GROUP, row_group, (zero,) * (2 * VSUM_CHUNKS))
        for c in range(VSUM_CHUNKS):
            for half in range(2):
                dst = pl.ds(pl.multiple_of(out_base + half * width + col0 + c * SC_LANES, SC_LANES), SC_LANES)
                if first:
                    out_ref[dst] = accs[2 * c + half]
                else:
                    out_ref[dst] = out_ref[dst] + accs[2 * c + half]


def _sc_vsum(table, idx, wgt):
    t, k = idx.shape
    width = table.shape[1]
    d = 2 * width
    tpw = t // SC_WORKERS
    slots = k // RING_ROWS
    assert tpw * SC_WORKERS == t and tpw % 2 == 0 and slots == RING_AHEAD + 1
    nbuf = tpw * slots
    mesh = plsc.VectorSubcoreMesh(core_axis_name="c", subcore_axis_name="s")

    def body(table_hbm, idx_hbm, wgt_hbm, out_hbm, idx_v, rows_v, wgt_v, out_v, gsem, osem):
        wid = lax.axis_index("s") * SC_CORES + lax.axis_index("c")
        tok0 = wid * tpw
        pltpu.sync_copy(idx_hbm.at[wid], idx_v)
        pltpu.sync_copy(wgt_hbm.at[wid], wgt_v)

        def gather(j, slot):
            rows = idx_v.at[pl.ds(pl.multiple_of(j * RING_ROWS, RING_ROWS), RING_ROWS)]
            return pltpu.make_async_copy(table_hbm.at[rows], rows_v.at[slot], gsem.at[slot])

        def put(tok, slot):
            return pltpu.make_async_copy(out_v.at[pl.ds(slot * d, d)], out_hbm.at[tok0 + tok], osem.at[slot])

        for j in range(RING_AHEAD):
            gather(j, j).start()

        @pl.loop(0, tpw, step=2)
        def _(t0):
            for os_ in range(2):
                tok = t0 + os_

                @pl.when(tok >= 2)
                def _():
                    put(tok - 2, os_).wait()

                for slot in range(slots):
                    j = tok * slots + slot
                    gather(j, slot).wait()

                    @pl.when(j + RING_AHEAD < nbuf)
                    def _():
                        gather(j + RING_AHEAD, (slot + RING_AHEAD) % slots).start()

                    _vsum_rows(rows_v.at[slot], wgt_v, tok * k + slot * RING_ROWS, out_v, os_ * d, slot == 0)
                put(tok, os_).start()

        put(tpw - 2, 0).wait()
        put(tpw - 1, 1).wait()

    return pl.kernel(
        body,
        out_type=jax.ShapeDtypeStruct((t, d), F32),
        mesh=mesh,
        scratch_types=[
            pltpu.VMEM((nbuf * RING_ROWS,), I32),
            pltpu.VMEM((slots, RING_ROWS, width), table.dtype),
            pltpu.VMEM((tpw * k,), I32),
            pltpu.VMEM((2 * d,), F32),
            pltpu.SemaphoreType.DMA((slots,)),
            pltpu.SemaphoreType.DMA((2,)),
        ],
        compiler_params=pltpu.CompilerParams(needs_layout_passes=False),
        name="peer_vsum",
    )(table, idx.reshape(SC_WORKERS, nbuf * RING_ROWS), wgt.reshape(SC_WORKERS, tpw * k))


def _wgt_kernel(a_ref, gate_ref, o_ref):
    a = a_ref[...]
    w = (gate_ref[...] * (0.5 * a * (1.0 + lax.erf(a * (2.0 ** -0.5))))).astype(BF16)
    o_ref[...] = _pack_pairs(w, w)


def _expert_weights(a, gate):
    t, k = a.shape
    tm = math.gcd(t, 1024)
    spec = pl.BlockSpec((tm, k), lambda i: (i, 0))
    return pl.pallas_call(
        _wgt_kernel,
        grid=(t // tm,),
        in_specs=[spec, spec],
        out_specs=spec,
        out_shape=jax.ShapeDtypeStruct((t, k), I32),
        compiler_params=pltpu.CompilerParams(dimension_semantics=("arbitrary",), vmem_limit_bytes=VMEM_LIMIT),
        name="peer_weights",
    )(a, gate)


def _final_kernel(x1_ref, ffn_ref, mod_ref, ln_ref, o_ref, *, alpha):
    y = alpha * x1_ref[...] + mod_ref[5:6, :] * ffn_ref[...]
    o_ref[...] = _layer_norm(y, ln_ref[0:1, :], ln_ref[1:2, :])


def _final(x1, ffn, mod3, ln2, alpha, b0):
    nb, s, d = x1.shape
    tm = 512
    row = lambda b, i: (b, i, 0)
    return pl.pallas_call(
        functools.partial(_final_kernel, alpha=alpha),
        grid=(nb, s // tm),
        in_specs=[
            pl.BlockSpec((None, tm, d), row),
            pl.BlockSpec((None, tm, d), row),
            pl.BlockSpec((None, 6, d), lambda b, i: (b0 + b, 0, 0)),
            pl.BlockSpec((2, d), lambda b, i: (0, 0)),
        ],
        out_specs=pl.BlockSpec((None, tm, d), row),
        out_shape=jax.ShapeDtypeStruct((nb, s, d), F32),
        compiler_params=pltpu.CompilerParams(
            dimension_semantics=("arbitrary", "arbitrary"), vmem_limit_bytes=VMEM_LIMIT),
        name="deepnorm_ln2",
    )(x1, ffn, mod3, ln2)


CHUNK_BATCHES = 1
CHUNK_LAG = 6


def _layer_chunk(x, b0, nb, mod3, slopes, lam_vecs, lambda_init, alpha, w):
    _, seq, d = x.shape
    proj = _inproj(x, mod3, w["w_in"], b0, nb)
    diff_out = _diff_attention(proj, slopes, lam_vecs, w["subln_g"], lambda_init)
    swa_out = _swa_attention(proj, slopes, w["sinks"])
    x1, h2, q = _mid(diff_out, swa_out, x, mod3, w["w_out"], w["ln1"], w["w_pq"], alpha, b0)
    idx, gate = _route(q.reshape(nb * seq, -1), w["sub_keys"])
    a = _sc_udot(w["u_pack"], idx, h2.reshape(nb * seq, d // 2))
    ffn = _sc_vsum(w["v_pack"], idx, _expert_weights(a, gate))
    return _final(x1, ffn.reshape(nb, seq, d), mod3, w["ln2"], alpha, b0)


def kernel(x, c, w_ada, b_ada, w_in, lambda_q1, lambda_k1, lambda_q2, lambda_k2, subln_g, sinks, w_out, ln1_g, ln1_b, w_pq, sub_keys, u_tab, v_tab, ln2_g, ln2_b):
    bsz, seq, d = x.shape
    depth = w_ada.shape[0]
    alpha = (2 * depth) ** 0.25
    slopes = jnp.exp2(-8.0 * jnp.arange(1, N_ATT_HEADS + 1, dtype=F32) / N_ATT_HEADS)
    nb = CHUNK_BATCHES
    for l in range(depth):
        lambda_init = 0.8 - 0.6 * math.exp(-0.3 * l)
        mod3 = _mod(c, w_ada[l], b_ada[l]).reshape(bsz, 6, d)
        lam_vecs = jnp.stack([lambda_q1[l], lambda_k1[l], lambda_q2[l], lambda_k2[l]])
        w = dict(w_in=w_in[l].astype(BF16), subln_g=subln_g[l], sinks=sinks[l], w_out=w_out[l].astype(BF16),
                 ln1=jnp.stack([ln1_g[l], ln1_b[l]]), w_pq=w_pq[l].astype(BF16),
                 sub_keys=sub_keys[l].astype(BF16), u_pack=_pack_table(u_tab[l]), v_pack=_pack_table(v_tab[l]),
                 ln2=jnp.stack([ln2_g[l], ln2_b[l]]))
        outs = []
        for ci, b0 in enumerate(range(0, bsz, nb)):
            mod_c = mod3
            if ci >= CHUNK_LAG:
                mod_c, outs[ci - CHUNK_LAG] = lax.optimization_barrier((mod3, outs[ci - CHUNK_LAG]))
            outs.append(_layer_chunk(x, b0, nb, mod_c, slopes, lam_vecs, lambda_init, alpha, w))
        x = jnp.concatenate(outs, axis=0).reshape(bsz, seq, d)
    return x
```

```python
import functools
import math

import jax
import jax.numpy as jnp
from jax import lax
from jax.experimental import pallas as pl
from jax.experimental.pallas import tpu as pltpu
from jax.experimental.pallas import tpu_sc as plsc

F32 = jnp.float32
BF16 = jnp.bfloat16
I32 = jnp.int32

HEAD_DIM = 64
DIFF_HEADS = 4
DIFF_V = 2 * HEAD_DIM
SWA_Q_HEADS = 8
SWA_KV_HEADS = 2
SWA_GROUP = SWA_Q_HEADS // SWA_KV_HEADS
WINDOW = 128
N_ATT_HEADS = SWA_Q_HEADS + DIFF_HEADS
PEER_HEADS = 8
N_KEYS = 128
PEER_HALF = 128
PEER_TOPK = 16
PEER_SLOTS = PEER_HEADS * PEER_TOPK
LN_EPS = 1e-5
NEG_INF = -1e30

LANES = 128
VMEM_LIMIT = 48 * 1024 * 1024
HI_HALF = -65536

ROW_TILE = 512
ATTN_TILE = 256
ROUTE_TILE = 2 * LANES
MOD_COL_TILE = 768

SC_CORES = 2
SC_SUBCORES = 16
SC_WORKERS = SC_CORES * SC_SUBCORES


def _nt_dot(a, b):
    return lax.dot_general(a, b, (((1,), (1,)), ((), ())), preferred_element_type=F32)


def _mod_kernel(c_ref, w_ref, b_ref, o_ref):
    c = c_ref[...]
    s = c * (1.0 / (1.0 + jnp.exp(-c)))
    o_ref[...] = jnp.dot(s.astype(BF16), w_ref[...].astype(BF16), preferred_element_type=F32) + b_ref[...]


def _mod(c, w, b):
    bsz, d = c.shape
    n = w.shape[1]
    tn = MOD_COL_TILE
    return pl.pallas_call(
        _mod_kernel,
        grid=(n // tn,),
        in_specs=[
            pl.BlockSpec((bsz, d), lambda j: (0, 0)),
            pl.BlockSpec((d, tn), lambda j: (0, j)),
            pl.BlockSpec((1, tn), lambda j: (0, j)),
        ],
        out_specs=pl.BlockSpec((bsz, tn), lambda j: (0, j)),
        out_shape=jax.ShapeDtypeStruct((bsz, n), F32),
        compiler_params=pltpu.CompilerParams(dimension_semantics=("arbitrary",), vmem_limit_bytes=VMEM_LIMIT),
        name="adaln_mod",
    )(c, w, b.reshape(1, n))


def _inproj_kernel(x_ref, mod_ref, w_ref, o_ref):
    h = x_ref[...] * (1.0 + mod_ref[1:2, :]) + mod_ref[0:1, :]
    o_ref[...] = jnp.dot(h.astype(BF16), w_ref[...], preferred_element_type=F32).astype(BF16)


def _inproj(x, mod3, w_bf16, b0, bsz):
    _, s, d = x.shape
    n = w_bf16.shape[1]
    tm = ROW_TILE
    return pl.pallas_call(
        _inproj_kernel,
        grid=(bsz, s // tm),
        in_specs=[
            pl.BlockSpec((None, tm, d), lambda b, i: (b0 + b, i, 0)),
            pl.BlockSpec((None, 6, d), lambda b, i: (b0 + b, 0, 0)),
            pl.BlockSpec((d, n), lambda b, i: (0, 0)),
        ],
        out_specs=pl.BlockSpec((None, tm, n), lambda b, i: (b, i, 0)),
        out_shape=jax.ShapeDtypeStruct((bsz, s, n), BF16),
        compiler_params=pltpu.CompilerParams(
            dimension_semantics=("arbitrary", "arbitrary"), vmem_limit_bytes=VMEM_LIMIT),
        name="in_proj",
    )(x, mod3, w_bf16)


def _diff_kernel(slopes_ref, q_ref, k_ref, v_ref, lam_ref, g_ref, o_ref, *, tq, lambda_init):
    h = pl.program_id(1)
    i = pl.program_id(2)
    slope = slopes_ref[SWA_Q_HEADS + h]
    q = q_ref[...] * (HEAD_DIM ** -0.5)
    qs = (q[:, :HEAD_DIM], q[:, HEAD_DIM:])

    def tile(j, carry, diagonal):
        ks = k_ref[pl.ds(pl.multiple_of(j * tq, tq), tq), :]
        vs = v_ref[pl.ds(pl.multiple_of(j * tq, tq), tq), :]
        col_bias = slope * (j * tq + lax.broadcasted_iota(I32, (1, tq), 1)).astype(F32)
        if diagonal:
            valid = lax.broadcasted_iota(I32, (tq, tq), 0) >= lax.broadcasted_iota(I32, (tq, tq), 1)
        new = []
        for m in range(2):
            mx, l, acc = carry[3 * m: 3 * m + 3]
            s = _nt_dot(qs[m], ks[:, m * HEAD_DIM:(m + 1) * HEAD_DIM]) + col_bias
            if diagonal:
                s = jnp.where(valid, s, NEG_INF)
            mx_new = jnp.maximum(mx, jnp.max(s, axis=-1, keepdims=True))
            p = jnp.exp(s - mx_new)
            corr = jnp.exp(mx - mx_new)
            l = l * corr + jnp.sum(p, axis=-1, keepdims=True)
            acc = acc * corr + jnp.dot(p.astype(BF16), vs, preferred_element_type=F32)
            new += [mx_new, l, acc]
        return tuple(new)

    init = []
    for _ in range(2):
        init += [jnp.full((tq, 1), NEG_INF, F32), jnp.zeros((tq, 1), F32), jnp.zeros((tq, DIFF_V), F32)]
    carry = lax.fori_loop(0, i, lambda j, c: tile(j, c, False), tuple(init))
    m0, l0, a0, m1, l1, a1 = tile(i, carry, True)

    lam_v = lam_ref[...]
    lam = (jnp.exp(jnp.sum(lam_v[0:1, :] * lam_v[1:2, :], axis=-1, keepdims=True))
           - jnp.exp(jnp.sum(lam_v[2:3, :] * lam_v[3:4, :], axis=-1, keepdims=True)) + lambda_init)
    o = a0 / l0 - lam * (a1 / l1)
    o = o * lax.rsqrt(jnp.mean(o * o, axis=-1, keepdims=True) + LN_EPS)
    o_ref[...] = (o * g_ref[...] * (1.0 - lambda_init)).astype(BF16)


def _diff_attention(proj, slopes, lam_vecs, subln_g, lambda_init):
    bsz, s, _ = proj.shape
    tq = ATTN_TILE
    kcol = DIFF_HEADS
    vcol = 2 * DIFF_HEADS
    return pl.pallas_call(
        functools.partial(_diff_kernel, tq=tq, lambda_init=lambda_init),
        grid=(bsz, DIFF_HEADS, s // tq),
        in_specs=[
            pl.BlockSpec(memory_space=pltpu.SMEM),
            pl.BlockSpec((None, tq, DIFF_V), lambda b, h, i: (b, i, h)),
            pl.BlockSpec((None, s, DIFF_V), lambda b, h, i: (b, 0, kcol + h)),
            pl.BlockSpec((None, s, DIFF_V), lambda b, h, i: (b, 0, vcol + h)),
            pl.BlockSpec((4, HEAD_DIM), lambda b, h, i: (0, 0)),
            pl.BlockSpec((1, DIFF_V), lambda b, h, i: (0, 0)),
        ],
        out_specs=pl.BlockSpec((None, tq, DIFF_V), lambda b, h, i: (b, i, h)),
        out_shape=jax.ShapeDtypeStruct((bsz, s, DIFF_HEADS * DIFF_V), BF16),
        compiler_params=pltpu.CompilerParams(
            dimension_semantics=("arbitrary", "arbitrary", "arbitrary"), vmem_limit_bytes=VMEM_LIMIT),
        name="diff_attention",
    )(slopes, proj, proj, proj, lam_vecs, subln_g.reshape(1, DIFF_V))


def _swa_kernel(slopes_ref, sinks_ref, q_ref, k_ref, v_ref, o_ref, *, tq):
    i = pl.program_id(1)
    scale = HEAD_DIM ** -0.5
    blk = WINDOW
    ii = lax.broadcasted_iota(I32, (blk, 2 * blk), 0)
    jj = lax.broadcasted_iota(I32, (blk, 2 * blk), 1)
    for r in range(tq // blk):
        start = i * tq + r * blk
        kstart = jnp.maximum(start - blk, 0)
        kb = k_ref[pl.ds(pl.multiple_of(kstart, blk), 2 * blk), :]
        vb = v_ref[pl.ds(pl.multiple_of(kstart, blk), 2 * blk), :]
        dist = (start + ii) - (kstart + jj)
        valid = (dist >= 0) & (dist < WINDOW)
        distf = dist.astype(F32)
        outs = []
        for kvh in range(SWA_KV_HEADS):
            k = kb[:, kvh * HEAD_DIM:(kvh + 1) * HEAD_DIM]
            v = vb[:, kvh * HEAD_DIM:(kvh + 1) * HEAD_DIM]
            for g in range(SWA_GROUP):
                hq = kvh * SWA_GROUP + g
                qh = q_ref[r * blk:(r + 1) * blk, hq * HEAD_DIM:(hq + 1) * HEAD_DIM]
                s = _nt_dot(qh, k) * scale - slopes_ref[hq] * distf
                s = jnp.where(valid, s, NEG_INF)
                sink = sinks_ref[hq]
                m = jnp.maximum(jnp.max(s, axis=-1, keepdims=True), sink)
                p = jnp.exp(s - m)
                denom = jnp.sum(p, axis=-1, keepdims=True) + jnp.exp(sink - m)
                outs.append(jnp.dot(p.astype(BF16), v, preferred_element_type=F32) / denom)
        o_ref[r * blk:(r + 1) * blk, :] = jnp.concatenate(outs, axis=-1).astype(BF16)


def _swa_attention(proj, slopes, sinks):
    bsz, s, _ = proj.shape
    tq = ATTN_TILE
    width = SWA_Q_HEADS * HEAD_DIM
    qcol = (3 * DIFF_HEADS * DIFF_V) // width
    kcol = (3 * DIFF_HEADS * DIFF_V + width) // LANES
    return pl.pallas_call(
        functools.partial(_swa_kernel, tq=tq),
        grid=(bsz, s // tq),
        in_specs=[
            pl.BlockSpec(memory_space=pltpu.SMEM),
            pl.BlockSpec(memory_space=pltpu.SMEM),
            pl.BlockSpec((None, tq, width), lambda b, i: (b, i, qcol)),
            pl.BlockSpec((None, s, LANES), lambda b, i: (b, 0, kcol)),
            pl.BlockSpec((None, s, LANES), lambda b, i: (b, 0, kcol + 1)),
        ],
        out_specs=pl.BlockSpec((None, tq, width), lambda b, i: (b, i, 0)),
        out_shape=jax.ShapeDtypeStruct((bsz, s, width), BF16),
        compiler_params=pltpu.CompilerParams(
            dimension_semantics=("arbitrary", "arbitrary"), vmem_limit_bytes=VMEM_LIMIT),
        name="swa_attention",
    )(slopes, sinks, proj, proj, proj)


def _layer_norm(y, g, b):
    mu = jnp.mean(y, axis=-1, keepdims=True)
    yc = y - mu
    var = jnp.mean(yc * yc, axis=-1, keepdims=True)
    return yc * lax.rsqrt(var + LN_EPS) * g + b


def _pack_pairs(lo, hi):
    lo_bits = lax.bitcast_convert_type(lo.astype(F32), I32)
    hi_bits = lax.bitcast_convert_type(hi.astype(F32), I32)
    return lax.shift_right_logical(lo_bits, 16) | (hi_bits & jnp.int32(HI_HALF))


def _mid_kernel(do_ref, so_ref, x_ref, mod_ref, wo_ref, ln_ref, wpq_ref, x1_ref, h2_ref, q_ref, *, alpha):
    nd = do_ref.shape[-1]
    mixed = (jnp.dot(do_ref[...], wo_ref[:nd, :], preferred_element_type=F32)
             + jnp.dot(so_ref[...], wo_ref[nd:, :], preferred_element_type=F32))
    y = alpha * x_ref[...] + mod_ref[2:3, :] * mixed
    x1 = _layer_norm(y, ln_ref[0:1, :], ln_ref[1:2, :])
    x1_ref[...] = x1
    h2 = (x1 * (1.0 + mod_ref[4:5, :]) + mod_ref[3:4, :]).astype(BF16)
    half = h2.shape[-1] // 2
    h2_ref[...] = _pack_pairs(h2[:, :half], h2[:, half:])
    q_ref[...] = jnp.dot(h2, wpq_ref[...], preferred_element_type=F32).astype(BF16)


def _mid(diff_out, swa_out, x, mod3, wo_bf16, ln1, wpq_bf16, alpha, b0):
    bsz = diff_out.shape[0]
    _, s, d = x.shape
    nq = wpq_bf16.shape[1]
    tm = ROW_TILE
    row = lambda b, i: (b, i, 0)
    const = lambda b, i: (0, 0)
    return pl.pallas_call(
        functools.partial(_mid_kernel, alpha=alpha),
        grid=(bsz, s // tm),
        in_specs=[
            pl.BlockSpec((None, tm, diff_out.shape[-1]), row),
            pl.BlockSpec((None, tm, swa_out.shape[-1]), row),
            pl.BlockSpec((None, tm, d), lambda b, i: (b0 + b, i, 0)),
            pl.BlockSpec((None, 6, d), lambda b, i: (b0 + b, 0, 0)),
            pl.BlockSpec(wo_bf16.shape, const),
            pl.BlockSpec((2, d), const),
            pl.BlockSpec(wpq_bf16.shape, const),
        ],
        out_specs=[
            pl.BlockSpec((None, tm, d), row),
            pl.BlockSpec((None, tm, d // 2), row),
            pl.BlockSpec((None, tm, nq), row),
        ],
        out_shape=[
            jax.ShapeDtypeStruct((bsz, s, d), F32),
            jax.ShapeDtypeStruct((bsz, s, d // 2), I32),
            jax.ShapeDtypeStruct((bsz, s, nq), BF16),
        ],
        compiler_params=pltpu.CompilerParams(
            dimension_semantics=("arbitrary", "arbitrary"), vmem_limit_bytes=VMEM_LIMIT),
        name="outproj_ln1_peerq",
    )(diff_out, swa_out, x, mod3, wo_bf16, ln1, wpq_bf16)


def _topk_rows(vals, pos, payload, k):
    out_v, out_p = [], []
    for _ in range(k):
        m = jnp.max(vals, axis=0, keepdims=True)
        first = jnp.min(jnp.where(vals == m, pos, jnp.inf), axis=0, keepdims=True)
        sel = pos == first
        if payload is None:
            out_p.append(first)
        else:
            out_p.append(jnp.max(jnp.where(sel, payload, -1.0), axis=0, keepdims=True))
        out_v.append(m)
        vals = jnp.where(sel, -jnp.inf, vals)
    return jnp.concatenate(out_v, axis=0), jnp.concatenate(out_p, axis=0)


def _candidates(v0, i0, v1, i1):
    k, lanes = v0.shape
    vals, poss, eids = [], [], []
    for a in range(4):
        nb = k if a == 0 else k // 2
        b_iota = lax.broadcasted_iota(I32, (nb, lanes), 0).astype(F32)
        vals.append(v0[a:a + 1, :] + v1[:nb, :])
        poss.append(a * k + b_iota)
        eids.append(i0[a:a + 1, :] * N_KEYS + i1[:nb, :])
    for b in range(3):
        na = k if b == 0 else k // 2
        a_iota = lax.broadcasted_iota(I32, (na, lanes), 0).astype(F32)
        vals.append(jnp.where(a_iota >= 4.0, v0[:na, :] + v1[b:b + 1, :], -jnp.inf))
        poss.append(a_iota * k + b)
        eids.append(i0[:na, :] * N_KEYS + i1[b:b + 1, :])
    return jnp.concatenate(vals, axis=0), jnp.concatenate(poss, axis=0), jnp.concatenate(eids, axis=0)


def _route_kernel(q_ref, keys_ref, idx_ref, gate_ref, idx_t, gate_t):
    tt = q_ref.shape[0]
    key_pos = lax.broadcasted_iota(I32, (N_KEYS, tt), 0).astype(F32)

    def head(h, carry):
        halves = []
        for p in range(2):
            qh = q_ref[:, pl.ds(pl.multiple_of((2 * h + p) * PEER_HALF, PEER_HALF), PEER_HALF)]
            sc = _nt_dot(keys_ref[h, p], qh)
            halves.append(_topk_rows(sc, key_pos, None, PEER_TOPK))
        (v0, i0), (v1, i1) = halves
        cv, cp, ce = _candidates(v0, i0, v1, i1)
        top_s, top_e = _topk_rows(cv, cp, ce, PEER_TOPK)
        e = jnp.exp(top_s - top_s[0:1, :])
        gate = e / jnp.sum(e, axis=0, keepdims=True)
        rows = pl.ds(pl.multiple_of(h * PEER_TOPK, PEER_TOPK), PEER_TOPK)
        idx_t[rows, :] = top_e
        gate_t[rows, :] = gate
        return carry

    lax.fori_loop(0, PEER_HEADS, head, 0)
    idx_ref[...] = idx_t[...].T.astype(I32)
    gate_ref[...] = gate_t[...].T


def _route(q2d, keys_bf16):
    t, nq = q2d.shape
    tt = ROUTE_TILE
    return pl.pallas_call(
        _route_kernel,
        grid=(t // tt,),
        in_specs=[
            pl.BlockSpec((tt, nq), lambda i: (i, 0)),
            pl.BlockSpec(keys_bf16.shape, lambda i: (0, 0, 0, 0)),
        ],
        out_specs=[
            pl.BlockSpec((tt, PEER_SLOTS), lambda i: (i, 0)),
            pl.BlockSpec((tt, PEER_SLOTS), lambda i: (i, 0)),
        ],
        out_shape=[
            jax.ShapeDtypeStruct((t, PEER_SLOTS), I32),
            jax.ShapeDtypeStruct((t, PEER_SLOTS), F32),
        ],
        scratch_shapes=[pltpu.VMEM((PEER_SLOTS, tt), F32), pltpu.VMEM((PEER_SLOTS, tt), F32)],
        compiler_params=pltpu.CompilerParams(dimension_semantics=("arbitrary",), vmem_limit_bytes=VMEM_LIMIT),
        name="peer_route",
    )(q2d, keys_bf16)


def _pack_table(tab):
    half = tab.shape[1] // 2
    bits = lax.bitcast_convert_type(tab.astype(BF16), jnp.uint16).astype(jnp.uint32)
    return lax.bitcast_convert_type(bits[:, :half] | (bits[:, half:] << 16), I32)


SC_LANES = 16
RING_ROWS = 32
RING_AHEAD = 3
UDOT_GROUP = 16


BF16_GROUP = 4


def _widen_pair_sum(acc, packed_bf16):
    bits = plsc.bitcast(packed_bf16, I32)
    lo = lax.bitcast_convert_type(bits << 16, F32)
    hi = lax.bitcast_convert_type(bits & jnp.int32(HI_HALF), F32)
    return acc + lo + hi


def _udot_rows(rows_ref, h_ref, out_ref, out_base):
    nrows, width = rows_ref.shape
    span = BF16_GROUP * SC_LANES
    lane = lax.iota(I32, SC_LANES)

    @pl.loop(0, nrows // UDOT_GROUP)
    def _(g):
        r0 = g * UDOT_GROUP

        def chunk(c, accs):
            off = pl.multiple_of(c * span, span)
            hs = [plsc.bitcast(h_ref[pl.ds(off + i * SC_LANES, SC_LANES)], BF16) for i in range(BF16_GROUP)]
            out = []
            for r in range(UDOT_GROUP):
                part = None
                for i in range(BF16_GROUP):
                    w = plsc.bitcast(rows_ref[r0 + r, pl.ds(off + i * SC_LANES, SC_LANES)], BF16)
                    part = w * hs[i] if part is None else part + w * hs[i]
                out.append(_widen_pair_sum(accs[r], part))
            return tuple(out)

        zero = jnp.zeros((SC_LANES,), F32)
        accs = lax.fori_loop(0, width // span, chunk, (zero,) * UDOT_GROUP)
        vec = zero
        for r in range(UDOT_GROUP):
            vec = jnp.where(lane == r, jnp.sum(accs[r]), vec)
        out_ref[pl.ds(pl.multiple_of(out_base + r0, UDOT_GROUP), UDOT_GROUP)] = vec


def _sc_udot(table, idx, h):
    t, k = idx.shape
    width = table.shape[1]
    tpw = t // SC_WORKERS
    slots = k // RING_ROWS
    assert tpw * SC_WORKERS == t and tpw % 2 == 0 and slots == RING_AHEAD + 1
    nbuf = tpw * slots
    mesh = plsc.VectorSubcoreMesh(core_axis_name="c", subcore_axis_name="s")

    def body(table_hbm, idx_hbm, h_hbm, out_hbm, idx_v, rows_v, h_v, a_v, gsem, hsem):
        wid = lax.axis_index("s") * SC_CORES + lax.axis_index("c")
        tok0 = wid * tpw
        pltpu.sync_copy(idx_hbm.at[wid], idx_v)

        def h_slot(slot):
            return h_v.at[pl.ds(slot * width, width)]

        def gather(j, slot):
            rows = idx_v.at[pl.ds(pl.multiple_of(j * RING_ROWS, RING_ROWS), RING_ROWS)]
            return pltpu.make_async_copy(table_hbm.at[rows], rows_v.at[slot], gsem.at[slot])

        def h_copy(tok, slot):
            return pltpu.make_async_copy(h_hbm.at[tok0 + tok], h_slot(slot), hsem.at[slot])

        for j in range(RING_AHEAD):
            gather(j, j).start()
        h_copy(0, 0).start()

        @pl.loop(0, tpw, step=2)
        def _(t0):
            for hs in range(2):
                tok = t0 + hs
                h_copy(tok, hs).wait()

                @pl.when(tok + 1 < tpw)
                def _():
                    h_copy(tok + 1, 1 - hs).start()

                for slot in range(slots):
                    j = tok * slots + slot
                    gather(j, slot).wait()

                    @pl.when(j + RING_AHEAD < nbuf)
                    def _():
                        gather(j + RING_AHEAD, (slot + RING_AHEAD) % slots).start()

                    _udot_rows(rows_v.at[slot], h_slot(hs), a_v, tok * k + slot * RING_ROWS)

        pltpu.sync_copy(a_v, out_hbm.at[pl.ds(pl.multiple_of(tok0 * k, 8), tpw * k)])

    out = pl.kernel(
        body,
        out_type=jax.ShapeDtypeStruct((t * k,), F32),
        mesh=mesh,
        scratch_types=[
            pltpu.VMEM((nbuf * RING_ROWS,), I32),
            pltpu.VMEM((slots, RING_ROWS, width), table.dtype),
            pltpu.VMEM((2 * width,), I32),
            pltpu.VMEM((tpw * k,), F32),
            pltpu.SemaphoreType.DMA((slots,)),
            pltpu.SemaphoreType.DMA((2,)),
        ],
        compiler_params=pltpu.CompilerParams(needs_layout_passes=False),
        name="peer_udot",
    )(table, idx.reshape(SC_WORKERS, nbuf * RING_ROWS), h)
    return out.reshape(t, k)


VSUM_CHUNKS = 8


def _vsum_rows(rows_ref, wgt_ref, wgt_base, out_ref, out_base, first):
    nrows, width = rows_ref.shape
    span = VSUM_CHUNKS * SC_LANES

    @pl.loop(0, width // span)
    def _(blk):
        col0 = pl.multiple_of(blk * span, span)

        def row_group(rg, accs):
            r0 = rg * BF16_GROUP
            parts = [None] * VSUM_CHUNKS
            wvec = wgt_ref[pl.ds(pl.multiple_of(wgt_base + (r0 // SC_LANES) * SC_LANES, SC_LANES), SC_LANES)]
            for i in range(BF16_GROUP):
                lane = jnp.full((SC_LANES,), r0 % SC_LANES + i, I32)
                wv = plsc.bitcast(wvec.at[lane].get(mode="promise_in_bounds"), BF16)
                for c in range(VSUM_CHUNKS):
                    w = plsc.bitcast(rows_ref[r0 + i, pl.ds(col0 + c * SC_LANES, SC_LANES)], BF16)
                    parts[c] = w * wv if parts[c] is None else parts[c] + w * wv
            out = []
            for c in range(VSUM_CHUNKS):
                bits = plsc.bitcast(parts[c], I32)
                out.append(accs[2 * c] + lax.bitcast_convert_type(bits << 16, F32))
                out.append(accs[2 * c + 1] + lax.bitcast_convert_type(bits & jnp.int32(HI_HALF), F32))
            return tuple(out)

        zero = jnp.zeros((SC_LANES,), F32)
        accs = lax.fori_loop(0, nrows // BF16_GROUP, row_group, (zero,) * (2 * VSUM_CHUNKS))
        for c in range(VSUM_CHUNKS):
            for half in range(2):
                dst = pl.ds(pl.multiple_of(out_base + half * width + col0 + c * SC_LANES, SC_LANES), SC_LANES)
                if first:
                    out_ref[dst] = accs[2 * c + half]
                else:
                    out_ref[dst] = out_ref[dst] + accs[2 * c + half]


def _sc_vsum(table, idx, wgt):
    t, k = idx.shape
    width = table.shape[1]
    d = 2 * width
    tpw = t // SC_WORKERS
    slots = k // RING_ROWS
    assert tpw * SC_WORKERS == t and tpw % 2 == 0 and slots == RING_AHEAD + 1
    nbuf = tpw * slots
    mesh = plsc.VectorSubcoreMesh(core_axis_name="c", subcore_axis_name="s")

    def body(table_hbm, idx_hbm, wgt_hbm, out_hbm, idx_v, rows_v, wgt_v, out_v, gsem, osem):
        wid = lax.axis_index("s") * SC_CORES + lax.axis_index("c")
        tok0 = wid * tpw
        pltpu.sync_copy(idx_hbm.at[wid], idx_v)
        pltpu.sync_copy(wgt_hbm.at[wid], wgt_v)

        def gather(j, slot):
            rows = idx_v.at[pl.ds(pl.multiple_of(j * RING_ROWS, RING_ROWS), RING_ROWS)]
            return pltpu.make_async_copy(table_hbm.at[rows], rows_v.at[slot], gsem.at[slot])

        def put(tok, slot):
            return pltpu.make_async_copy(out_v.at[pl.ds(slot * d, d)], out_hbm.at[tok0 + tok], osem.at[slot])

        for j in range(RING_AHEAD):
            gather(j, j).start()

        @pl.loop(0, tpw, step=2)
        def _(t0):
            for os_ in range(2):
                tok = t0 + os_

                @pl.when(tok >= 2)
                def _():
                    put(tok - 2, os_).wait()

                for slot in range(slots):
                    j = tok * slots + slot
                    gather(j, slot).wait()

                    @pl.when(j + RING_AHEAD < nbuf)
                    def _():
                        gather(j + RING_AHEAD, (slot + RING_AHEAD) % slots).start()

                    _vsum_rows(rows_v.at[slot], wgt_v, tok * k + slot * RING_ROWS, out_v, os_ * d, slot == 0)
                put(tok, os_).start()

        put(tpw - 2, 0).wait()
        put(tpw - 1, 1).wait()

    return pl.kernel(
        body,
        out_type=jax.ShapeDtypeStruct((t, d), F32),
        mesh=mesh,
        scratch_types=[
            pltpu.VMEM((nbuf * RING_ROWS,), I32),
            pltpu.VMEM((slots, RING_ROWS, width), table.dtype),
            pltpu.VMEM((tpw * k,), I32),
            pltpu.VMEM((2 * d,), F32),
            pltpu.SemaphoreType.DMA((slots,)),
            pltpu.SemaphoreType.DMA((2,)),
        ],
        compiler_params=pltpu.CompilerParams(needs_layout_passes=False),
        name="peer_vsum",
    )(table, idx.reshape(SC_WORKERS, nbuf * RING_ROWS), wgt.reshape(SC_WORKERS, tpw * k))


def _wgt_kernel(a_ref, gate_ref, o_ref):
    a = a_ref[...]
    w = (gate_ref[...] * (0.5 * a * (1.0 + lax.erf(a * (2.0 ** -0.5))))).astype(BF16)
    o_ref[...] = _pack_pairs(w, w)


def _expert_weights(a, gate):
    t, k = a.shape
    tm = math.gcd(t, 1024)
    spec = pl.BlockSpec((tm, k), lambda i: (i, 0))
    return pl.pallas_call(
        _wgt_kernel,
        grid=(t // tm,),
        in_specs=[spec, spec],
        out_specs=spec,
        out_shape=jax.ShapeDtypeStruct((t, k), I32),
        compiler_params=pltpu.CompilerParams(dimension_semantics=("arbitrary",), vmem_limit_bytes=VMEM_LIMIT),
        name="peer_weights",
    )(a, gate)


def _final_kernel(x1_ref, ffn_ref, mod_ref, ln_ref, o_ref, *, alpha):
    y = alpha * x1_ref[...] + mod_ref[5:6, :] * ffn_ref[...]
    o_ref[...] = _layer_norm(y, ln_ref[0:1, :], ln_ref[1:2, :])


def _final(x1, ffn, mod3, ln2, alpha, b0):
    nb, s, d = x1.shape
    tm = ROW_TILE
    row = lambda b, i: (b, i, 0)
    return pl.pallas_call(
        functools.partial(_final_kernel, alpha=alpha),
        grid=(nb, s // tm),
        in_specs=[
            pl.BlockSpec((None, tm, d), row),
            pl.BlockSpec((None, tm, d), row),
            pl.BlockSpec((None, 6, d), lambda b, i: (b0 + b, 0, 0)),
            pl.BlockSpec((2, d), lambda b, i: (0, 0)),
        ],
        out_specs=pl.BlockSpec((None, tm, d), row),
        out_shape=jax.ShapeDtypeStruct((nb, s, d), F32),
        compiler_params=pltpu.CompilerParams(
            dimension_semantics=("arbitrary", "arbitrary"), vmem_limit_bytes=VMEM_LIMIT),
        name="deepnorm_ln2",
    )(x1, ffn, mod3, ln2)


CHUNK_BATCHES = 1
CHUNK_LAG = 6
STAGE_LAG = 2


def _layer_chunk(x, b0, nb, mod3, slopes, lam_vecs, lambda_init, alpha, w, prev_wgt):
    _, seq, d = x.shape
    proj = _inproj(x, mod3, w["w_in"], b0, nb)
    diff_out = _diff_attention(proj, slopes, lam_vecs, w["subln_g"], lambda_init)
    swa_out = _swa_attention(proj, slopes, w["sinks"])
    x1, h2, q = _mid(diff_out, swa_out, x, mod3, w["w_out"], w["ln1"], w["w_pq"], alpha, b0)
    idx, gate = _route(q.reshape(nb * seq, -1), w["sub_keys"])
    if prev_wgt is not None:
        idx, _ = lax.optimization_barrier((idx, prev_wgt))
    a = _sc_udot(w["u_pack"], idx, h2.reshape(nb * seq, d // 2))
    wgt = _expert_weights(a, gate)
    ffn = _sc_vsum(w["v_pack"], idx, wgt)
    return _final(x1, ffn.reshape(nb, seq, d), mod3, w["ln2"], alpha, b0), wgt


def kernel(x, c, w_ada, b_ada, w_in, lambda_q1, lambda_k1, lambda_q2, lambda_k2, subln_g, sinks, w_out, ln1_g, ln1_b, w_pq, sub_keys, u_tab, v_tab, ln2_g, ln2_b):
    bsz, seq, d = x.shape
    depth = w_ada.shape[0]
    alpha = (2 * depth) ** 0.25
    slopes = jnp.exp2(-8.0 * jnp.arange(1, N_ATT_HEADS + 1, dtype=F32) / N_ATT_HEADS)
    nb = CHUNK_BATCHES
    for l in range(depth):
        lambda_init = 0.8 - 0.6 * math.exp(-0.3 * l)
        mod3 = _mod(c, w_ada[l], b_ada[l]).reshape(bsz, 6, d)
        lam_vecs = jnp.stack([lambda_q1[l], lambda_k1[l], lambda_q2[l], lambda_k2[l]])
        w = dict(w_in=w_in[l].astype(BF16), subln_g=subln_g[l], sinks=sinks[l], w_out=w_out[l].astype(BF16),
                 ln1=jnp.stack([ln1_g[l], ln1_b[l]]), w_pq=w_pq[l].astype(BF16),
                 sub_keys=sub_keys[l].astype(BF16), u_pack=_pack_table(u_tab[l]), v_pack=_pack_table(v_tab[l]),
                 ln2=jnp.stack([ln2_g[l], ln2_b[l]]))
        outs, wgts = [], []
        for ci, b0 in enumerate(range(0, bsz, nb)):
            mod_c = mod3
            if ci >= CHUNK_LAG:
                mod_c, outs[ci - CHUNK_LAG] = lax.optimization_barrier((mod3, outs[ci - CHUNK_LAG]))
            prev_wgt = wgts[ci - STAGE_LAG] if ci >= STAGE_LAG else None
            out, wgt = _layer_chunk(x, b0, nb, mod_c, slopes, lam_vecs, lambda_init, alpha, w, prev_wgt)
            outs.append(out)
            wgts.append(wgt)
        x = jnp.concatenate(outs, axis=0).reshape(bsz, seq, d)
    return x
```

```python
import functools
import math

import jax
import jax.numpy as jnp
from jax import lax
from jax.experimental import pallas as pl
from jax.experimental.pallas import tpu as pltpu
from jax.experimental.pallas import tpu_sc as plsc

F32 = jnp.float32
BF16 = jnp.bfloat16
I32 = jnp.int32

HEAD_DIM = 64
DIFF_HEADS = 4
DIFF_V = 2 * HEAD_DIM
SWA_Q_HEADS = 8
SWA_KV_HEADS = 2
SWA_GROUP = SWA_Q_HEADS // SWA_KV_HEADS
WINDOW = 128
N_ATT_HEADS = SWA_Q_HEADS + DIFF_HEADS
PEER_HEADS = 8
N_KEYS = 128
PEER_HALF = 128
PEER_TOPK = 16
PEER_SLOTS = PEER_HEADS * PEER_TOPK
LN_EPS = 1e-5
NEG_INF = -1e30

LANES = 128
VMEM_LIMIT = 48 * 1024 * 1024
HI_HALF = -65536

ROW_TILE = 512
ATTN_TILE = 512
ROUTE_TILE = 2 * LANES
MOD_COL_TILE = 768

SC_CORES = 2
SC_SUBCORES = 16
SC_WORKERS = SC_CORES * SC_SUBCORES


def _nt_dot(a, b):
    return lax.dot_general(a, b, (((1,), (1,)), ((), ())), preferred_element_type=F32)


def _mod_kernel(c_ref, w_ref, b_ref, o_ref):
    c = c_ref[...]
    s = c * (1.0 / (1.0 + jnp.exp(-c)))
    o_ref[...] = jnp.dot(s.astype(BF16), w_ref[...].astype(BF16), preferred_element_type=F32) + b_ref[...]


def _mod(c, w, b):
    bsz, d = c.shape
    n = w.shape[1]
    tn = MOD_COL_TILE
    return pl.pallas_call(
        _mod_kernel,
        grid=(n // tn,),
        in_specs=[
            pl.BlockSpec((bsz, d), lambda j: (0, 0)),
            pl.BlockSpec((d, tn), lambda j: (0, j)),
            pl.BlockSpec((1, tn), lambda j: (0, j)),
        ],
        out_specs=pl.BlockSpec((bsz, tn), lambda j: (0, j)),
        out_shape=jax.ShapeDtypeStruct((bsz, n), F32),
        compiler_params=pltpu.CompilerParams(dimension_semantics=("arbitrary",), vmem_limit_bytes=VMEM_LIMIT),
        name="adaln_mod",
    )(c, w, b.reshape(1, n))


def _inproj_kernel(x_ref, mod_ref, w_ref, o_ref):
    h = x_ref[...] * (1.0 + mod_ref[1:2, :]) + mod_ref[0:1, :]
    o_ref[...] = jnp.dot(h.astype(BF16), w_ref[...], preferred_element_type=F32).astype(BF16)


def _inproj(x, mod3, w_bf16, b0, bsz):
    _, s, d = x.shape
    n = w_bf16.shape[1]
    tm = ROW_TILE
    return pl.pallas_call(
        _inproj_kernel,
        grid=(bsz, s // tm),
        in_specs=[
            pl.BlockSpec((None, tm, d), lambda b, i: (b0 + b, i, 0)),
            pl.BlockSpec((None, 6, d), lambda b, i: (b0 + b, 0, 0)),
            pl.BlockSpec((d, n), lambda b, i: (0, 0)),
        ],
        out_specs=pl.BlockSpec((None, tm, n), lambda b, i: (b, i, 0)),
        out_shape=jax.ShapeDtypeStruct((bsz, s, n), BF16),
        compiler_params=pltpu.CompilerParams(
            dimension_semantics=("arbitrary", "arbitrary"), vmem_limit_bytes=VMEM_LIMIT),
        name="in_proj",
    )(x, mod3, w_bf16)


def _diff_kernel(slopes_ref, q_ref, k_ref, v_ref, lam_ref, g_ref, o_ref, *, tq, lambda_init):
    h = pl.program_id(1)
    i = pl.program_id(2)
    slope = slopes_ref[SWA_Q_HEADS + h]
    q = q_ref[...] * (HEAD_DIM ** -0.5)
    qs = (q[:, :HEAD_DIM], q[:, HEAD_DIM:])

    def tile(j, carry, diagonal):
        ks = k_ref[pl.ds(pl.multiple_of(j * tq, tq), tq), :]
        vs = v_ref[pl.ds(pl.multiple_of(j * tq, tq), tq), :]
        col_bias = slope * (j * tq + lax.broadcasted_iota(I32, (1, tq), 1)).astype(F32)
        if diagonal:
            valid = lax.broadcasted_iota(I32, (tq, tq), 0) >= lax.broadcasted_iota(I32, (tq, tq), 1)
        new = []
        for m in range(2):
            mx, l, acc = carry[3 * m: 3 * m + 3]
            s = _nt_dot(qs[m], ks[:, m * HEAD_DIM:(m + 1) * HEAD_DIM]) + col_bias
            if diagonal:
                s = jnp.where(valid, s, NEG_INF)
            mx_new = jnp.maximum(mx, jnp.max(s, axis=-1, keepdims=True))
            p = jnp.exp(s - mx_new)
            corr = jnp.exp(mx - mx_new)
            l = l * corr + jnp.sum(p, axis=-1, keepdims=True)
            acc = acc * corr + jnp.dot(p.astype(BF16), vs, preferred_element_type=F32)
            new += [mx_new, l, acc]
        return tuple(new)

    init = []
    for _ in range(2):
        init += [jnp.full((tq, 1), NEG_INF, F32), jnp.zeros((tq, 1), F32), jnp.zeros((tq, DIFF_V), F32)]
    carry = lax.fori_loop(0, i, lambda j, c: tile(j, c, False), tuple(init))
    m0, l0, a0, m1, l1, a1 = tile(i, carry, True)

    lam_v = lam_ref[...]
    lam = (jnp.exp(jnp.sum(lam_v[0:1, :] * lam_v[1:2, :], axis=-1, keepdims=True))
           - jnp.exp(jnp.sum(lam_v[2:3, :] * lam_v[3:4, :], axis=-1, keepdims=True)) + lambda_init)
    o = a0 / l0 - lam * (a1 / l1)
    o = o * lax.rsqrt(jnp.mean(o * o, axis=-1, keepdims=True) + LN_EPS)
    o_ref[...] = (o * g_ref[...] * (1.0 - lambda_init)).astype(BF16)


def _diff_attention(proj, slopes, lam_vecs, subln_g, lambda_init):
    bsz, s, _ = proj.shape
    tq = ATTN_TILE
    kcol = DIFF_HEADS
    vcol = 2 * DIFF_HEADS
    return pl.pallas_call(
        functools.partial(_diff_kernel, tq=tq, lambda_init=lambda_init),
        grid=(bsz, DIFF_HEADS, s // tq),
        in_specs=[
            pl.BlockSpec(memory_space=pltpu.SMEM),
            pl.BlockSpec((None, tq, DIFF_V), lambda b, h, i: (b, i, h)),
            pl.BlockSpec((None, s, DIFF_V), lambda b, h, i: (b, 0, kcol + h)),
            pl.BlockSpec((None, s, DIFF_V), lambda b, h, i: (b, 0, vcol + h)),
            pl.BlockSpec((4, HEAD_DIM), lambda b, h, i: (0, 0)),
            pl.BlockSpec((1, DIFF_V), lambda b, h, i: (0, 0)),
        ],
        out_specs=pl.BlockSpec((None, tq, DIFF_V), lambda b, h, i: (b, i, h)),
        out_shape=jax.ShapeDtypeStruct((bsz, s, DIFF_HEADS * DIFF_V), BF16),
        compiler_params=pltpu.CompilerParams(
            dimension_semantics=("arbitrary", "arbitrary", "arbitrary"), vmem_limit_bytes=VMEM_LIMIT),
        name="diff_attention",
    )(slopes, proj, proj, proj, lam_vecs, subln_g.reshape(1, DIFF_V))


def _swa_kernel(slopes_ref, sinks_ref, q_ref, k_ref, v_ref, o_ref, *, tq):
    i = pl.program_id(1)
    scale = HEAD_DIM ** -0.5
    blk = WINDOW
    ii = lax.broadcasted_iota(I32, (blk, 2 * blk), 0)
    jj = lax.broadcasted_iota(I32, (blk, 2 * blk), 1)
    for r in range(tq // blk):
        start = i * tq + r * blk
        kstart = jnp.maximum(start - blk, 0)
        kb = k_ref[pl.ds(pl.multiple_of(kstart, blk), 2 * blk), :]
        vb = v_ref[pl.ds(pl.multiple_of(kstart, blk), 2 * blk), :]
        dist = (start + ii) - (kstart + jj)
        valid = (dist >= 0) & (dist < WINDOW)
        distf = dist.astype(F32)
        outs = []
        for kvh in range(SWA_KV_HEADS):
            k = kb[:, kvh * HEAD_DIM:(kvh + 1) * HEAD_DIM]
            v = vb[:, kvh * HEAD_DIM:(kvh + 1) * HEAD_DIM]
            for g in range(SWA_GROUP):
                hq = kvh * SWA_GROUP + g
                qh = q_ref[r * blk:(r + 1) * blk, hq * HEAD_DIM:(hq + 1) * HEAD_DIM]
                s = _nt_dot(qh, k) * scale - slopes_ref[hq] * distf
                s = jnp.where(valid, s, NEG_INF)
                sink = sinks_ref[hq]
                m = jnp.maximum(jnp.max(s, axis=-1, keepdims=True), sink)
                p = jnp.exp(s - m)
                denom = jnp.sum(p, axis=-1, keepdims=True) + jnp.exp(sink - m)
                outs.append(jnp.dot(p.astype(BF16), v, preferred_element_type=F32) / denom)
        o_ref[r * blk:(r + 1) * blk, :] = jnp.concatenate(outs, axis=-1).astype(BF16)


def _swa_attention(proj, slopes, sinks):
    bsz, s, _ = proj.shape
    tq = ATTN_TILE
    width = SWA_Q_HEADS * HEAD_DIM
    qcol = (3 * DIFF_HEADS * DIFF_V) // width
    kcol = (3 * DIFF_HEADS * DIFF_V + width) // LANES
    return pl.pallas_call(
        functools.partial(_swa_kernel, tq=tq),
        grid=(bsz, s // tq),
        in_specs=[
            pl.BlockSpec(memory_space=pltpu.SMEM),
            pl.BlockSpec(memory_space=pltpu.SMEM),
            pl.BlockSpec((None, tq, width), lambda b, i: (b, i, qcol)),
            pl.BlockSpec((None, s, LANES), lambda b, i: (b, 0, kcol)),
            pl.BlockSpec((None, s, LANES), lambda b, i: (b, 0, kcol + 1)),
        ],
        out_specs=pl.BlockSpec((None, tq, width), lambda b, i: (b, i, 0)),
        out_shape=jax.ShapeDtypeStruct((bsz, s, width), BF16),
        compiler_params=pltpu.CompilerParams(
            dimension_semantics=("arbitrary", "arbitrary"), vmem_limit_bytes=VMEM_LIMIT),
        name="swa_attention",
    )(slopes, sinks, proj, proj, proj)


def _layer_norm(y, g, b):
    mu = jnp.mean(y, axis=-1, keepdims=True)
    yc = y - mu
    var = jnp.mean(yc * yc, axis=-1, keepdims=True)
    return yc * lax.rsqrt(var + LN_EPS) * g + b


def _pack_pairs(lo, hi):
    lo_bits = lax.bitcast_convert_type(lo.astype(F32), I32)
    hi_bits = lax.bitcast_convert_type(hi.astype(F32), I32)
    return lax.shift_right_logical(lo_bits, 16) | (hi_bits & jnp.int32(HI_HALF))


def _mid_kernel(do_ref, so_ref, x_ref, mod_ref, wo_ref, ln_ref, wpq_ref, x1_ref, h2_ref, q_ref, *, alpha):
    nd = do_ref.shape[-1]
    mixed = (jnp.dot(do_ref[...], wo_ref[:nd, :], preferred_element_type=F32)
             + jnp.dot(so_ref[...], wo_ref[nd:, :], preferred_element_type=F32))
    y = alpha * x_ref[...] + mod_ref[2:3, :] * mixed
    x1 = _layer_norm(y, ln_ref[0:1, :], ln_ref[1:2, :])
    x1_ref[...] = x1
    h2 = (x1 * (1.0 + mod_ref[4:5, :]) + mod_ref[3:4, :]).astype(BF16)
    half = h2.shape[-1] // 2
    h2_ref[...] = _pack_pairs(h2[:, :half], h2[:, half:])
    q_ref[...] = jnp.dot(h2, wpq_ref[...], preferred_element_type=F32).astype(BF16)


def _mid(diff_out, swa_out, x, mod3, wo_bf16, ln1, wpq_bf16, alpha, b0):
    bsz = diff_out.shape[0]
    _, s, d = x.shape
    nq = wpq_bf16.shape[1]
    tm = ROW_TILE
    row = lambda b, i: (b, i, 0)
    const = lambda b, i: (0, 0)
    return pl.pallas_call(
        functools.partial(_mid_kernel, alpha=alpha),
        grid=(bsz, s // tm),
        in_specs=[
            pl.BlockSpec((None, tm, diff_out.shape[-1]), row),
            pl.BlockSpec((None, tm, swa_out.shape[-1]), row),
            pl.BlockSpec((None, tm, d), lambda b, i: (b0 + b, i, 0)),
            pl.BlockSpec((None, 6, d), lambda b, i: (b0 + b, 0, 0)),
            pl.BlockSpec(wo_bf16.shape, const),
            pl.BlockSpec((2, d), const),
            pl.BlockSpec(wpq_bf16.shape, const),
        ],
        out_specs=[
            pl.BlockSpec((None, tm, d), row),
            pl.BlockSpec((None, tm, d // 2), row),
            pl.BlockSpec((None, tm, nq), row),
        ],
        out_shape=[
            jax.ShapeDtypeStruct((bsz, s, d), F32),
            jax.ShapeDtypeStruct((bsz, s, d // 2), I32),
            jax.ShapeDtypeStruct((bsz, s, nq), BF16),
        ],
        compiler_params=pltpu.CompilerParams(
            dimension_semantics=("arbitrary", "arbitrary"), vmem_limit_bytes=VMEM_LIMIT),
        name="outproj_ln1_peerq",
    )(diff_out, swa_out, x, mod3, wo_bf16, ln1, wpq_bf16)


def _topk_rows(vals, pos, payload, k):
    out_v, out_p = [], []
    for _ in range(k):
        m = jnp.max(vals, axis=0, keepdims=True)
        first = jnp.min(jnp.where(vals == m, pos, jnp.inf), axis=0, keepdims=True)
        sel = pos == first
        if payload is None:
            out_p.append(first)
        else:
            out_p.append(jnp.max(jnp.where(sel, payload, -1.0), axis=0, keepdims=True))
        out_v.append(m)
        vals = jnp.where(sel, -jnp.inf, vals)
    return jnp.concatenate(out_v, axis=0), jnp.concatenate(out_p, axis=0)


def _candidates(v0, i0, v1, i1):
    k, lanes = v0.shape
    vals, poss, eids = [], [], []
    for a in range(4):
        nb = k if a == 0 else k // 2
        b_iota = lax.broadcasted_iota(I32, (nb, lanes), 0).astype(F32)
        vals.append(v0[a:a + 1, :] + v1[:nb, :])
        poss.append(a * k + b_iota)
        eids.append(i0[a:a + 1, :] * N_KEYS + i1[:nb, :])
    for b in range(3):
        na = k if b == 0 else k // 2
        a_iota = lax.broadcasted_iota(I32, (na, lanes), 0).astype(F32)
        vals.append(jnp.where(a_iota >= 4.0, v0[:na, :] + v1[b:b + 1, :], -jnp.inf))
        poss.append(a_iota * k + b)
        eids.append(i0[:na, :] * N_KEYS + i1[b:b + 1, :])
    return jnp.concatenate(vals, axis=0), jnp.concatenate(poss, axis=0), jnp.concatenate(eids, axis=0)


def _route_kernel(q_ref, keys_ref, idx_ref, gate_ref, idx_t, gate_t):
    tt = q_ref.shape[0]
    key_pos = lax.broadcasted_iota(I32, (N_KEYS, tt), 0).astype(F32)

    def head(h, carry):
        halves = []
        for p in range(2):
            qh = q_ref[:, pl.ds(pl.multiple_of((2 * h + p) * PEER_HALF, PEER_HALF), PEER_HALF)]
            sc = _nt_dot(keys_ref[h, p], qh)
            halves.append(_topk_rows(sc, key_pos, None, PEER_TOPK))
        (v0, i0), (v1, i1) = halves
        cv, cp, ce = _candidates(v0, i0, v1, i1)
        top_s, top_e = _topk_rows(cv, cp, ce, PEER_TOPK)
        e = jnp.exp(top_s - top_s[0:1, :])
        gate = e / jnp.sum(e, axis=0, keepdims=True)
        rows = pl.ds(pl.multiple_of(h * PEER_TOPK, PEER_TOPK), PEER_TOPK)
        idx_t[rows, :] = top_e
        gate_t[rows, :] = gate
        return carry

    lax.fori_loop(0, PEER_HEADS, head, 0)
    idx_ref[...] = idx_t[...].T.astype(I32)
    gate_ref[...] = gate_t[...].T


def _route(q2d, keys_bf16):
    t, nq = q2d.shape
    tt = ROUTE_TILE
    return pl.pallas_call(
        _route_kernel,
        grid=(t // tt,),
        in_specs=[
            pl.BlockSpec((tt, nq), lambda i: (i, 0)),
            pl.BlockSpec(keys_bf16.shape, lambda i: (0, 0, 0, 0)),
        ],
        out_specs=[
            pl.BlockSpec((tt, PEER_SLOTS), lambda i: (i, 0)),
            pl.BlockSpec((tt, PEER_SLOTS), lambda i: (i, 0)),
        ],
        out_shape=[
            jax.ShapeDtypeStruct((t, PEER_SLOTS), I32),
            jax.ShapeDtypeStruct((t, PEER_SLOTS), F32),
        ],
        scratch_shapes=[pltpu.VMEM((PEER_SLOTS, tt), F32), pltpu.VMEM((PEER_SLOTS, tt), F32)],
        compiler_params=pltpu.CompilerParams(dimension_semantics=("arbitrary",), vmem_limit_bytes=VMEM_LIMIT),
        name="peer_route",
    )(q2d, keys_bf16)


def _pack_table(tab):
    half = tab.shape[1] // 2
    bits = lax.bitcast_convert_type(tab.astype(BF16), jnp.uint16).astype(jnp.uint32)
    return lax.bitcast_convert_type(bits[:, :half] | (bits[:, half:] << 16), I32)


SC_LANES = 16
RING_ROWS = 32
RING_AHEAD = 3
UDOT_GROUP = 16


BF16_GROUP = 4


def _widen_pair_sum(acc, packed_bf16):
    bits = plsc.bitcast(packed_bf16, I32)
    lo = lax.bitcast_convert_type(bits << 16, F32)
    hi = lax.bitcast_convert_type(bits & jnp.int32(HI_HALF), F32)
    return acc + lo + hi


def _udot_rows(rows_ref, h_ref, out_ref, out_base):
    nrows, width = rows_ref.shape
    span = BF16_GROUP * SC_LANES
    lane = lax.iota(I32, SC_LANES)

    @pl.loop(0, nrows // UDOT_GROUP)
    def _(g):
        r0 = g * UDOT_GROUP

        def chunk(c, accs):
            off = pl.multiple_of(c * span, span)
            hs = [plsc.bitcast(h_ref[pl.ds(off + i * SC_LANES, SC_LANES)], BF16) for i in range(BF16_GROUP)]
            out = []
            for r in range(UDOT_GROUP):
                part = None
                for i in range(BF16_GROUP):
                    w = plsc.bitcast(rows_ref[r0 + r, pl.ds(off + i * SC_LANES, SC_LANES)], BF16)
                    part = w * hs[i] if part is None else part + w * hs[i]
                out.append(_widen_pair_sum(accs[r], part))
            return tuple(out)

        zero = jnp.zeros((SC_LANES,), F32)
        accs = lax.fori_loop(0, width // span, chunk, (zero,) * UDOT_GROUP)
        vec = zero
        for r in range(UDOT_GROUP):
            vec = jnp.where(lane == r, jnp.sum(accs[r]), vec)
        out_ref[pl.ds(pl.multiple_of(out_base + r0, UDOT_GROUP), UDOT_GROUP)] = vec


def _sc_udot(table, idx, h):
    t, k = idx.shape
    width = table.shape[1]
    tpw = t // SC_WORKERS
    slots = k // RING_ROWS
    assert tpw * SC_WORKERS == t and tpw % 2 == 0 and slots == RING_AHEAD + 1
    nbuf = tpw * slots
    mesh = plsc.VectorSubcoreMesh(core_axis_name="c", subcore_axis_name="s")

    def body(table_hbm, idx_hbm, h_hbm, out_hbm, idx_v, rows_v, h_v, a_v, gsem, hsem):
        wid = lax.axis_index("s") * SC_CORES + lax.axis_index("c")
        tok0 = wid * tpw
        pltpu.sync_copy(idx_hbm.at[wid], idx_v)

        def h_slot(slot):
            return h_v.at[pl.ds(slot * width, width)]

        def gather(j, slot):
            rows = idx_v.at[pl.ds(pl.multiple_of(j * RING_ROWS, RING_ROWS), RING_ROWS)]
            return pltpu.make_async_copy(table_hbm.at[rows], rows_v.at[slot], gsem.at[slot])

        def h_copy(tok, slot):
            return pltpu.make_async_copy(h_hbm.at[tok0 + tok], h_slot(slot), hsem.at[slot])

        for j in range(RING_AHEAD):
            gather(j, j).start()
        h_copy(0, 0).start()

        @pl.loop(0, tpw, step=2)
        def _(t0):
            for hs in range(2):
                tok = t0 + hs
                h_copy(tok, hs).wait()

                @pl.when(tok + 1 < tpw)
                def _():
                    h_copy(tok + 1, 1 - hs).start()

                for slot in range(slots):
                    j = tok * slots + slot
                    gather(j, slot).wait()

                    @pl.when(j + RING_AHEAD < nbuf)
                    def _():
                        gather(j + RING_AHEAD, (slot + RING_AHEAD) % slots).start()

                    _udot_rows(rows_v.at[slot], h_slot(hs), a_v, tok * k + slot * RING_ROWS)

        pltpu.sync_copy(a_v, out_hbm.at[pl.ds(pl.multiple_of(tok0 * k, 8), tpw * k)])

    out = pl.kernel(
        body,
        out_type=jax.ShapeDtypeStruct((t * k,), F32),
        mesh=mesh,
        scratch_types=[
            pltpu.VMEM((nbuf * RING_ROWS,), I32),
            pltpu.VMEM((slots, RING_ROWS, width), table.dtype),
            pltpu.VMEM((2 * width,), I32),
            pltpu.VMEM((tpw * k,), F32),
            pltpu.SemaphoreType.DMA((slots,)),
            pltpu.SemaphoreType.DMA((2,)),
        ],
        compiler_params=pltpu.CompilerParams(needs_layout_passes=False),
        name="peer_udot",
    )(table, idx.reshape(SC_WORKERS, nbuf * RING_ROWS), h)
    return out.reshape(t, k)


VSUM_CHUNKS = 8


def _vsum_rows(rows_ref, wgt_ref, wgt_base, out_ref, out_base, first):
    nrows, width = rows_ref.shape
    span = VSUM_CHUNKS * SC_LANES

    @pl.loop(0, width // span)
    def _(blk):
        col0 = pl.multiple_of(blk * span, span)

        def row_group(rg, accs):
            r0 = rg * BF16_GROUP
            parts = [None] * VSUM_CHUNKS
            wvec = wgt_ref[pl.ds(pl.multiple_of(wgt_base + (r0 // SC_LANES) * SC_LANES, SC_LANES), SC_LANES)]
            for i in range(BF16_GROUP):
                lane = jnp.full((SC_LANES,), r0 % SC_LANES + i, I32)
                wv = plsc.bitcast(wvec.at[lane].get(mode="promise_in_bounds"), BF16)
                for c in range(VSUM_CHUNKS):
                    w = plsc.bitcast(rows_ref[r0 + i, pl.ds(col0 + c * SC_LANES, SC_LANES)], BF16)
                    parts[c] = w * wv if parts[c] is None else parts[c] + w * wv
            out = []
            for c in range(VSUM_CHUNKS):
                bits = plsc.bitcast(parts[c], I32)
                out.append(accs[2 * c] + lax.bitcast_convert_type(bits << 16, F32))
                out.append(accs[2 * c + 1] + lax.bitcast_convert_type(bits & jnp.int32(HI_HALF), F32))
            return tuple(out)

        zero = jnp.zeros((SC_LANES,), F32)
        accs = lax.fori_loop(0, nrows // BF16_GROUP, row_group, (zero,) * (2 * VSUM_CHUNKS))
        for c in range(VSUM_CHUNKS):
            for half in range(2):
                dst = pl.ds(pl.multiple_of(out_base + half * width + col0 + c * SC_LANES, SC_LANES), SC_LANES)
                if first:
                    out_ref[dst] = accs[2 * c + half]
                else:
                    out_ref[dst] = out_ref[dst] + accs[2 * c + half]


def _sc_vsum(table, idx, wgt):
    t, k = idx.shape
    width = table.shape[1]
    d = 2 * width
    tpw = t // SC_WORKERS
    slots = k // RING_ROWS
    assert tpw * SC_WORKERS == t and tpw % 2 == 0 and slots == RING_AHEAD + 1
    nbuf = tpw * slots
    mesh = plsc.VectorSubcoreMesh(core_axis_name="c", subcore_axis_name="s")

    def body(table_hbm, idx_hbm, wgt_hbm, out_hbm, idx_v, rows_v, wgt_v, out_v, gsem, osem):
        wid = lax.axis_index("s") * SC_CORES + lax.axis_index("c")
        tok0 = wid * tpw
        pltpu.sync_copy(idx_hbm.at[wid], idx_v)
        pltpu.sync_copy(wgt_hbm.at[wid], wgt_v)

        def gather(j, slot):
            rows = idx_v.at[pl.ds(pl.multiple_of(j * RING_ROWS, RING_ROWS), RING_ROWS)]
            return pltpu.make_async_copy(table_hbm.at[rows], rows_v.at[slot], gsem.at[slot])

        def put(tok, slot):
            return pltpu.make_async_copy(out_v.at[pl.ds(slot * d, d)], out_hbm.at[tok0 + tok], osem.at[slot])

        for j in range(RING_AHEAD):
            gather(j, j).start()

        @pl.loop(0, tpw, step=2)
        def _(t0):
            for os_ in range(2):
                tok = t0 + os_

                @pl.when(tok >= 2)
                def _():
                    put(tok - 2, os_).wait()

                for slot in range(slots):
                    j = tok * slots + slot
                    gather(j, slot).wait()

                    @pl.when(j + RING_AHEAD < nbuf)
                    def _():
                        gather(j + RING_AHEAD, (slot + RING_AHEAD) % slots).start()

                    _vsum_rows(rows_v.at[slot], wgt_v, tok * k + slot * RING_ROWS, out_v, os_ * d, slot == 0)
                put(tok, os_).start()

        put(tpw - 2, 0).wait()
        put(tpw - 1, 1).wait()

    return pl.kernel(
        body,
        out_type=jax.ShapeDtypeStruct((t, d), F32),
        mesh=mesh,
        scratch_types=[
            pltpu.VMEM((nbuf * RING_ROWS,), I32),
            pltpu.VMEM((slots, RING_ROWS, width), table.dtype),
            pltpu.VMEM((tpw * k,), I32),
            pltpu.VMEM((2 * d,), F32),
            pltpu.SemaphoreType.DMA((slots,)),
            pltpu.SemaphoreType.DMA((2,)),
        ],
        compiler_params=pltpu.CompilerParams(needs_layout_passes=False),
        name="peer_vsum",
    )(table, idx.reshape(SC_WORKERS, nbuf * RING_ROWS), wgt.reshape(SC_WORKERS, tpw * k))


def _wgt_kernel(a_ref, gate_ref, o_ref):
    a = a_ref[...]
    w = (gate_ref[...] * (0.5 * a * (1.0 + lax.erf(a * (2.0 ** -0.5))))).astype(BF16)
    o_ref[...] = _pack_pairs(w, w)


def _expert_weights(a, gate):
    t, k = a.shape
    tm = math.gcd(t, 1024)
    spec = pl.BlockSpec((tm, k), lambda i: (i, 0))
    return pl.pallas_call(
        _wgt_kernel,
        grid=(t // tm,),
        in_specs=[spec, spec],
        out_specs=spec,
        out_shape=jax.ShapeDtypeStruct((t, k), I32),
        compiler_params=pltpu.CompilerParams(dimension_semantics=("arbitrary",), vmem_limit_bytes=VMEM_LIMIT),
        name="peer_weights",
    )(a, gate)


def _final_kernel(x1_ref, ffn_ref, mod_ref, ln_ref, o_ref, *, alpha):
    y = alpha * x1_ref[...] + mod_ref[5:6, :] * ffn_ref[...]
    o_ref[...] = _layer_norm(y, ln_ref[0:1, :], ln_ref[1:2, :])


def _final(x1, ffn, mod3, ln2, alpha, b0):
    nb, s, d = x1.shape
    tm = ROW_TILE
    row = lambda b, i: (b, i, 0)
    return pl.pallas_call(
        functools.partial(_final_kernel, alpha=alpha),
        grid=(nb, s // tm),
        in_specs=[
            pl.BlockSpec((None, tm, d), row),
            pl.BlockSpec((None, tm, d), row),
            pl.BlockSpec((None, 6, d), lambda b, i: (b0 + b, 0, 0)),
            pl.BlockSpec((2, d), lambda b, i: (0, 0)),
        ],
        out_specs=pl.BlockSpec((None, tm, d), row),
        out_shape=jax.ShapeDtypeStruct((nb, s, d), F32),
        compiler_params=pltpu.CompilerParams(
            dimension_semantics=("arbitrary", "arbitrary"), vmem_limit_bytes=VMEM_LIMIT),
        name="deepnorm_ln2",
    )(x1, ffn, mod3, ln2)


CHUNK_BATCHES = 1
CHUNK_LAG = 6
STAGE_LAG = 2


def _layer_chunk(x, b0, nb, mod3, slopes, lam_vecs, lambda_init, alpha, w, prev_wgt):
    _, seq, d = x.shape
    proj = _inproj(x, mod3, w["w_in"], b0, nb)
    diff_out = _diff_attention(proj, slopes, lam_vecs, w["subln_g"], lambda_init)
    swa_out = _swa_attention(proj, slopes, w["sinks"])
    x1, h2, q = _mid(diff_out, swa_out, x, mod3, w["w_out"], w["ln1"], w["w_pq"], alpha, b0)
    idx, gate = _route(q.reshape(nb * seq, -1), w["sub_keys"])
    if prev_wgt is not None:
        idx, _ = lax.optimization_barrier((idx, prev_wgt))
    a = _sc_udot(w["u_pack"], idx, h2.reshape(nb * seq, d // 2))
    wgt = _expert_weights(a, gate)
    ffn = _sc_vsum(w["v_pack"], idx, wgt)
    return _final(x1, ffn.reshape(nb, seq, d), mod3, w["ln2"], alpha, b0), wgt


def kernel(x, c, w_ada, b_ada, w_in, lambda_q1, lambda_k1, lambda_q2, lambda_k2, subln_g, sinks, w_out, ln1_g, ln1_b, w_pq, sub_keys, u_tab, v_tab, ln2_g, ln2_b):
    bsz, seq, d = x.shape
    depth = w_ada.shape[0]
    alpha = (2 * depth) ** 0.25
    slopes = jnp.exp2(-8.0 * jnp.arange(1, N_ATT_HEADS + 1, dtype=F32) / N_ATT_HEADS)
    nb = CHUNK_BATCHES
    for l in range(depth):
        lambda_init = 0.8 - 0.6 * math.exp(-0.3 * l)
        mod3 = _mod(c, w_ada[l], b_ada[l]).reshape(bsz, 6, d)
        lam_vecs = jnp.stack([lambda_q1[l], lambda_k1[l], lambda_q2[l], lambda_k2[l]])
        w = dict(w_in=w_in[l].astype(BF16), subln_g=subln_g[l], sinks=sinks[l], w_out=w_out[l].astype(BF16),
                 ln1=jnp.stack([ln1_g[l], ln1_b[l]]), w_pq=w_pq[l].astype(BF16),
                 sub_keys=sub_keys[l].astype(BF16), u_pack=_pack_table(u_tab[l]), v_pack=_pack_table(v_tab[l]),
                 ln2=jnp.stack([ln2_g[l], ln2_b[l]]))
        outs, wgts = [], []
        for ci, b0 in enumerate(range(0, bsz, nb)):
            mod_c = mod3
            if ci >= CHUNK_LAG:
                mod_c, outs[ci - CHUNK_LAG] = lax.optimization_barrier((mod3, outs[ci - CHUNK_LAG]))
            prev_wgt = wgts[ci - STAGE_LAG] if ci >= STAGE_LAG else None
            out, wgt = _layer_chunk(x, b0, nb, mod_c, slopes, lam_vecs, lambda_init, alpha, w, prev_wgt)
            outs.append(out)
            wgts.append(wgt)
        x = jnp.concatenate(outs, axis=0).reshape(bsz, seq, d)
    return x
```

```python
import functools
import math

import jax
import jax.numpy as jnp
from jax import lax
from jax.experimental import pallas as pl
from jax.experimental.pallas import tpu as pltpu
from jax.experimental.pallas import tpu_sc as plsc

F32 = jnp.float32
BF16 = jnp.bfloat16
I32 = jnp.int32

HEAD_DIM = 64
DIFF_HEADS = 4
DIFF_V = 2 * HEAD_DIM
SWA_Q_HEADS = 8
SWA_KV_HEADS = 2
SWA_GROUP = SWA_Q_HEADS // SWA_KV_HEADS
WINDOW = 128
N_ATT_HEADS = SWA_Q_HEADS + DIFF_HEADS
PEER_HEADS = 8
N_KEYS = 128
PEER_HALF = 128
PEER_TOPK = 16
PEER_SLOTS = PEER_HEADS * PEER_TOPK
LN_EPS = 1e-5
NEG_INF = -1e30

LANES = 128
VMEM_LIMIT = 48 * 1024 * 1024
HI_HALF = -65536

ROW_TILE = 512
ATTN_TILE = 512
ROUTE_TILE = 4 * LANES
MOD_COL_TILE = 768

SC_CORES = 2
SC_SUBCORES = 16
SC_WORKERS = SC_CORES * SC_SUBCORES


def _nt_dot(a, b):
    return lax.dot_general(a, b, (((1,), (1,)), ((), ())), preferred_element_type=F32)


def _mod_kernel(c_ref, w_ref, b_ref, o_ref):
    c = c_ref[...]
    s = c * (1.0 / (1.0 + jnp.exp(-c)))
    o_ref[...] = jnp.dot(s.astype(BF16), w_ref[...].astype(BF16), preferred_element_type=F32) + b_ref[...]


def _mod(c, w, b):
    bsz, d = c.shape
    n = w.shape[1]
    tn = MOD_COL_TILE
    return pl.pallas_call(
        _mod_kernel,
        grid=(n // tn,),
        in_specs=[
            pl.BlockSpec((bsz, d), lambda j: (0, 0)),
            pl.BlockSpec((d, tn), lambda j: (0, j)),
            pl.BlockSpec((1, tn), lambda j: (0, j)),
        ],
        out_specs=pl.BlockSpec((bsz, tn), lambda j: (0, j)),
        out_shape=jax.ShapeDtypeStruct((bsz, n), F32),
        compiler_params=pltpu.CompilerParams(dimension_semantics=("arbitrary",), vmem_limit_bytes=VMEM_LIMIT),
        name="adaln_mod",
    )(c, w, b.reshape(1, n))


def _inproj_kernel(x_ref, mod_ref, w_ref, o_ref):
    h = x_ref[...] * (1.0 + mod_ref[1:2, :]) + mod_ref[0:1, :]
    o_ref[...] = jnp.dot(h.astype(BF16), w_ref[...], preferred_element_type=F32).astype(BF16)


def _inproj(x, mod3, w_bf16, b0, bsz):
    _, s, d = x.shape
    n = w_bf16.shape[1]
    tm = ROW_TILE
    return pl.pallas_call(
        _inproj_kernel,
        grid=(bsz, s // tm),
        in_specs=[
            pl.BlockSpec((None, tm, d), lambda b, i: (b0 + b, i, 0)),
            pl.BlockSpec((None, 6, d), lambda b, i: (b0 + b, 0, 0)),
            pl.BlockSpec((d, n), lambda b, i: (0, 0)),
        ],
        out_specs=pl.BlockSpec((None, tm, n), lambda b, i: (b, i, 0)),
        out_shape=jax.ShapeDtypeStruct((bsz, s, n), BF16),
        compiler_params=pltpu.CompilerParams(
            dimension_semantics=("arbitrary", "arbitrary"), vmem_limit_bytes=VMEM_LIMIT),
        name="in_proj",
    )(x, mod3, w_bf16)


def _diff_kernel(slopes_ref, q_ref, k_ref, v_ref, lam_ref, g_ref, o_ref, *, tq, lambda_init):
    h = pl.program_id(1)
    i = pl.program_id(2)
    slope = slopes_ref[SWA_Q_HEADS + h]
    q = q_ref[...] * (HEAD_DIM ** -0.5)
    qs = (q[:, :HEAD_DIM], q[:, HEAD_DIM:])

    def tile(j, carry, diagonal):
        ks = k_ref[pl.ds(pl.multiple_of(j * tq, tq), tq), :]
        vs = v_ref[pl.ds(pl.multiple_of(j * tq, tq), tq), :]
        col_bias = slope * (j * tq + lax.broadcasted_iota(I32, (1, tq), 1)).astype(F32)
        if diagonal:
            valid = lax.broadcasted_iota(I32, (tq, tq), 0) >= lax.broadcasted_iota(I32, (tq, tq), 1)
        new = []
        for m in range(2):
            mx, l, acc = carry[3 * m: 3 * m + 3]
            s = _nt_dot(qs[m], ks[:, m * HEAD_DIM:(m + 1) * HEAD_DIM]) + col_bias
            if diagonal:
                s = jnp.where(valid, s, NEG_INF)
            mx_new = jnp.maximum(mx, jnp.max(s, axis=-1, keepdims=True))
            p = jnp.exp(s - mx_new)
            corr = jnp.exp(mx - mx_new)
            l = l * corr + jnp.sum(p, axis=-1, keepdims=True)
            acc = acc * corr + jnp.dot(p.astype(BF16), vs, preferred_element_type=F32)
            new += [mx_new, l, acc]
        return tuple(new)

    init = []
    for _ in range(2):
        init += [jnp.full((tq, 1), NEG_INF, F32), jnp.zeros((tq, 1), F32), jnp.zeros((tq, DIFF_V), F32)]
    carry = lax.fori_loop(0, i, lambda j, c: tile(j, c, False), tuple(init))
    m0, l0, a0, m1, l1, a1 = tile(i, carry, True)

    lam_v = lam_ref[...]
    lam = (jnp.exp(jnp.sum(lam_v[0:1, :] * lam_v[1:2, :], axis=-1, keepdims=True))
           - jnp.exp(jnp.sum(lam_v[2:3, :] * lam_v[3:4, :], axis=-1, keepdims=True)) + lambda_init)
    o = a0 / l0 - lam * (a1 / l1)
    o = o * lax.rsqrt(jnp.mean(o * o, axis=-1, keepdims=True) + LN_EPS)
    o_ref[...] = (o * g_ref[...] * (1.0 - lambda_init)).astype(BF16)


def _diff_attention(proj, slopes, lam_vecs, subln_g, lambda_init):
    bsz, s, _ = proj.shape
    tq = ATTN_TILE
    kcol = DIFF_HEADS
    vcol = 2 * DIFF_HEADS
    return pl.pallas_call(
        functools.partial(_diff_kernel, tq=tq, lambda_init=lambda_init),
        grid=(bsz, DIFF_HEADS, s // tq),
        in_specs=[
            pl.BlockSpec(memory_space=pltpu.SMEM),
            pl.BlockSpec((None, tq, DIFF_V), lambda b, h, i: (b, i, h)),
            pl.BlockSpec((None, s, DIFF_V), lambda b, h, i: (b, 0, kcol + h)),
            pl.BlockSpec((None, s, DIFF_V), lambda b, h, i: (b, 0, vcol + h)),
            pl.BlockSpec((4, HEAD_DIM), lambda b, h, i: (0, 0)),
            pl.BlockSpec((1, DIFF_V), lambda b, h, i: (0, 0)),
        ],
        out_specs=pl.BlockSpec((None, tq, DIFF_V), lambda b, h, i: (b, i, h)),
        out_shape=jax.ShapeDtypeStruct((bsz, s, DIFF_HEADS * DIFF_V), BF16),
        compiler_params=pltpu.CompilerParams(
            dimension_semantics=("arbitrary", "arbitrary", "arbitrary"), vmem_limit_bytes=VMEM_LIMIT),
        name="diff_attention",
    )(slopes, proj, proj, proj, lam_vecs, subln_g.reshape(1, DIFF_V))


def _swa_kernel(slopes_ref, sinks_ref, q_ref, k_ref, v_ref, o_ref, *, tq):
    i = pl.program_id(1)
    scale = HEAD_DIM ** -0.5
    blk = WINDOW
    ii = lax.broadcasted_iota(I32, (blk, 2 * blk), 0)
    jj = lax.broadcasted_iota(I32, (blk, 2 * blk), 1)
    for r in range(tq // blk):
        start = i * tq + r * blk
        kstart = jnp.maximum(start - blk, 0)
        kb = k_ref[pl.ds(pl.multiple_of(kstart, blk), 2 * blk), :]
        vb = v_ref[pl.ds(pl.multiple_of(kstart, blk), 2 * blk), :]
        dist = (start + ii) - (kstart + jj)
        valid = (dist >= 0) & (dist < WINDOW)
        distf = dist.astype(F32)
        outs = []
        for kvh in range(SWA_KV_HEADS):
            k = kb[:, kvh * HEAD_DIM:(kvh + 1) * HEAD_DIM]
            v = vb[:, kvh * HEAD_DIM:(kvh + 1) * HEAD_DIM]
            for g in range(SWA_GROUP):
                hq = kvh * SWA_GROUP + g
                qh = q_ref[r * blk:(r + 1) * blk, hq * HEAD_DIM:(hq + 1) * HEAD_DIM]
                s = _nt_dot(qh, k) * scale - slopes_ref[hq] * distf
                s = jnp.where(valid, s, NEG_INF)
                sink = sinks_ref[hq]
                m = jnp.maximum(jnp.max(s, axis=-1, keepdims=True), sink)
                p = jnp.exp(s - m)
                denom = jnp.sum(p, axis=-1, keepdims=True) + jnp.exp(sink - m)
                outs.append(jnp.dot(p.astype(BF16), v, preferred_element_type=F32) / denom)
        o_ref[r * blk:(r + 1) * blk, :] = jnp.concatenate(outs, axis=-1).astype(BF16)


def _swa_attention(proj, slopes, sinks):
    bsz, s, _ = proj.shape
    tq = ATTN_TILE
    width = SWA_Q_HEADS * HEAD_DIM
    qcol = (3 * DIFF_HEADS * DIFF_V) // width
    kcol = (3 * DIFF_HEADS * DIFF_V + width) // LANES
    return pl.pallas_call(
        functools.partial(_swa_kernel, tq=tq),
        grid=(bsz, s // tq),
        in_specs=[
            pl.BlockSpec(memory_space=pltpu.SMEM),
            pl.BlockSpec(memory_space=pltpu.SMEM),
            pl.BlockSpec((None, tq, width), lambda b, i: (b, i, qcol)),
            pl.BlockSpec((None, s, LANES), lambda b, i: (b, 0, kcol)),
            pl.BlockSpec((None, s, LANES), lambda b, i: (b, 0, kcol + 1)),
        ],
        out_specs=pl.BlockSpec((None, tq, width), lambda b, i: (b, i, 0)),
        out_shape=jax.ShapeDtypeStruct((bsz, s, width), BF16),
        compiler_params=pltpu.CompilerParams(
            dimension_semantics=("arbitrary", "arbitrary"), vmem_limit_bytes=VMEM_LIMIT),
        name="swa_attention",
    )(slopes, sinks, proj, proj, proj)


def _layer_norm(y, g, b):
    mu = jnp.mean(y, axis=-1, keepdims=True)
    yc = y - mu
    var = jnp.mean(yc * yc, axis=-1, keepdims=True)
    return yc * lax.rsqrt(var + LN_EPS) * g + b


def _pack_pairs(lo, hi):
    lo_bits = lax.bitcast_convert_type(lo.astype(F32), I32)
    hi_bits = lax.bitcast_convert_type(hi.astype(F32), I32)
    return lax.shift_right_logical(lo_bits, 16) | (hi_bits & jnp.int32(HI_HALF))


def _mid_kernel(do_ref, so_ref, x_ref, mod_ref, wo_ref, ln_ref, wpq_ref, x1_ref, h2_ref, q_ref, *, alpha):
    nd = do_ref.shape[-1]
    mixed = (jnp.dot(do_ref[...], wo_ref[:nd, :], preferred_element_type=F32)
             + jnp.dot(so_ref[...], wo_ref[nd:, :], preferred_element_type=F32))
    y = alpha * x_ref[...] + mod_ref[2:3, :] * mixed
    x1 = _layer_norm(y, ln_ref[0:1, :], ln_ref[1:2, :])
    x1_ref[...] = x1
    h2 = (x1 * (1.0 + mod_ref[4:5, :]) + mod_ref[3:4, :]).astype(BF16)
    half = h2.shape[-1] // 2
    h2_ref[...] = _pack_pairs(h2[:, :half], h2[:, half:])
    q_ref[...] = jnp.dot(h2, wpq_ref[...], preferred_element_type=F32).astype(BF16)


def _mid(diff_out, swa_out, x, mod3, wo_bf16, ln1, wpq_bf16, alpha, b0):
    bsz = diff_out.shape[0]
    _, s, d = x.shape
    nq = wpq_bf16.shape[1]
    tm = ROW_TILE
    row = lambda b, i: (b, i, 0)
    const = lambda b, i: (0, 0)
    return pl.pallas_call(
        functools.partial(_mid_kernel, alpha=alpha),
        grid=(bsz, s // tm),
        in_specs=[
            pl.BlockSpec((None, tm, diff_out.shape[-1]), row),
            pl.BlockSpec((None, tm, swa_out.shape[-1]), row),
            pl.BlockSpec((None, tm, d), lambda b, i: (b0 + b, i, 0)),
            pl.BlockSpec((None, 6, d), lambda b, i: (b0 + b, 0, 0)),
            pl.BlockSpec(wo_bf16.shape, const),
            pl.BlockSpec((2, d), const),
            pl.BlockSpec(wpq_bf16.shape, const),
        ],
        out_specs=[
            pl.BlockSpec((None, tm, d), row),
            pl.BlockSpec((None, tm, d // 2), row),
            pl.BlockSpec((None, tm, nq), row),
        ],
        out_shape=[
            jax.ShapeDtypeStruct((bsz, s, d), F32),
            jax.ShapeDtypeStruct((bsz, s, d // 2), I32),
            jax.ShapeDtypeStruct((bsz, s, nq), BF16),
        ],
        compiler_params=pltpu.CompilerParams(
            dimension_semantics=("arbitrary", "arbitrary"), vmem_limit_bytes=VMEM_LIMIT),
        name="outproj_ln1_peerq",
    )(diff_out, swa_out, x, mod3, wo_bf16, ln1, wpq_bf16)


def _topk_rows(vals, pos, payload, k):
    out_v, out_p = [], []
    for _ in range(k):
        m = jnp.max(vals, axis=0, keepdims=True)
        first = jnp.min(jnp.where(vals == m, pos, jnp.inf), axis=0, keepdims=True)
        sel = pos == first
        if payload is None:
            out_p.append(first)
        else:
            out_p.append(jnp.max(jnp.where(sel, payload, -1.0), axis=0, keepdims=True))
        out_v.append(m)
        vals = jnp.where(sel, -jnp.inf, vals)
    return jnp.concatenate(out_v, axis=0), jnp.concatenate(out_p, axis=0)


def _candidates(v0, i0, v1, i1):
    k, lanes = v0.shape
    vals, poss, eids = [], [], []
    for a in range(4):
        nb = k if a == 0 else k // 2
        b_iota = lax.broadcasted_iota(I32, (nb, lanes), 0).astype(F32)
        vals.append(v0[a:a + 1, :] + v1[:nb, :])
        poss.append(a * k + b_iota)
        eids.append(i0[a:a + 1, :] * N_KEYS + i1[:nb, :])
    for b in range(3):
        na = k if b == 0 else k // 2
        a_iota = lax.broadcasted_iota(I32, (na, lanes), 0).astype(F32)
        vals.append(jnp.where(a_iota >= 4.0, v0[:na, :] + v1[b:b + 1, :], -jnp.inf))
        poss.append(a_iota * k + b)
        eids.append(i0[:na, :] * N_KEYS + i1[b:b + 1, :])
    return jnp.concatenate(vals, axis=0), jnp.concatenate(poss, axis=0), jnp.concatenate(eids, axis=0)


def _route_kernel(q_ref, keys_ref, idx_ref, gate_ref, idx_t, gate_t):
    tt = q_ref.shape[0]
    key_pos = lax.broadcasted_iota(I32, (N_KEYS, tt), 0).astype(F32)

    def head(h, carry):
        halves = []
        for p in range(2):
            qh = q_ref[:, pl.ds(pl.multiple_of((2 * h + p) * PEER_HALF, PEER_HALF), PEER_HALF)]
            sc = _nt_dot(keys_ref[h, p], qh)
            halves.append(_topk_rows(sc, key_pos, None, PEER_TOPK))
        (v0, i0), (v1, i1) = halves
        cv, cp, ce = _candidates(v0, i0, v1, i1)
        top_s, top_e = _topk_rows(cv, cp, ce, PEER_TOPK)
        e = jnp.exp(top_s - top_s[0:1, :])
        gate = e / jnp.sum(e, axis=0, keepdims=True)
        rows = pl.ds(pl.multiple_of(h * PEER_TOPK, PEER_TOPK), PEER_TOPK)
        idx_t[rows, :] = top_e
        gate_t[rows, :] = gate
        return carry

    lax.fori_loop(0, PEER_HEADS, head, 0)
    idx_ref[...] = idx_t[...].T.astype(I32)
    gate_ref[...] = gate_t[...].T


def _route(q2d, keys_bf16):
    t, nq = q2d.shape
    tt = ROUTE_TILE
    return pl.pallas_call(
        _route_kernel,
        grid=(t // tt,),
        in_specs=[
            pl.BlockSpec((tt, nq), lambda i: (i, 0)),
            pl.BlockSpec(keys_bf16.shape, lambda i: (0, 0, 0, 0)),
        ],
        out_specs=[
            pl.BlockSpec((tt, PEER_SLOTS), lambda i: (i, 0)),
            pl.BlockSpec((tt, PEER_SLOTS), lambda i: (i, 0)),
        ],
        out_shape=[
            jax.ShapeDtypeStruct((t, PEER_SLOTS), I32),
            jax.ShapeDtypeStruct((t, PEER_SLOTS), F32),
        ],
        scratch_shapes=[pltpu.VMEM((PEER_SLOTS, tt), F32), pltpu.VMEM((PEER_SLOTS, tt), F32)],
        compiler_params=pltpu.CompilerParams(dimension_semantics=("arbitrary",), vmem_limit_bytes=VMEM_LIMIT),
        name="peer_route",
    )(q2d, keys_bf16)


def _pack_table(tab):
    half = tab.shape[1] // 2
    bits = lax.bitcast_convert_type(tab.astype(BF16), jnp.uint16).astype(jnp.uint32)
    return lax.bitcast_convert_type(bits[:, :half] | (bits[:, half:] << 16), I32)


SC_LANES = 16
RING_ROWS = 32
RING_AHEAD = 3
UDOT_GROUP = 16


BF16_GROUP = 4


def _widen_pair_sum(acc, packed_bf16):
    bits = plsc.bitcast(packed_bf16, I32)
    lo = lax.bitcast_convert_type(bits << 16, F32)
    hi = lax.bitcast_convert_type(bits & jnp.int32(HI_HALF), F32)
    return acc + lo + hi


def _udot_rows(rows_ref, h_ref, out_ref, out_base):
    nrows, width = rows_ref.shape
    span = BF16_GROUP * SC_LANES
    lane = lax.iota(I32, SC_LANES)

    @pl.loop(0, nrows // UDOT_GROUP)
    def _(g):
        r0 = g * UDOT_GROUP

        def chunk(c, accs):
            off = pl.multiple_of(c * span, span)
            hs = [plsc.bitcast(h_ref[pl.ds(off + i * SC_LANES, SC_LANES)], BF16) for i in range(BF16_GROUP)]
            out = []
            for r in range(UDOT_GROUP):
                part = None
                for i in range(BF16_GROUP):
                    w = plsc.bitcast(rows_ref[r0 + r, pl.ds(off + i * SC_LANES, SC_LANES)], BF16)
                    part = w * hs[i] if part is None else part + w * hs[i]
                out.append(_widen_pair_sum(accs[r], part))
            return tuple(out)

        zero = jnp.zeros((SC_LANES,), F32)
        accs = lax.fori_loop(0, width // span, chunk, (zero,) * UDOT_GROUP)
        vec = zero
        for r in range(UDOT_GROUP):
            vec = jnp.where(lane == r, jnp.sum(accs[r]), vec)
        out_ref[pl.ds(pl.multiple_of(out_base + r0, UDOT_GROUP), UDOT_GROUP)] = vec


def _sc_udot(table, idx, h):
    t, k = idx.shape
    width = table.shape[1]
    tpw = t // SC_WORKERS
    slots = k // RING_ROWS
    assert tpw * SC_WORKERS == t and tpw % 2 == 0 and slots == RING_AHEAD + 1
    nbuf = tpw * slots
    mesh = plsc.VectorSubcoreMesh(core_axis_name="c", subcore_axis_name="s")

    def body(table_hbm, idx_hbm, h_hbm, out_hbm, idx_v, rows_v, h_v, a_v, gsem, hsem):
        wid = lax.axis_index("s") * SC_CORES + lax.axis_index("c")
        tok0 = wid * tpw
        pltpu.sync_copy(idx_hbm.at[wid], idx_v)

        def h_slot(slot):
            return h_v.at[pl.ds(slot * width, width)]

        def gather(j, slot):
            rows = idx_v.at[pl.ds(pl.multiple_of(j * RING_ROWS, RING_ROWS), RING_ROWS)]
            return pltpu.make_async_copy(table_hbm.at[rows], rows_v.at[slot], gsem.at[slot])

        def h_copy(tok, slot):
            return pltpu.make_async_copy(h_hbm.at[tok0 + tok], h_slot(slot), hsem.at[slot])

        for j in range(RING_AHEAD):
            gather(j, j).start()
        h_copy(0, 0).start()

        @pl.loop(0, tpw, step=2)
        def _(t0):
            for hs in range(2):
                tok = t0 + hs
                h_copy(tok, hs).wait()

                @pl.when(tok + 1 < tpw)
                def _():
                    h_copy(tok + 1, 1 - hs).start()

                for slot in range(slots):
                    j = tok * slots + slot
                    gather(j, slot).wait()

                    @pl.when(j + RING_AHEAD < nbuf)
                    def _():
                        gather(j + RING_AHEAD, (slot + RING_AHEAD) % slots).start()

                    _udot_rows(rows_v.at[slot], h_slot(hs), a_v, tok * k + slot * RING_ROWS)

        pltpu.sync_copy(a_v, out_hbm.at[pl.ds(pl.multiple_of(tok0 * k, 8), tpw * k)])

    out = pl.kernel(
        body,
        out_type=jax.ShapeDtypeStruct((t * k,), F32),
        mesh=mesh,
        scratch_types=[
            pltpu.VMEM((nbuf * RING_ROWS,), I32),
            pltpu.VMEM((slots, RING_ROWS, width), table.dtype),
            pltpu.VMEM((2 * width,), I32),
            pltpu.VMEM((tpw * k,), F32),
            pltpu.SemaphoreType.DMA((slots,)),
            pltpu.SemaphoreType.DMA((2,)),
        ],
        compiler_params=pltpu.CompilerParams(needs_layout_passes=False),
        name="peer_udot",
    )(table, idx.reshape(SC_WORKERS, nbuf * RING_ROWS), h)
    return out.reshape(t, k)


VSUM_CHUNKS = 8


def _vsum_rows(rows_ref, wgt_ref, wgt_base, out_ref, out_base, first):
    nrows, width = rows_ref.shape
    span = VSUM_CHUNKS * SC_LANES

    @pl.loop(0, width // span)
    def _(blk):
        col0 = pl.multiple_of(blk * span, span)

        def row_group(rg, accs):
            r0 = rg * BF16_GROUP
            parts = [None] * VSUM_CHUNKS
            wvec = wgt_ref[pl.ds(pl.multiple_of(wgt_base + (r0 // SC_LANES) * SC_LANES, SC_LANES), SC_LANES)]
            for i in range(BF16_GROUP):
                lane = jnp.full((SC_LANES,), r0 % SC_LANES + i, I32)
                wv = plsc.bitcast(wvec.at[lane].get(mode="promise_in_bounds"), BF16)
                for c in range(VSUM_CHUNKS):
                    w = plsc.bitcast(rows_ref[r0 + i, pl.ds(col0 + c * SC_LANES, SC_LANES)], BF16)
                    parts[c] = w * wv if parts[c] is None else parts[c] + w * wv
            out = []
            for c in range(VSUM_CHUNKS):
                bits = plsc.bitcast(parts[c], I32)
                out.append(accs[2 * c] + lax.bitcast_convert_type(bits << 16, F32))
                out.append(accs[2 * c + 1] + lax.bitcast_convert_type(bits & jnp.int32(HI_HALF), F32))
            return tuple(out)

        zero = jnp.zeros((SC_LANES,), F32)
        accs = lax.fori_loop(0, nrows // BF16_GROUP, row_group, (zero,) * (2 * VSUM_CHUNKS))
        for c in range(VSUM_CHUNKS):
            for half in range(2):
                dst = pl.ds(pl.multiple_of(out_base + half * width + col0 + c * SC_LANES, SC_LANES), SC_LANES)
                if first:
                    out_ref[dst] = accs[2 * c + half]
                else:
                    out_ref[dst] = out_ref[dst] + accs[2 * c + half]


def _sc_vsum(table, idx, wgt):
    t, k = idx.shape
    width = table.shape[1]
    d = 2 * width
    tpw = t // SC_WORKERS
    slots = k // RING_ROWS
    assert tpw * SC_WORKERS == t and tpw % 2 == 0 and slots == RING_AHEAD + 1
    nbuf = tpw * slots
    mesh = plsc.VectorSubcoreMesh(core_axis_name="c", subcore_axis_name="s")

    def body(table_hbm, idx_hbm, wgt_hbm, out_hbm, idx_v, rows_v, wgt_v, out_v, gsem, osem):
        wid = lax.axis_index("s") * SC_CORES + lax.axis_index("c")
        tok0 = wid * tpw
        pltpu.sync_copy(idx_hbm.at[wid], idx_v)
        pltpu.sync_copy(wgt_hbm.at[wid], wgt_v)

        def gather(j, slot):
            rows = idx_v.at[pl.ds(pl.multiple_of(j * RING_ROWS, RING_ROWS), RING_ROWS)]
            return pltpu.make_async_copy(table_hbm.at[rows], rows_v.at[slot], gsem.at[slot])

        def put(tok, slot):
            return pltpu.make_async_copy(out_v.at[pl.ds(slot * d, d)], out_hbm.at[tok0 + tok], osem.at[slot])

        for j in range(RING_AHEAD):
            gather(j, j).start()

        @pl.loop(0, tpw, step=2)
        def _(t0):
            for os_ in range(2):
                tok = t0 + os_

                @pl.when(tok >= 2)
                def _():
                    put(tok - 2, os_).wait()

                for slot in range(slots):
                    j = tok * slots + slot
                    gather(j, slot).wait()

                    @pl.when(j + RING_AHEAD < nbuf)
                    def _():
                        gather(j + RING_AHEAD, (slot + RING_AHEAD) % slots).start()

                    _vsum_rows(rows_v.at[slot], wgt_v, tok * k + slot * RING_ROWS, out_v, os_ * d, slot == 0)
                put(tok, os_).start()

        put(tpw - 2, 0).wait()
        put(tpw - 1, 1).wait()

    return pl.kernel(
        body,
        out_type=jax.ShapeDtypeStruct((t, d), F32),
        mesh=mesh,
        scratch_types=[
            pltpu.VMEM((nbuf * RING_ROWS,), I32),
            pltpu.VMEM((slots, RING_ROWS, width), table.dtype),
            pltpu.VMEM((tpw * k,), I32),
            pltpu.VMEM((2 * d,), F32),
            pltpu.SemaphoreType.DMA((slots,)),
            pltpu.SemaphoreType.DMA((2,)),
        ],
        compiler_params=pltpu.CompilerParams(needs_layout_passes=False),
        name="peer_vsum",
    )(table, idx.reshape(SC_WORKERS, nbuf * RING_ROWS), wgt.reshape(SC_WORKERS, tpw * k))


def _wgt_kernel(a_ref, gate_ref, o_ref):
    a = a_ref[...]
    w = (gate_ref[...] * (0.5 * a * (1.0 + lax.erf(a * (2.0 ** -0.5))))).astype(BF16)
    o_ref[...] = _pack_pairs(w, w)


def _expert_weights(a, gate):
    t, k = a.shape
    tm = math.gcd(t, 1024)
    spec = pl.BlockSpec((tm, k), lambda i: (i, 0))
    return pl.pallas_call(
        _wgt_kernel,
        grid=(t // tm,),
        in_specs=[spec, spec],
        out_specs=spec,
        out_shape=jax.ShapeDtypeStruct((t, k), I32),
        compiler_params=pltpu.CompilerParams(dimension_semantics=("arbitrary",), vmem_limit_bytes=VMEM_LIMIT),
        name="peer_weights",
    )(a, gate)


def _final_kernel(x1_ref, ffn_ref, mod_ref, ln_ref, o_ref, *, alpha):
    y = alpha * x1_ref[...] + mod_ref[5:6, :] * ffn_ref[...]
    o_ref[...] = _layer_norm(y, ln_ref[0:1, :], ln_ref[1:2, :])


def _final(x1, ffn, mod3, ln2, alpha, b0):
    nb, s, d = x1.shape
    tm = ROW_TILE
    row = lambda b, i: (b, i, 0)
    return pl.pallas_call(
        functools.partial(_final_kernel, alpha=alpha),
        grid=(nb, s // tm),
        in_specs=[
            pl.BlockSpec((None, tm, d), row),
            pl.BlockSpec((None, tm, d), row),
            pl.BlockSpec((None, 6, d), lambda b, i: (b0 + b, 0, 0)),
            pl.BlockSpec((2, d), lambda b, i: (0, 0)),
        ],
        out_specs=pl.BlockSpec((None, tm, d), row),
        out_shape=jax.ShapeDtypeStruct((nb, s, d), F32),
        compiler_params=pltpu.CompilerParams(
            dimension_semantics=("arbitrary", "arbitrary"), vmem_limit_bytes=VMEM_LIMIT),
        name="deepnorm_ln2",
    )(x1, ffn, mod3, ln2)


CHUNK_BATCHES = 1
CHUNK_LAG = 6
STAGE_LAG = 2


def _layer_chunk(x, b0, nb, mod3, slopes, lam_vecs, lambda_init, alpha, w, prev_wgt):
    _, seq, d = x.shape
    proj = _inproj(x, mod3, w["w_in"], b0, nb)
    diff_out = _diff_attention(proj, slopes, lam_vecs, w["subln_g"], lambda_init)
    swa_out = _swa_attention(proj, slopes, w["sinks"])
    x1, h2, q = _mid(diff_out, swa_out, x, mod3, w["w_out"], w["ln1"], w["w_pq"], alpha, b0)
    idx, gate = _route(q.reshape(nb * seq, -1), w["sub_keys"])
    if prev_wgt is not None:
        idx, _ = lax.optimization_barrier((idx, prev_wgt))
    a = _sc_udot(w["u_pack"], idx, h2.reshape(nb * seq, d // 2))
    wgt = _expert_weights(a, gate)
    ffn = _sc_vsum(w["v_pack"], idx, wgt)
    return _final(x1, ffn.reshape(nb, seq, d), mod3, w["ln2"], alpha, b0), wgt


def kernel(x, c, w_ada, b_ada, w_in, lambda_q1, lambda_k1, lambda_q2, lambda_k2, subln_g, sinks, w_out, ln1_g, ln1_b, w_pq, sub_keys, u_tab, v_tab, ln2_g, ln2_b):
    bsz, seq, d = x.shape
    depth = w_ada.shape[0]
    alpha = (2 * depth) ** 0.25
    slopes = jnp.exp2(-8.0 * jnp.arange(1, N_ATT_HEADS + 1, dtype=F32) / N_ATT_HEADS)
    nb = CHUNK_BATCHES
    for l in range(depth):
        lambda_init = 0.8 - 0.6 * math.exp(-0.3 * l)
        mod3 = _mod(c, w_ada[l], b_ada[l]).reshape(bsz, 6, d)
        lam_vecs = jnp.stack([lambda_q1[l], lambda_k1[l], lambda_q2[l], lambda_k2[l]])
        w = dict(w_in=w_in[l].astype(BF16), subln_g=subln_g[l], sinks=sinks[l], w_out=w_out[l].astype(BF16),
                 ln1=jnp.stack([ln1_g[l], ln1_b[l]]), w_pq=w_pq[l].astype(BF16),
                 sub_keys=sub_keys[l].astype(BF16), u_pack=_pack_table(u_tab[l]), v_pack=_pack_table(v_tab[l]),
                 ln2=jnp.stack([ln2_g[l], ln2_b[l]]))
        outs, wgts = [], []
        for ci, b0 in enumerate(range(0, bsz, nb)):
            mod_c = mod3
            if ci >= CHUNK_LAG:
                mod_c, outs[ci - CHUNK_LAG] = lax.optimization_barrier((mod3, outs[ci - CHUNK_LAG]))
            prev_wgt = wgts[ci - STAGE_LAG] if ci >= STAGE_LAG else None
            out, wgt = _layer_chunk(x, b0, nb, mod_c, slopes, lam_vecs, lambda_init, alpha, w, prev_wgt)
            outs.append(out)
            wgts.append(wgt)
        x = jnp.concatenate(outs, axis=0).reshape(bsz, seq, d)
    return x
```

```python
import functools
import math

import jax
import jax.numpy as jnp
from jax import lax
from jax.experimental import pallas as pl
from jax.experimental.pallas import tpu as pltpu
from jax.experimental.pallas import tpu_sc as plsc

F32 = jnp.float32
BF16 = jnp.bfloat16
I32 = jnp.int32

HEAD_DIM = 64
DIFF_HEADS = 4
DIFF_V = 2 * HEAD_DIM
SWA_Q_HEADS = 8
SWA_KV_HEADS = 2
SWA_GROUP = SWA_Q_HEADS // SWA_KV_HEADS
WINDOW = 128
N_ATT_HEADS = SWA_Q_HEADS + DIFF_HEADS
PEER_HEADS = 8
N_KEYS = 128
PEER_HALF = 128
PEER_TOPK = 16
PEER_SLOTS = PEER_HEADS * PEER_TOPK
LN_EPS = 1e-5
NEG_INF = -1e30

LANES = 128
VMEM_LIMIT = 48 * 1024 * 1024
HI_HALF = -65536

ROW_TILE = 512
ATTN_TILE = 512
ROUTE_TILE = 4 * LANES
MOD_COL_TILE = 768

SC_CORES = 2
SC_SUBCORES = 16
SC_WORKERS = SC_CORES * SC_SUBCORES


def _nt_dot(a, b):
    return lax.dot_general(a, b, (((1,), (1,)), ((), ())), preferred_element_type=F32)


def _mod_kernel(c_ref, w_ref, b_ref, o_ref):
    c = c_ref[...]
    s = c * (1.0 / (1.0 + jnp.exp(-c)))
    o_ref[...] = jnp.dot(s.astype(BF16), w_ref[...].astype(BF16), preferred_element_type=F32) + b_ref[...]


def _mod(c, w, b):
    bsz, d = c.shape
    n = w.shape[1]
    tn = MOD_COL_TILE
    return pl.pallas_call(
        _mod_kernel,
        grid=(n // tn,),
        in_specs=[
            pl.BlockSpec((bsz, d), lambda j: (0, 0)),
            pl.BlockSpec((d, tn), lambda j: (0, j)),
            pl.BlockSpec((1, tn), lambda j: (0, j)),
        ],
        out_specs=pl.BlockSpec((bsz, tn), lambda j: (0, j)),
        out_shape=jax.ShapeDtypeStruct((bsz, n), F32),
        compiler_params=pltpu.CompilerParams(dimension_semantics=("arbitrary",), vmem_limit_bytes=VMEM_LIMIT),
        name="adaln_mod",
    )(c, w, b.reshape(1, n))


def _inproj_kernel(x_ref, mod_ref, w_ref, o_ref):
    h = x_ref[...] * (1.0 + mod_ref[1:2, :]) + mod_ref[0:1, :]
    o_ref[...] = jnp.dot(h.astype(BF16), w_ref[...], preferred_element_type=F32).astype(BF16)


def _inproj(x, mod3, w_bf16, b0, bsz):
    _, s, d = x.shape
    n = w_bf16.shape[1]
    tm = ROW_TILE
    return pl.pallas_call(
        _inproj_kernel,
        grid=(bsz, s // tm),
        in_specs=[
            pl.BlockSpec((None, tm, d), lambda b, i: (b0 + b, i, 0)),
            pl.BlockSpec((None, 6, d), lambda b, i: (b0 + b, 0, 0)),
            pl.BlockSpec((d, n), lambda b, i: (0, 0)),
        ],
        out_specs=pl.BlockSpec((None, tm, n), lambda b, i: (b, i, 0)),
        out_shape=jax.ShapeDtypeStruct((bsz, s, n), BF16),
        compiler_params=pltpu.CompilerParams(
            dimension_semantics=("arbitrary", "arbitrary"), vmem_limit_bytes=VMEM_LIMIT),
        name="in_proj",
    )(x, mod3, w_bf16)


def _diff_kernel(slopes_ref, q_ref, k_ref, v_ref, lam_ref, g_ref, o_ref, *, tq, lambda_init):
    h = pl.program_id(1)
    i = pl.program_id(2)
    slope = slopes_ref[SWA_Q_HEADS + h]
    q = q_ref[...] * (HEAD_DIM ** -0.5)
    qs = (q[:, :HEAD_DIM], q[:, HEAD_DIM:])

    def tile(j, carry, diagonal):
        ks = k_ref[pl.ds(pl.multiple_of(j * tq, tq), tq), :]
        vs = v_ref[pl.ds(pl.multiple_of(j * tq, tq), tq), :]
        col_bias = slope * (j * tq + lax.broadcasted_iota(I32, (1, tq), 1)).astype(F32)
        if diagonal:
            valid = lax.broadcasted_iota(I32, (tq, tq), 0) >= lax.broadcasted_iota(I32, (tq, tq), 1)
        new = []
        for m in range(2):
            mx, l, acc = carry[3 * m: 3 * m + 3]
            s = _nt_dot(qs[m], ks[:, m * HEAD_DIM:(m + 1) * HEAD_DIM]) + col_bias
            if diagonal:
                s = jnp.where(valid, s, NEG_INF)
            mx_new = jnp.maximum(mx, jnp.max(s, axis=-1, keepdims=True))
            p = jnp.exp(s - mx_new)
            corr = jnp.exp(mx - mx_new)
            l = l * corr + jnp.sum(p, axis=-1, keepdims=True)
            acc = acc * corr + jnp.dot(p.astype(BF16), vs, preferred_element_type=F32)
            new += [mx_new, l, acc]
        return tuple(new)

    init = []
    for _ in range(2):
        init += [jnp.full((tq, 1), NEG_INF, F32), jnp.zeros((tq, 1), F32), jnp.zeros((tq, DIFF_V), F32)]
    carry = lax.fori_loop(0, i, lambda j, c: tile(j, c, False), tuple(init))
    m0, l0, a0, m1, l1, a1 = tile(i, carry, True)

    lam_v = lam_ref[...]
    lam = (jnp.exp(jnp.sum(lam_v[0:1, :] * lam_v[1:2, :], axis=-1, keepdims=True))
           - jnp.exp(jnp.sum(lam_v[2:3, :] * lam_v[3:4, :], axis=-1, keepdims=True)) + lambda_init)
    o = a0 / l0 - lam * (a1 / l1)
    o = o * lax.rsqrt(jnp.mean(o * o, axis=-1, keepdims=True) + LN_EPS)
    o_ref[...] = (o * g_ref[...] * (1.0 - lambda_init)).astype(BF16)


def _diff_attention(proj, slopes, lam_vecs, subln_g, lambda_init):
    bsz, s, _ = proj.shape
    tq = ATTN_TILE
    kcol = DIFF_HEADS
    vcol = 2 * DIFF_HEADS
    return pl.pallas_call(
        functools.partial(_diff_kernel, tq=tq, lambda_init=lambda_init),
        grid=(bsz, DIFF_HEADS, s // tq),
        in_specs=[
            pl.BlockSpec(memory_space=pltpu.SMEM),
            pl.BlockSpec((None, tq, DIFF_V), lambda b, h, i: (b, i, h)),
            pl.BlockSpec((None, s, DIFF_V), lambda b, h, i: (b, 0, kcol + h)),
            pl.BlockSpec((None, s, DIFF_V), lambda b, h, i: (b, 0, vcol + h)),
            pl.BlockSpec((4, HEAD_DIM), lambda b, h, i: (0, 0)),
            pl.BlockSpec((1, DIFF_V), lambda b, h, i: (0, 0)),
        ],
        out_specs=pl.BlockSpec((None, tq, DIFF_V), lambda b, h, i: (b, i, h)),
        out_shape=jax.ShapeDtypeStruct((bsz, s, DIFF_HEADS * DIFF_V), BF16),
        compiler_params=pltpu.CompilerParams(
            dimension_semantics=("arbitrary", "arbitrary", "arbitrary"), vmem_limit_bytes=VMEM_LIMIT),
        name="diff_attention",
    )(slopes, proj, proj, proj, lam_vecs, subln_g.reshape(1, DIFF_V))


def _swa_kernel(slopes_ref, sinks_ref, q_ref, k_ref, v_ref, o_ref, *, tq):
    i = pl.program_id(1)
    scale = HEAD_DIM ** -0.5
    blk = WINDOW
    ii = lax.broadcasted_iota(I32, (blk, 2 * blk), 0)
    jj = lax.broadcasted_iota(I32, (blk, 2 * blk), 1)
    for r in range(tq // blk):
        start = i * tq + r * blk
        kstart = jnp.maximum(start - blk, 0)
        kb = k_ref[pl.ds(pl.multiple_of(kstart, blk), 2 * blk), :]
        vb = v_ref[pl.ds(pl.multiple_of(kstart, blk), 2 * blk), :]
        dist = (start + ii) - (kstart + jj)
        valid = (dist >= 0) & (dist < WINDOW)
        distf = dist.astype(F32)
        outs = []
        for kvh in range(SWA_KV_HEADS):
            k = kb[:, kvh * HEAD_DIM:(kvh + 1) * HEAD_DIM]
            v = vb[:, kvh * HEAD_DIM:(kvh + 1) * HEAD_DIM]
            for g in range(SWA_GROUP):
                hq = kvh * SWA_GROUP + g
                qh = q_ref[r * blk:(r + 1) * blk, hq * HEAD_DIM:(hq + 1) * HEAD_DIM]
                s = _nt_dot(qh, k) * scale - slopes_ref[hq] * distf
                s = jnp.where(valid, s, NEG_INF)
                sink = sinks_ref[hq]
                m = jnp.maximum(jnp.max(s, axis=-1, keepdims=True), sink)
                p = jnp.exp(s - m)
                denom = jnp.sum(p, axis=-1, keepdims=True) + jnp.exp(sink - m)
                outs.append(jnp.dot(p.astype(BF16), v, preferred_element_type=F32) / denom)
        o_ref[r * blk:(r + 1) * blk, :] = jnp.concatenate(outs, axis=-1).astype(BF16)


def _swa_attention(proj, slopes, sinks):
    bsz, s, _ = proj.shape
    tq = ATTN_TILE
    width = SWA_Q_HEADS * HEAD_DIM
    qcol = (3 * DIFF_HEADS * DIFF_V) // width
    kcol = (3 * DIFF_HEADS * DIFF_V + width) // LANES
    return pl.pallas_call(
        functools.partial(_swa_kernel, tq=tq),
        grid=(bsz, s // tq),
        in_specs=[
            pl.BlockSpec(memory_space=pltpu.SMEM),
            pl.BlockSpec(memory_space=pltpu.SMEM),
            pl.BlockSpec((None, tq, width), lambda b, i: (b, i, qcol)),
            pl.BlockSpec((None, s, LANES), lambda b, i: (b, 0, kcol)),
            pl.BlockSpec((None, s, LANES), lambda b, i: (b, 0, kcol + 1)),
        ],
        out_specs=pl.BlockSpec((None, tq, width), lambda b, i: (b, i, 0)),
        out_shape=jax.ShapeDtypeStruct((bsz, s, width), BF16),
        compiler_params=pltpu.CompilerParams(
            dimension_semantics=("arbitrary", "arbitrary"), vmem_limit_bytes=VMEM_LIMIT),
        name="swa_attention",
    )(slopes, sinks, proj, proj, proj)


def _layer_norm(y, g, b):
    mu = jnp.mean(y, axis=-1, keepdims=True)
    yc = y - mu
    var = jnp.mean(yc * yc, axis=-1, keepdims=True)
    return yc * lax.rsqrt(var + LN_EPS) * g + b


def _pack_pairs(lo, hi):
    lo_bits = lax.bitcast_convert_type(lo.astype(F32), I32)
    hi_bits = lax.bitcast_convert_type(hi.astype(F32), I32)
    return lax.shift_right_logical(lo_bits, 16) | (hi_bits & jnp.int32(HI_HALF))


def _mid_kernel(do_ref, so_ref, x_ref, mod_ref, wo_ref, ln_ref, wpq_ref, x1_ref, h2_ref, q_ref, *, alpha):
    nd = do_ref.shape[-1]
    mixed = (jnp.dot(do_ref[...], wo_ref[:nd, :], preferred_element_type=F32)
             + jnp.dot(so_ref[...], wo_ref[nd:, :], preferred_element_type=F32))
    y = alpha * x_ref[...] + mod_ref[2:3, :] * mixed
    x1 = _layer_norm(y, ln_ref[0:1, :], ln_ref[1:2, :])
    x1_ref[...] = x1
    h2 = (x1 * (1.0 + mod_ref[4:5, :]) + mod_ref[3:4, :]).astype(BF16)
    half = h2.shape[-1] // 2
    h2_ref[...] = _pack_pairs(h2[:, :half], h2[:, half:])
    q_ref[...] = jnp.dot(h2, wpq_ref[...], preferred_element_type=F32).astype(BF16)


def _mid(diff_out, swa_out, x, mod3, wo_bf16, ln1, wpq_bf16, alpha, b0):
    bsz = diff_out.shape[0]
    _, s, d = x.shape
    nq = wpq_bf16.shape[1]
    tm = ROW_TILE
    row = lambda b, i: (b, i, 0)
    const = lambda b, i: (0, 0)
    return pl.pallas_call(
        functools.partial(_mid_kernel, alpha=alpha),
        grid=(bsz, s // tm),
        in_specs=[
            pl.BlockSpec((None, tm, diff_out.shape[-1]), row),
            pl.BlockSpec((None, tm, swa_out.shape[-1]), row),
            pl.BlockSpec((None, tm, d), lambda b, i: (b0 + b, i, 0)),
            pl.BlockSpec((None, 6, d), lambda b, i: (b0 + b, 0, 0)),
            pl.BlockSpec(wo_bf16.shape, const),
            pl.BlockSpec((2, d), const),
            pl.BlockSpec(wpq_bf16.shape, const),
        ],
        out_specs=[
            pl.BlockSpec((None, tm, d), row),
            pl.BlockSpec((None, tm, d // 2), row),
            pl.BlockSpec((None, tm, nq), row),
        ],
        out_shape=[
            jax.ShapeDtypeStruct((bsz, s, d), F32),
            jax.ShapeDtypeStruct((bsz, s, d // 2), I32),
            jax.ShapeDtypeStruct((bsz, s, nq), BF16),
        ],
        compiler_params=pltpu.CompilerParams(
            dimension_semantics=("arbitrary", "arbitrary"), vmem_limit_bytes=VMEM_LIMIT),
        name="outproj_ln1_peerq",
    )(diff_out, swa_out, x, mod3, wo_bf16, ln1, wpq_bf16)


def _topk_rows(vals, pos, payload, k):
    out_v, out_p = [], []
    for _ in range(k):
        m = jnp.max(vals, axis=0, keepdims=True)
        first = jnp.min(jnp.where(vals == m, pos, jnp.inf), axis=0, keepdims=True)
        sel = pos == first
        if payload is None:
            out_p.append(first)
        else:
            out_p.append(jnp.max(jnp.where(sel, payload, -1.0), axis=0, keepdims=True))
        out_v.append(m)
        vals = jnp.where(sel, -jnp.inf, vals)
    return jnp.concatenate(out_v, axis=0), jnp.concatenate(out_p, axis=0)


def _candidates(v0, i0, v1, i1):
    k, lanes = v0.shape
    vals, poss, eids = [], [], []
    for a in range(4):
        nb = k if a == 0 else k // 2
        b_iota = lax.broadcasted_iota(I32, (nb, lanes), 0).astype(F32)
        vals.append(v0[a:a + 1, :] + v1[:nb, :])
        poss.append(a * k + b_iota)
        eids.append(i0[a:a + 1, :] * N_KEYS + i1[:nb, :])
    for b in range(3):
        na = k if b == 0 else k // 2
        a_iota = lax.broadcasted_iota(I32, (na, lanes), 0).astype(F32)
        vals.append(jnp.where(a_iota >= 4.0, v0[:na, :] + v1[b:b + 1, :], -jnp.inf))
        poss.append(a_iota * k + b)
        eids.append(i0[:na, :] * N_KEYS + i1[b:b + 1, :])
    return jnp.concatenate(vals, axis=0), jnp.concatenate(poss, axis=0), jnp.concatenate(eids, axis=0)


def _route_kernel(q_ref, keys_ref, idx_ref, gate_ref, idx_t, gate_t):
    tt = q_ref.shape[0]
    key_pos = lax.broadcasted_iota(I32, (N_KEYS, tt), 0).astype(F32)

    def head(h, carry):
        halves = []
        for p in range(2):
            qh = q_ref[:, pl.ds(pl.multiple_of((2 * h + p) * PEER_HALF, PEER_HALF), PEER_HALF)]
            sc = _nt_dot(keys_ref[h, p], qh)
            halves.append(_topk_rows(sc, key_pos, None, PEER_TOPK))
        (v0, i0), (v1, i1) = halves
        cv, cp, ce = _candidates(v0, i0, v1, i1)
        top_s, top_e = _topk_rows(cv, cp, ce, PEER_TOPK)
        e = jnp.exp(top_s - top_s[0:1, :])
        gate = e / jnp.sum(e, axis=0, keepdims=True)
        rows = pl.ds(pl.multiple_of(h * PEER_TOPK, PEER_TOPK), PEER_TOPK)
        idx_t[rows, :] = top_e
        gate_t[rows, :] = gate
        return carry

    lax.fori_loop(0, PEER_HEADS, head, 0)
    idx_ref[...] = idx_t[...].T.astype(I32)
    gate_ref[...] = gate_t[...].T


def _route(q2d, keys_bf16):
    t, nq = q2d.shape
    tt = ROUTE_TILE
    return pl.pallas_call(
        _route_kernel,
        grid=(t // tt,),
        in_specs=[
            pl.BlockSpec((tt, nq), lambda i: (i, 0)),
            pl.BlockSpec(keys_bf16.shape, lambda i: (0, 0, 0, 0)),
        ],
        out_specs=[
            pl.BlockSpec((tt, PEER_SLOTS), lambda i: (i, 0)),
            pl.BlockSpec((tt, PEER_SLOTS), lambda i: (i, 0)),
        ],
        out_shape=[
            jax.ShapeDtypeStruct((t, PEER_SLOTS), I32),
            jax.ShapeDtypeStruct((t, PEER_SLOTS), F32),
        ],
        scratch_shapes=[pltpu.VMEM((PEER_SLOTS, tt), F32), pltpu.VMEM((PEER_SLOTS, tt), F32)],
        compiler_params=pltpu.CompilerParams(dimension_semantics=("arbitrary",), vmem_limit_bytes=VMEM_LIMIT),
        name="peer_route",
    )(q2d, keys_bf16)


def _pack_table_kernel(t_ref, o_ref):
    half = o_ref.shape[-1]
    x = t_ref[...].astype(BF16)
    o_ref[...] = _pack_pairs(x[:, :half], x[:, half:])


def _pack_table(tab):
    v, d = tab.shape
    tm = math.gcd(v, 2 * ROW_TILE)
    return pl.pallas_call(
        _pack_table_kernel,
        grid=(v // tm,),
        in_specs=[pl.BlockSpec((tm, d), lambda i: (i, 0))],
        out_specs=pl.BlockSpec((tm, d // 2), lambda i: (i, 0)),
        out_shape=jax.ShapeDtypeStruct((v, d // 2), I32),
        compiler_params=pltpu.CompilerParams(dimension_semantics=("arbitrary",), vmem_limit_bytes=VMEM_LIMIT),
        name="pack_table",
    )(tab)


SC_LANES = 16
RING_ROWS = 32
RING_AHEAD = 3
UDOT_GROUP = 16


BF16_GROUP = 4


def _widen_pair_sum(acc, packed_bf16):
    bits = plsc.bitcast(packed_bf16, I32)
    lo = lax.bitcast_convert_type(bits << 16, F32)
    hi = lax.bitcast_convert_type(bits & jnp.int32(HI_HALF), F32)
    return acc + lo + hi


def _udot_rows(rows_ref, h_ref, out_ref, out_base):
    nrows, width = rows_ref.shape
    span = BF16_GROUP * SC_LANES
    lane = lax.iota(I32, SC_LANES)

    @pl.loop(0, nrows // UDOT_GROUP)
    def _(g):
        r0 = g * UDOT_GROUP

        def chunk(c, accs):
            off = pl.multiple_of(c * span, span)
            hs = [plsc.bitcast(h_ref[pl.ds(off + i * SC_LANES, SC_LANES)], BF16) for i in range(BF16_GROUP)]
            out = []
            for r in range(UDOT_GROUP):
                part = None
                for i in range(BF16_GROUP):
                    w = plsc.bitcast(rows_ref[r0 + r, pl.ds(off + i * SC_LANES, SC_LANES)], BF16)
                    part = w * hs[i] if part is None else part + w * hs[i]
                out.append(_widen_pair_sum(accs[r], part))
            return tuple(out)

        zero = jnp.zeros((SC_LANES,), F32)
        accs = lax.fori_loop(0, width // span, chunk, (zero,) * UDOT_GROUP)
        vec = zero
        for r in range(UDOT_GROUP):
            vec = jnp.where(lane == r, jnp.sum(accs[r]), vec)
        out_ref[pl.ds(pl.multiple_of(out_base + r0, UDOT_GROUP), UDOT_GROUP)] = vec


def _sc_udot(table, idx, h):
    t, k = idx.shape
    width = table.shape[1]
    tpw = t // SC_WORKERS
    slots = k // RING_ROWS
    assert tpw * SC_WORKERS == t and tpw % 2 == 0 and slots == RING_AHEAD + 1
    nbuf = tpw * slots
    mesh = plsc.VectorSubcoreMesh(core_axis_name="c", subcore_axis_name="s")

    def body(table_hbm, idx_hbm, h_hbm, out_hbm, idx_v, rows_v, h_v, a_v, gsem, hsem):
        wid = lax.axis_index("s") * SC_CORES + lax.axis_index("c")
        tok0 = wid * tpw
        pltpu.sync_copy(idx_hbm.at[wid], idx_v)

        def h_slot(slot):
            return h_v.at[pl.ds(slot * width, width)]

        def gather(j, slot):
            rows = idx_v.at[pl.ds(pl.multiple_of(j * RING_ROWS, RING_ROWS), RING_ROWS)]
            return pltpu.make_async_copy(table_hbm.at[rows], rows_v.at[slot], gsem.at[slot])

        def h_copy(tok, slot):
            return pltpu.make_async_copy(h_hbm.at[tok0 + tok], h_slot(slot), hsem.at[slot])

        for j in range(RING_AHEAD):
            gather(j, j).start()
        h_copy(0, 0).start()

        @pl.loop(0, tpw, step=2)
        def _(t0):
            for hs in range(2):
                tok = t0 + hs
                h_copy(tok, hs).wait()

                @pl.when(tok + 1 < tpw)
                def _():
                    h_copy(tok + 1, 1 - hs).start()

                for slot in range(slots):
                    j = tok * slots + slot
                    gather(j, slot).wait()

                    @pl.when(j + RING_AHEAD < nbuf)
                    def _():
                        gather(j + RING_AHEAD, (slot + RING_AHEAD) % slots).start()

                    _udot_rows(rows_v.at[slot], h_slot(hs), a_v, tok * k + slot * RING_ROWS)

        pltpu.sync_copy(a_v, out_hbm.at[pl.ds(pl.multiple_of(tok0 * k, 8), tpw * k)])

    out = pl.kernel(
        body,
        out_type=jax.ShapeDtypeStruct((t * k,), F32),
        mesh=mesh,
        scratch_types=[
            pltpu.VMEM((nbuf * RING_ROWS,), I32),
            pltpu.VMEM((slots, RING_ROWS, width), table.dtype),
            pltpu.VMEM((2 * width,), I32),
            pltpu.VMEM((tpw * k,), F32),
            pltpu.SemaphoreType.DMA((slots,)),
            pltpu.SemaphoreType.DMA((2,)),
        ],
        compiler_params=pltpu.CompilerParams(needs_layout_passes=False),
        name="peer_udot",
    )(table, idx.reshape(SC_WORKERS, nbuf * RING_ROWS), h)
    return out.reshape(t, k)


VSUM_CHUNKS = 8


def _vsum_rows(rows_ref, wgt_ref, wgt_base, out_ref, out_base, first):
    nrows, width = rows_ref.shape
    span = VSUM_CHUNKS * SC_LANES

    @pl.loop(0, width // span)
    def _(blk):
        col0 = pl.multiple_of(blk * span, span)

        def row_group(rg, accs):
            r0 = rg * BF16_GROUP
            parts = [None] * VSUM_CHUNKS
            wvec = wgt_ref[pl.ds(pl.multiple_of(wgt_base + (r0 // SC_LANES) * SC_LANES, SC_LANES), SC_LANES)]
            for i in range(BF16_GROUP):
                lane = jnp.full((SC_LANES,), r0 % SC_LANES + i, I32)
                wv = plsc.bitcast(wvec.at[lane].get(mode="promise_in_bounds"), BF16)
                for c in range(VSUM_CHUNKS):
                    w = plsc.bitcast(rows_ref[r0 + i, pl.ds(col0 + c * SC_LANES, SC_LANES)], BF16)
                    parts[c] = w * wv if parts[c] is None else parts[c] + w * wv
            out = []
            for c in range(VSUM_CHUNKS):
                bits = plsc.bitcast(parts[c], I32)
                out.append(accs[2 * c] + lax.bitcast_convert_type(bits << 16, F32))
                out.append(accs[2 * c + 1] + lax.bitcast_convert_type(bits & jnp.int32(HI_HALF), F32))
            return tuple(out)

        zero = jnp.zeros((SC_LANES,), F32)
        accs = lax.fori_loop(0, nrows // BF16_GROUP, row_group, (zero,) * (2 * VSUM_CHUNKS))
        for c in range(VSUM_CHUNKS):
            for half in range(2):
                dst = pl.ds(pl.multiple_of(out_base + half * width + col0 + c * SC_LANES, SC_LANES), SC_LANES)
                if first:
                    out_ref[dst] = accs[2 * c + half]
                else:
                    out_ref[dst] = out_ref[dst] + accs[2 * c + half]


def _sc_vsum(table, idx, wgt):
    t, k = idx.shape
    width = table.shape[1]
    d = 2 * width
    tpw = t // SC_WORKERS
    slots = k // RING_ROWS
    assert tpw * SC_WORKERS == t and tpw % 2 == 0 and slots == RING_AHEAD + 1
    nbuf = tpw * slots
    mesh = plsc.VectorSubcoreMesh(core_axis_name="c", subcore_axis_name="s")

    def body(table_hbm, idx_hbm, wgt_hbm, out_hbm, idx_v, rows_v, wgt_v, out_v, gsem, osem):
        wid = lax.axis_index("s") * SC_CORES + lax.axis_index("c")
        tok0 = wid * tpw
        pltpu.sync_copy(idx_hbm.at[wid], idx_v)
        pltpu.sync_copy(wgt_hbm.at[wid], wgt_v)

        def gather(j, slot):
            rows = idx_v.at[pl.ds(pl.multiple_of(j * RING_ROWS, RING_ROWS), RING_ROWS)]
            return pltpu.make_async_copy(table_hbm.at[rows], rows_v.at[slot], gsem.at[slot])

        def put(tok, slot):
            return pltpu.make_async_copy(out_v.at[pl.ds(slot * d, d)], out_hbm.at[tok0 + tok], osem.at[slot])

        for j in range(RING_AHEAD):
            gather(j, j).start()

        @pl.loop(0, tpw, step=2)
        def _(t0):
            for os_ in range(2):
                tok = t0 + os_

                @pl.when(tok >= 2)
                def _():
                    put(tok - 2, os_).wait()

                for slot in range(slots):
                    j = tok * slots + slot
                    gather(j, slot).wait()

                    @pl.when(j + RING_AHEAD < nbuf)
                    def _():
                        gather(j + RING_AHEAD, (slot + RING_AHEAD) % slots).start()

                    _vsum_rows(rows_v.at[slot], wgt_v, tok * k + slot * RING_ROWS, out_v, os_ * d, slot == 0)
                put(tok, os_).start()

        put(tpw - 2, 0).wait()
        put(tpw - 1, 1).wait()

    return pl.kernel(
        body,
        out_type=jax.ShapeDtypeStruct((t, d), F32),
        mesh=mesh,
        scratch_types=[
            pltpu.VMEM((nbuf * RING_ROWS,), I32),
            pltpu.VMEM((slots, RING_ROWS, width), table.dtype),
            pltpu.VMEM((tpw * k,), I32),
            pltpu.VMEM((2 * d,), F32),
            pltpu.SemaphoreType.DMA((slots,)),
            pltpu.SemaphoreType.DMA((2,)),
        ],
        compiler_params=pltpu.CompilerParams(needs_layout_passes=False),
        name="peer_vsum",
    )(table, idx.reshape(SC_WORKERS, nbuf * RING_ROWS), wgt.reshape(SC_WORKERS, tpw * k))


def _wgt_kernel(a_ref, gate_ref, o_ref):
    a = a_ref[...]
    w = (gate_ref[...] * (0.5 * a * (1.0 + lax.erf(a * (2.0 ** -0.5))))).astype(BF16)
    o_ref[...] = _pack_pairs(w, w)


def _expert_weights(a, gate):
    t, k = a.shape
    tm = math.gcd(t, 1024)
    spec = pl.BlockSpec((tm, k), lambda i: (i, 0))
    return pl.pallas_call(
        _wgt_kernel,
        grid=(t // tm,),
        in_specs=[spec, spec],
        out_specs=spec,
        out_shape=jax.ShapeDtypeStruct((t, k), I32),
        compiler_params=pltpu.CompilerParams(dimension_semantics=("arbitrary",), vmem_limit_bytes=VMEM_LIMIT),
        name="peer_weights",
    )(a, gate)


def _final_kernel(x1_ref, ffn_ref, mod_ref, ln_ref, o_ref, *, alpha):
    y = alpha * x1_ref[...] + mod_ref[5:6, :] * ffn_ref[...]
    o_ref[...] = _layer_norm(y, ln_ref[0:1, :], ln_ref[1:2, :])


def _final(x1, ffn, mod3, ln2, alpha, b0):
    nb, s, d = x1.shape
    tm = ROW_TILE
    row = lambda b, i: (b, i, 0)
    return pl.pallas_call(
        functools.partial(_final_kernel, alpha=alpha),
        grid=(nb, s // tm),
        in_specs=[
            pl.BlockSpec((None, tm, d), row),
            pl.BlockSpec((None, tm, d), row),
            pl.BlockSpec((None, 6, d), lambda b, i: (b0 + b, 0, 0)),
            pl.BlockSpec((2, d), lambda b, i: (0, 0)),
        ],
        out_specs=pl.BlockSpec((None, tm, d), row),
        out_shape=jax.ShapeDtypeStruct((nb, s, d), F32),
        compiler_params=pltpu.CompilerParams(
            dimension_semantics=("arbitrary", "arbitrary"), vmem_limit_bytes=VMEM_LIMIT),
        name="deepnorm_ln2",
    )(x1, ffn, mod3, ln2)


CHUNK_BATCHES = 1
CHUNK_LAG = 6
STAGE_LAG = 2


def _layer_chunk(x, b0, nb, mod3, slopes, lam_vecs, lambda_init, alpha, w, prev_wgt):
    _, seq, d = x.shape
    proj = _inproj(x, mod3, w["w_in"], b0, nb)
    diff_out = _diff_attention(proj, slopes, lam_vecs, w["subln_g"], lambda_init)
    swa_out = _swa_attention(proj, slopes, w["sinks"])
    x1, h2, q = _mid(diff_out, swa_out, x, mod3, w["w_out"], w["ln1"], w["w_pq"], alpha, b0)
    idx, gate = _route(q.reshape(nb * seq, -1), w["sub_keys"])
    if prev_wgt is not None:
        idx, _ = lax.optimization_barrier((idx, prev_wgt))
    a = _sc_udot(w["u_pack"], idx, h2.reshape(nb * seq, d // 2))
    wgt = _expert_weights(a, gate)
    ffn = _sc_vsum(w["v_pack"], idx, wgt)
    return _final(x1, ffn.reshape(nb, seq, d), mod3, w["ln2"], alpha, b0), wgt


def kernel(x, c, w_ada, b_ada, w_in, lambda_q1, lambda_k1, lambda_q2, lambda_k2, subln_g, sinks, w_out, ln1_g, ln1_b, w_pq, sub_keys, u_tab, v_tab, ln2_g, ln2_b):
    bsz, seq, d = x.shape
    depth = w_ada.shape[0]
    alpha = (2 * depth) ** 0.25
    slopes = jnp.exp2(-8.0 * jnp.arange(1, N_ATT_HEADS + 1, dtype=F32) / N_ATT_HEADS)
    nb = CHUNK_BATCHES
    for l in range(depth):
        lambda_init = 0.8 - 0.6 * math.exp(-0.3 * l)
        mod3 = _mod(c, w_ada[l], b_ada[l]).reshape(bsz, 6, d)
        lam_vecs = jnp.stack([lambda_q1[l], lambda_k1[l], lambda_q2[l], lambda_k2[l]])
        w = dict(w_in=w_in[l].astype(BF16), subln_g=subln_g[l], sinks=sinks[l], w_out=w_out[l].astype(BF16),
                 ln1=jnp.stack([ln1_g[l], ln1_b[l]]), w_pq=w_pq[l].astype(BF16),
                 sub_keys=sub_keys[l].astype(BF16), u_pack=_pack_table(u_tab[l]), v_pack=_pack_table(v_tab[l]),
                 ln2=jnp.stack([ln2_g[l], ln2_b[l]]))
        outs, wgts = [], []
        for ci, b0 in enumerate(range(0, bsz, nb)):
            mod_c = mod3
            if ci >= CHUNK_LAG:
                mod_c, outs[ci - CHUNK_LAG] = lax.optimization_barrier((mod3, outs[ci - CHUNK_LAG]))
            prev_wgt = wgts[ci - STAGE_LAG] if ci >= STAGE_LAG else None
            out, wgt = _layer_chunk(x, b0, nb, mod_c, slopes, lam_vecs, lambda_init, alpha, w, prev_wgt)
            outs.append(out)
            wgts.append(wgt)
        x = jnp.concatenate(outs, axis=0).reshape(bsz, seq, d)
    return x
```
